```python
import math
import jax
import jax.numpy as jnp
from jax import lax
import numpy as np

D_MODEL = 1024
BATCH = 4
SEQ = 8192
DEPTH = 4

GRID_W = 64
CTX_LEN = 256
EPS = 1e-6
ROPE_THETA = 10000.0
ROT_DIM = 32
Q_BLOCK = 128

DA_HEADS = 6
DA_DIM = ROT_DIM
DA_VDIM = 2 * DA_DIM
DA_W = DA_HEADS * DA_VDIM

MLA_HEADS = 6
MLA_Q_RANK = 256
MLA_KV_RANK = 128
MLA_NOPE = 64
MLA_ROPE = ROT_DIM
MLA_VDIM = 64
MLA_W = MLA_HEADS * MLA_VDIM

HG_HEADS = 4
HG_KDIM = 128
HG_VDIM = 64
HG_CHUNK = 64
HG_KW = HG_HEADS * HG_KDIM
HG_VW = HG_HEADS * HG_VDIM
FORGET_FLOOR = 1e-30

D_MIX = DA_W + MLA_W + HG_VW
D_IN = 3 * DA_W + MLA_Q_RANK + MLA_KV_RANK + MLA_ROPE + 3 * HG_KW + 2 * HG_VW
D_FF = 4 * D_MODEL
N_MOD = 6

kernel_name = 'hybrid_parallel_group_dit_block'


def rmsnorm(x, g):
    xf = x.astype(jnp.float32)
    y = xf * lax.rsqrt(jnp.mean(jnp.square(xf), axis=-1, keepdims=True) + EPS)
    return (y * g.astype(jnp.float32)).astype(x.dtype)


def ada_norm(x, g, shift, scale):
    return rmsnorm(x, g) * (1 + scale[..., None, :]) + shift[..., None, :]


def axial_rope_tables(n):
    rows = n // GRID_W
    row = jnp.repeat(jnp.arange(rows, dtype=jnp.float32), GRID_W)
    col = jnp.tile(jnp.arange(GRID_W, dtype=jnp.float32), rows)
    n_freq = ROT_DIM // 4
    inv = ROPE_THETA ** (-jnp.arange(n_freq, dtype=jnp.float32) / n_freq)
    ang = jnp.stack([row[:, None] * inv, col[:, None] * inv], axis=1)
    return jnp.cos(ang), jnp.sin(ang)


def apply_rope(x, cos, sin):
    shp = x.shape
    xr = x.astype(jnp.float32).reshape(shp[:-1] + (2, 2, shp[-1] // 4))
    cc = cos[None, :, None]
    ss = sin[None, :, None]
    x1 = xr[..., 0, :]
    x2 = xr[..., 1, :]
    out = jnp.stack([x1 * cc - x2 * ss, x2 * cc + x1 * ss], axis=-2)
    return out.reshape(shp).astype(x.dtype)


def project(h, w_in, w_uq, w_ukv, g_cq, g_ckv, rope):
    bsz, n = h.shape[0], h.shape[1]
    p = jnp.einsum('bnd,de->bne', h, w_in)
    sizes = (DA_W, DA_W, DA_W, MLA_Q_RANK, MLA_KV_RANK, MLA_ROPE, HG_KW, HG_KW, HG_KW, HG_VW, HG_VW)
    idx = []
    acc = 0
    for s in sizes[:-1]:
        acc += s
        idx.append(acc)
    da_q, da_k, da_v, cq, ckv, kr, hq, hzf, hzb, hv, hgate = jnp.split(p, idx, axis=-1)
    da_q = da_q.reshape(bsz, n, DA_HEADS * 2, DA_DIM)
    da_k = da_k.reshape(bsz, n, DA_HEADS * 2, DA_DIM)
    qu = (rmsnorm(cq, g_cq) @ w_uq).reshape(bsz, n, MLA_HEADS, MLA_NOPE + MLA_ROPE)
    kvu = (rmsnorm(ckv, g_ckv) @ w_ukv).reshape(bsz, n, MLA_HEADS, MLA_NOPE + MLA_VDIM)
    q_nope, q_rope = qu[..., :MLA_NOPE], qu[..., MLA_NOPE:]
    k_nope, mla_v = kvu[..., :MLA_NOPE], kvu[..., MLA_NOPE:]
    kr = kr[:, :, None, :]
    if rope is not None:
        cos, sin = rope
        da_q = apply_rope(da_q, cos, sin)
        da_k = apply_rope(da_k, cos, sin)
        q_rope = apply_rope(q_rope, cos, sin)
        kr = apply_rope(kr, cos, sin)
    return {
        'da_q': da_q.reshape(bsz, n, DA_HEADS, 2, DA_DIM),
        'da_k': da_k.reshape(bsz, n, DA_HEADS, 2, DA_DIM),
        'da_v': da_v.reshape(bsz, n, DA_HEADS, DA_VDIM),
        'q_nope': q_nope, 'q_rope': q_rope, 'k_nope': k_nope, 'k_rope': kr[:, :, 0], 'mla_v': mla_v,
        'hg_q': hq.reshape(bsz, n, HG_HEADS, HG_KDIM),
        'hg_zf': hzf.reshape(bsz, n, HG_HEADS, HG_KDIM),
        'hg_zb': hzb.reshape(bsz, n, HG_HEADS, HG_KDIM),
        'hg_v': hv.reshape(bsz, n, HG_HEADS, HG_VDIM),
        'hg_gate': hgate.reshape(bsz, n, HG_HEADS, HG_VDIM),
    }


def sweep_query_blocks(fn, *q_arrays):
    n = q_arrays[0].shape[1]
    nb = n // Q_BLOCK
    blocks = tuple(a.reshape((a.shape[0], nb, Q_BLOCK) + a.shape[2:]).swapaxes(0, 1) for a in q_arrays)
    out = lax.map(lambda qs: fn(*qs), blocks)
    out = out.swapaxes(0, 1)
    return out.reshape((out.shape[0], n) + out.shape[3:])


def diff_attn_core(q, k, v, lam):
    s = jnp.einsum('bqhcd,bkhcd->bhcqk', q, k).astype(jnp.float32) * (DA_DIM ** -0.5)
    p = jax.nn.softmax(s, axis=-1)
    w = p[:, :, 0] - lam * p[:, :, 1]
    return jnp.einsum('bhqk,bkhe->bqhe', w.astype(v.dtype), v)


def da_post(o, g, lam_init):
    o = rmsnorm(o, g) * (1.0 - lam_init)
    return o.reshape(o.shape[:2] + (-1,))


def mla_core(q_nope, q_rope, k_nope, k_rope, v):
    s = (jnp.einsum('bqhd,bkhd->bhqk', q_nope, k_nope)
         + jnp.einsum('bqhr,bkr->bhqk', q_rope, k_rope)).astype(jnp.float32) * ((MLA_NOPE + MLA_ROPE) ** -0.5)
    p = jax.nn.softmax(s, axis=-1)
    return jnp.einsum('bhqk,bkhd->bqhd', p.astype(v.dtype), v)


def forget_gate(z, lb):
    lb = lb.reshape(HG_HEADS, HG_KDIM)
    zf = z.astype(jnp.float32)
    f = lb + (1.0 - lb) * jax.nn.sigmoid(zf)
    key = (1.0 - lb) * jax.nn.sigmoid(-zf)
    return jnp.log(jnp.maximum(f, FORGET_FLOOR)), key


def hgrn2_chunk_scan(q, k, v, logf, s0):
    bsz, n = q.shape[0], q.shape[1]
    nc = n // HG_CHUNK

    def to_chunks(a):
        return a.astype(jnp.float32).reshape(bsz, nc, HG_CHUNK, HG_HEADS, a.shape[-1]).transpose(1, 0, 3, 2, 4)

    mask = jnp.tril(jnp.ones((HG_CHUNK, HG_CHUNK), dtype=bool))[:, :, None]

    def step(S, xs):
        qc, kc, vc, gc = xs
        b = jnp.cumsum(gc, axis=-2)
        o_inter = jnp.einsum('bhtk,bhkv->bhtv', qc * jnp.exp(b), S)
        rel = jnp.where(mask, b[..., :, None, :] - b[..., None, :, :], 0.0)
        decay = jnp.where(mask, jnp.exp(rel), 0.0)
        A = jnp.einsum('bhtsk,bhsk->bhts', qc[..., :, None, :] * decay, kc)
        o = o_inter + jnp.einsum('bhts,bhsv->bhtv', A, vc)
        b_last = b[..., -1:, :]
        S_new = jnp.exp(b_last[..., 0, :])[..., None] * S + jnp.einsum('bhsk,bhsv->bhkv', kc * jnp.exp(b_last - b), vc)
        return S_new, o

    S, o = lax.scan(step, s0, (to_chunks(q), to_chunks(k), to_chunks(v), to_chunks(logf)))
    o = o.transpose(1, 0, 3, 2, 4).reshape(bsz, n, HG_HEADS, v.shape[-1])
    return o, S


def hgrn2_bidir(p, lb_f, lb_b, s0_f, s0_b):
    lf_f, k_f = forget_gate(p['hg_zf'], lb_f)
    lf_b, k_b = forget_gate(p['hg_zb'], lb_b)
    q, v = p['hg_q'], p['hg_v']
    o_f, s_f = hgrn2_chunk_scan(q, k_f, v, lf_f, s0_f)
    flip = lambda a: a[:, ::-1]
    o_b, s_b = hgrn2_chunk_scan(flip(q), flip(k_b), flip(v), flip(lf_b), s0_b)
    return (o_f + flip(o_b)).astype(q.dtype), s_f, s_b


def hgrn2_out(o, p, g):
    o = rmsnorm(o, g) * jax.nn.silu(p['hg_gate'])
    return o.reshape(o.shape[:2] + (-1,))


def sqrelu_mlp(h, w1, w2):
    return jnp.square(jax.nn.relu(h @ w1)) @ w2


def setup_inputs(seed: int = 0) -> dict:
    key = jax.random.key(seed)
    ks = jax.random.split(key, 24)
    f32 = jnp.float32

    def nrm(k, shape, scale):
        return jax.random.normal(k, shape, f32) * scale

    return {
        'x': nrm(ks[0], (BATCH, SEQ, D_MODEL), 1.0),
        'c': nrm(ks[1], (BATCH, D_MODEL), 1.0),
        'ctx': nrm(ks[2], (BATCH, CTX_LEN, D_MODEL), 1.0),
        'c_ctx': nrm(ks[3], (D_MODEL,), 1.0),
        'w_mod': nrm(ks[4], (DEPTH, D_MODEL, N_MOD * D_MODEL), 0.5 * D_MODEL ** -0.5),
        'b_mod': nrm(ks[5], (DEPTH, N_MOD * D_MODEL), 0.01),
        'g_mix': 1.0 + nrm(ks[6], (DEPTH, D_MODEL), 0.05),
        'g_mlp': 1.0 + nrm(ks[7], (DEPTH, D_MODEL), 0.05),
        'w_in': nrm(ks[8], (DEPTH, D_MODEL, D_IN), D_MODEL ** -0.5),
        'w_out': nrm(ks[9], (DEPTH, D_MIX, D_MODEL), D_MIX ** -0.5),
        'da_lambda': nrm(ks[10], (DEPTH, 4, DA_DIM), 0.1),
        'da_subln_g': 1.0 + nrm(ks[11], (DEPTH, DA_VDIM), 0.05),
        'mla_g_cq': 1.0 + nrm(ks[12], (DEPTH, MLA_Q_RANK), 0.05),
        'mla_g_ckv': 1.0 + nrm(ks[13], (DEPTH, MLA_KV_RANK), 0.05),
        'mla_w_uq': nrm(ks[14], (DEPTH, MLA_Q_RANK, MLA_HEADS * (MLA_NOPE + MLA_ROPE)), MLA_Q_RANK ** -0.5),
        'mla_w_ukv': nrm(ks[15], (DEPTH, MLA_KV_RANK, MLA_HEADS * (MLA_NOPE + MLA_VDIM)), MLA_KV_RANK ** -0.5),
        'hg_lower_bounds': 1.0 + nrm(ks[16], (2, DEPTH, HG_KW), 0.1),
        'hg_norm_g': 1.0 + nrm(ks[17], (DEPTH, HG_VDIM), 0.05),
        'w_ff1': nrm(ks[18], (DEPTH, D_MODEL, D_FF), D_MODEL ** -0.5),
        'w_ff2': nrm(ks[19], (DEPTH, D_FF, D_MODEL), D_FF ** -0.5),
        'g_final': 1.0 + nrm(ks[20], (D_MODEL,), 0.05),
    }


def reference(x, c, ctx, c_ctx, w_mod, b_mod, g_mix, g_mlp, w_in, w_out, da_lambda, da_subln_g,
              mla_g_cq, mla_g_ckv, mla_w_uq, mla_w_ukv, hg_lower_bounds, hg_norm_g, w_ff1, w_ff2, g_final):
    bsz, n = x.shape[0], x.shape[1]
    rope = axial_rope_tables(n)
    lb = jax.nn.softmax(hg_lower_bounds.astype(jnp.float32), axis=1)
    lb = jnp.cumsum(lb, axis=1) - lb[:, :1]
    silu_c = jax.nn.silu(c)
    silu_cc = jax.nn.silu(c_ctx)
    xc = ctx
    for l in range(DEPTH):
        last = l == DEPTH - 1
        mod = jnp.split(silu_c @ w_mod[l] + b_mod[l], N_MOD, axis=-1)
        modc = jnp.split(silu_cc @ w_mod[l] + b_mod[l], N_MOD, axis=-1)
        lam_init = 0.8 - 0.6 * math.exp(-0.3 * l)
        lam = (jnp.exp(jnp.sum(da_lambda[l, 0] * da_lambda[l, 1]))
               - jnp.exp(jnp.sum(da_lambda[l, 2] * da_lambda[l, 3])) + lam_init).astype(jnp.float32)

        hl = ada_norm(x, g_mix[l], mod[0], mod[1])
        hc = ada_norm(xc, g_mix[l], modc[0], modc[1])
        pl = project(hl, w_in[l], mla_w_uq[l], mla_w_ukv[l], mla_g_cq[l], mla_g_ckv[l], rope)
        pc = project(hc, w_in[l], mla_w_uq[l], mla_w_ukv[l], mla_g_cq[l], mla_g_ckv[l], None)

        k_da = jnp.concatenate([pc['da_k'], pl['da_k']], axis=1)
        v_da = jnp.concatenate([pc['da_v'], pl['da_v']], axis=1)
        o_da = sweep_query_blocks(lambda qb: diff_attn_core(qb, k_da, v_da, lam), pl['da_q'])

        kn_all = jnp.concatenate([pc['k_nope'], pl['k_nope']], axis=1)
        kr_all = jnp.concatenate([pc['k_rope'], pl['k_rope']], axis=1)
        v_all = jnp.concatenate([pc['mla_v'], pl['mla_v']], axis=1)
        o_mla = sweep_query_blocks(lambda qn, qr: mla_core(qn, qr, kn_all, kr_all, v_all), pl['q_nope'], pl['q_rope'])

        zeros = jnp.zeros((bsz, HG_HEADS, HG_KDIM, HG_VDIM), jnp.float32)
        oc_hg, s_f, s_b = hgrn2_bidir(pc, lb[0, l], lb[1, l], zeros, zeros)
        ol_hg, _, _ = hgrn2_bidir(pl, lb[0, l], lb[1, l], s_f, s_b)

        y = jnp.concatenate([da_post(o_da, da_subln_g[l], lam_init),
                             o_mla.reshape(bsz, n, MLA_W),
                             hgrn2_out(ol_hg, pl, hg_norm_g[l])], axis=-1)
        x = x + mod[2][..., None, :] * (y @ w_out[l])
        x = x + mod[5][..., None, :] * sqrelu_mlp(ada_norm(x, g_mlp[l], mod[3], mod[4]), w_ff1[l], w_ff2[l])

        if not last:
            oc_da = diff_attn_core(pc['da_q'], pc['da_k'], pc['da_v'], lam)
            oc_mla = mla_core(pc['q_nope'], pc['q_rope'], pc['k_nope'], pc['k_rope'], pc['mla_v'])
            yc = jnp.concatenate([da_post(oc_da, da_subln_g[l], lam_init),
                                  oc_mla.reshape(bsz, oc_mla.shape[1], MLA_W),
                                  hgrn2_out(oc_hg, pc, hg_norm_g[l])], axis=-1)
            xc = xc + modc[2][..., None, :] * (yc @ w_out[l])
            xc = xc + modc[5][..., None, :] * sqrelu_mlp(ada_norm(xc, g_mlp[l], modc[3], modc[4]), w_ff1[l], w_ff2[l])
    return rmsnorm(x, g_final)
```

```python
import functools
import math

import jax
import jax.numpy as jnp
from jax import lax
from jax.experimental import pallas as pl
from jax.experimental.pallas import tpu as pltpu

F32 = jnp.float32
BF16 = jnp.bfloat16

EPS = 1e-6
ROPE_THETA = 10000.0
GRID_W = 64
ROT = 32
DA_H, DA_D = 6, 32
DA_W = DA_H * 2 * DA_D
ML_H, ML_QR, ML_KVR, ML_NOPE, ML_V = 6, 256, 128, 64, 64
ML_W = ML_H * ML_V
HG_H, HG_K, HG_V = 4, 128, 64
HG_KW, HG_VW = HG_H * HG_K, HG_H * HG_V
FORGET_FLOOR = 1e-30
N_MOD = 6
LOG2E = 1.4426950408889634

HG_CHUNK = 64
HG_SUB = 16
LANE = 128
VMEM_LIMIT = 56 * 1024 * 1024


def _cparams(sem):
    return pltpu.CompilerParams(dimension_semantics=sem, vmem_limit_bytes=VMEM_LIMIT)


def _const_spec(shape):
    nd = len(shape)
    return pl.BlockSpec(shape, lambda *_: (0,) * nd)


def _pick(n, cands):
    for c in cands:
        if n % c == 0:
            return c
    raise ValueError(f"no tile for {n} in {cands}")


def _mod_body(c_ref, w_ref, b_ref, o_ref):
    cc = c_ref[...]
    s = cc * jax.nn.sigmoid(cc)
    o_ref[0] = jnp.dot(s.astype(BF16), w_ref[0].astype(BF16), preferred_element_type=F32) + b_ref[0]


def _modulation(cond, w_mod, b_mod):
    depth, d, nm = w_mod.shape
    r = cond.shape[0]
    tn = _pick(nm, (1536, 1024, 512, 128))
    return pl.pallas_call(
        _mod_body,
        grid=(depth, nm // tn),
        in_specs=[pl.BlockSpec((r, d), lambda l, j: (0, 0)),
                  pl.BlockSpec((1, d, tn), lambda l, j: (l, 0, j)),
                  pl.BlockSpec((1, 1, tn), lambda l, j: (l, 0, j))],
        out_specs=pl.BlockSpec((1, r, tn), lambda l, j: (l, 0, j)),
        out_shape=jax.ShapeDtypeStruct((depth, r, nm), F32),
        compiler_params=_cparams(("arbitrary", "arbitrary")),
        name="modulation",
    )(cond, w_mod, b_mod.reshape(depth, 1, nm))


def _ada_rows(x, g, modl, modc, is_ctx, i_shift, i_scale):
    ms = jnp.mean(x * x, axis=-1, keepdims=True)
    y = x * lax.rsqrt(ms + EPS) * g
    shift = jnp.where(is_ctx, modc[i_shift:i_shift + 1], modl[i_shift:i_shift + 1])
    scale = jnp.where(is_ctx, modc[i_scale:i_scale + 1], modl[i_scale:i_scale + 1])
    return y * (1.0 + scale) + shift


def _rms(x, g):
    return x * lax.rsqrt(jnp.mean(x * x, axis=-1, keepdims=True) + EPS) * g


def _proj_body(n_lat, x_ref, modl_ref, modc_ref, g_ref, rope_ref, w1_ref, w1s_ref, w2_ref, w3_ref,
               gcq_ref, gckv_ref, wuq_ref, wuqs_ref, wkn_ref, wvm_ref,
               qda_ref, kda_ref, vda_ref, qm_ref, km_ref, vm_ref, hg_ref):
    tm = x_ref.shape[1]
    rows = pl.program_id(1) * tm + lax.broadcasted_iota(jnp.int32, (tm, 1), 0)
    is_ctx = rows >= n_lat
    h = _ada_rows(x_ref[0], g_ref[...], modl_ref[0], modc_ref[...], is_ctx, 0, 1).astype(BF16)

    rope = rope_ref[...]
    c4, s4, cq_t, sq_t = (rope[:, k * LANE:(k + 1) * LANE] for k in range(4))
    n1 = w1_ref.shape[1] // LANE
    p1 = jnp.dot(h, w1_ref[...], preferred_element_type=F32)
    p1s = jnp.dot(h, w1s_ref[...], preferred_element_type=F32)
    rot = p1 * jnp.concatenate([c4] * n1, axis=1) + p1s * jnp.concatenate([s4] * n1, axis=1)

    rq = rot[:, :DA_W] * (DA_D ** -0.5 * LOG2E)
    lane64 = lax.broadcasted_iota(jnp.int32, (1, 2 * DA_D), 1)
    for vh in range(2 * DA_H):
        hh, comp = vh // 2, vh % 2
        piece = rq[:, hh * 2 * DA_D:(hh + 1) * 2 * DA_D]
        keep = (lane64 >= DA_D) if comp else (lane64 < DA_D)
        qda_ref[0, vh] = jnp.where(keep, piece, 0.0).astype(BF16)
    rk_t = rot[:, DA_W:2 * DA_W].T
    kda_ref[0] = rk_t.reshape(DA_H, 2 * DA_D, tm).astype(BF16)
    kr_t = rot[:, 2 * DA_W:2 * DA_W + LANE].T

    p2 = jnp.dot(h, w2_ref[...], preferred_element_type=F32)
    one_at_v = (lax.broadcasted_iota(jnp.int32, (1, LANE), 1) == ML_V).astype(F32)
    nv = DA_H * LANE
    for hh in range(DA_H):
        vda_ref[0, hh] = (p2[:, hh * LANE:(hh + 1) * LANE] + one_at_v).astype(BF16)
    cqn = _rms(p2[:, nv:nv + ML_QR], gcq_ref[...]).astype(BF16)
    ckvn = _rms(p2[:, nv + ML_QR:nv + ML_QR + ML_KVR], gckv_ref[...]).astype(BF16)

    qu = jnp.dot(cqn, wuq_ref[...], preferred_element_type=F32)
    qus = jnp.dot(cqn, wuqs_ref[...], preferred_element_type=F32)
    qm = (qu * jnp.concatenate([cq_t] * ML_H, axis=1) + qus * jnp.concatenate([sq_t] * ML_H, axis=1))
    qm = qm * ((ML_NOPE + ROT) ** -0.5 * LOG2E)
    for hh in range(ML_H):
        qm_ref[0, hh] = qm[:, hh * LANE:(hh + 1) * LANE].astype(BF16)
    kn_t = jnp.dot(ckvn, wkn_ref[...], preferred_element_type=F32).T
    for hh in range(ML_H):
        km_ref[0, hh] = jnp.concatenate(
            [kn_t[hh * ML_NOPE:(hh + 1) * ML_NOPE], kr_t[:LANE - ML_NOPE]], axis=0).astype(BF16)
    vmp = jnp.dot(ckvn, wvm_ref[...], preferred_element_type=F32)
    for hh in range(ML_H):
        vm_ref[0, hh] = (vmp[:, hh * LANE:(hh + 1) * LANE] + one_at_v).astype(BF16)

    hg_ref[0] = jnp.dot(h, w3_ref[...], preferred_element_type=F32)


def _project(xa, modl, modc, g, rope, wts, n_lat):
    b, t, d = xa.shape
    tm = _pick(t, (384, 256, 128))
    w1, w1s, w2, w3, gcq, gckv, wuq, wuqs, wkn, wvm = wts
    nhg = w3.shape[1]
    out_shapes = (
        jax.ShapeDtypeStruct((b, 2 * DA_H, t, 2 * DA_D), BF16),
        jax.ShapeDtypeStruct((b, DA_H, 2 * DA_D, t), BF16),
        jax.ShapeDtypeStruct((b, DA_H, t, LANE), BF16),
        jax.ShapeDtypeStruct((b, ML_H, t, LANE), BF16),
        jax.ShapeDtypeStruct((b, ML_H, LANE, t), BF16),
        jax.ShapeDtypeStruct((b, ML_H, t, LANE), BF16),
        jax.ShapeDtypeStruct((b, t, nhg), F32),
    )
    out_specs = (
        pl.BlockSpec((1, 2 * DA_H, tm, 2 * DA_D), lambda bi, i: (bi, 0, i, 0)),
        pl.BlockSpec((1, DA_H, 2 * DA_D, tm), lambda bi, i: (bi, 0, 0, i)),
        pl.BlockSpec((1, DA_H, tm, LANE), lambda bi, i: (bi, 0, i, 0)),
        pl.BlockSpec((1, ML_H, tm, LANE), lambda bi, i: (bi, 0, i, 0)),
        pl.BlockSpec((1, ML_H, LANE, tm), lambda bi, i: (bi, 0, 0, i)),
        pl.BlockSpec((1, ML_H, tm, LANE), lambda bi, i: (bi, 0, i, 0)),
        pl.BlockSpec((1, tm, nhg), lambda bi, i: (bi, i, 0)),
    )
    in_specs = [
        pl.BlockSpec((1, tm, d), lambda bi, i: (bi, i, 0)),
        pl.BlockSpec((1, N_MOD, d), lambda bi, i: (bi, 0, 0)),
        _const_spec(modc.shape), _const_spec(g.shape),
        pl.BlockSpec((tm, rope.shape[1]), lambda bi, i: (i, 0)),
    ] + [_const_spec(w.shape) for w in wts]
    return pl.pallas_call(
        functools.partial(_proj_body, n_lat),
        grid=(b, t // tm),
        in_specs=in_specs, out_specs=out_specs, out_shape=out_shapes,
        compiler_params=_cparams(("arbitrary", "arbitrary")),
        name="project",
    )(xa, modl, modc, g, rope, *wts)


def _flash_body(n_kv, group, diff_scale, lam_ref, g_ref, q_ref, kt_ref, v_ref, o_ref, m_ref, acc_ref):
    ki = pl.program_id(2)
    nk = pl.num_programs(2)
    tq = q_ref.shape[2]
    dk = q_ref.shape[3]

    @pl.when(ki == 0)
    def _():
        m_ref[...] = jnp.full(m_ref.shape, -jnp.inf, F32)
        acc_ref[...] = jnp.zeros(acc_ref.shape, F32)

    def head(hh, carry):
        q = q_ref[0, pl.ds(hh * group, group)].reshape(group * tq, dk)
        s = jnp.dot(q, kt_ref[0, hh], preferred_element_type=F32)
        m_prev = m_ref[hh]
        m_new = jnp.maximum(m_prev, jnp.max(s, axis=-1, keepdims=True))
        p = jnp.exp2(s - m_new)
        acc_ref[hh] = jnp.exp2(m_prev - m_new) * acc_ref[hh] + jnp.dot(
            p.astype(BF16), v_ref[0, hh], preferred_element_type=F32)
        m_ref[hh] = m_new
        return carry

    lax.fori_loop(0, n_kv, head, 0)

    @pl.when(ki == nk - 1)
    def _():
        outs = []
        for hh in range(n_kv):
            acc = acc_ref[hh]
            o = acc[:, :ML_V] / acc[:, ML_V:ML_V + 1]
            if group == 2:
                od = o[:tq] - lam_ref[0, 0] * o[tq:]
                o = _rms(od, g_ref[...]) * diff_scale
            outs.append(o)
        o_ref[0] = jnp.concatenate(outs, axis=-1).astype(o_ref.dtype)


def _flash(q, kt, v, lam, g, *, group, diff_scale, t_q, q_off, t_kv, kv_off, tq, tk):
    b, hq, _, dk = q.shape
    n_kv = kt.shape[1]
    assert hq == n_kv * group and q_off % tq == 0 and kv_off % tk == 0
    qo, ko = q_off // tq, kv_off // tk
    return pl.pallas_call(
        functools.partial(_flash_body, n_kv, group, diff_scale),
        grid=(b, t_q // tq, t_kv // tk),
        in_specs=[
            pl.BlockSpec(memory_space=pltpu.SMEM),
            _const_spec(g.shape),
            pl.BlockSpec((1, hq, tq, dk), lambda bi, qi, ki: (bi, 0, qi + qo, 0)),
            pl.BlockSpec((1, n_kv, dk, tk), lambda bi, qi, ki: (bi, 0, 0, ki + ko)),
            pl.BlockSpec((1, n_kv, tk, LANE), lambda bi, qi, ki: (bi, 0, ki + ko, 0)),
        ],
        out_specs=pl.BlockSpec((1, tq, n_kv * ML_V), lambda bi, qi, ki: (bi, qi, 0)),
        out_shape=jax.ShapeDtypeStruct((b, t_q, n_kv * ML_V), BF16),
        scratch_shapes=[pltpu.VMEM((n_kv, group * tq, 1), F32),
                        pltpu.VMEM((n_kv, group * tq, LANE), F32)],
        compiler_params=_cparams(("arbitrary", "arbitrary", "arbitrary")),
        name="flash_da" if group == 2 else "flash_mla",
    )(lam, g, q, kt, v)


def _cumsum_rows(x):
    n = x.shape[0]
    rows = lax.broadcasted_iota(jnp.int32, (n, 1), 0)
    d = 1
    while d < n:
        x = x + jnp.where(rows >= d, pltpu.roll(x, d, axis=0), 0.0)
        d *= 2
    return x


def _dot_nt(a, b):
    return lax.dot_general(a, b, (((1,), (1,)), ((), ())), preferred_element_type=F32)


def _hgrn_dir(q, z, v, lb, st_ref, reverse, e_mat):
    c = q.shape[0]
    nsub = c // HG_SUB
    sig = jax.nn.sigmoid(z)
    f = lb + (1.0 - lb) * sig
    kk = (1.0 - lb) * (1.0 - sig) if False else (1.0 - lb) * jax.nn.sigmoid(-z)
    logf = jnp.log(jnp.maximum(f, FORGET_FLOOR))
    bf = _cumsum_rows(logf)
    b = (bf[c - 1:c] - bf + logf) if reverse else bf
    b_end = b[0:1] if reverse else b[c - 1:c]
    rows = lax.broadcasted_iota(jnp.int32, (c, 1), 0)
    vrow_head = lax.broadcasted_iota(jnp.int32, (1, HG_VW), 1) // HG_V

    st = st_ref[...]
    o = _dot_nt((q * jnp.exp(b)).astype(BF16), st.astype(BF16))

    for hh in range(HG_H):
        sl = slice(hh * HG_K, (hh + 1) * HG_K)
        bh, qh, kh = b[:, sl], q[:, sl], kk[:, sl]
        qs, ks = [], []
        for i in (range(nsub - 1) if reverse else range(1, nsub)):
            if reverse:
                m = bh[(i + 1) * HG_SUB:(i + 1) * HG_SUB + 1]
                k_rows = rows >= (i + 1) * HG_SUB
            else:
                m = bh[i * HG_SUB - 1:i * HG_SUB]
                k_rows = rows < i * HG_SUB
            q_rows = (rows >= i * HG_SUB) & (rows < (i + 1) * HG_SUB)
            qs.append(jnp.where(q_rows, qh * jnp.exp(jnp.minimum(bh - m, 0.0)), 0.0))
            ks.append(jnp.where(k_rows, kh * jnp.exp(jnp.minimum(m - bh, 0.0)), 0.0))
        a_h = _dot_nt(jnp.concatenate(qs, axis=1).astype(BF16), jnp.concatenate(ks, axis=1).astype(BF16))
        v_h = jnp.where(vrow_head == hh, v, 0.0)
        o = o + jnp.dot(a_h.astype(BF16), v_h.astype(BF16), preferred_element_type=F32)

    sub_rows = lax.broadcasted_iota(jnp.int32, (HG_SUB, 1), 0)
    diag = []
    for i in range(nsub):
        r0 = i * HG_SUB
        qb, bb, kb, vb = q[r0:r0 + HG_SUB], b[r0:r0 + HG_SUB], kk[r0:r0 + HG_SUB], v[r0:r0 + HG_SUB]
        terms = []
        for s in range(HG_SUB):
            keep = (sub_rows <= s) if reverse else (sub_rows >= s)
            e = jnp.where(keep, jnp.exp(jnp.minimum(bb - bb[s:s + 1], 0.0)), 0.0)
            terms.append(qb * e * kb[s:s + 1])
        a_bc = jnp.dot(jnp.concatenate(terms, axis=0).astype(BF16), e_mat, preferred_element_type=F32)
        od = a_bc[0:HG_SUB] * vb[0:1]
        for s in range(1, HG_SUB):
            od = od + a_bc[s * HG_SUB:(s + 1) * HG_SUB] * vb[s:s + 1]
        diag.append(od)
    o = o + jnp.concatenate(diag, axis=0)

    kd = kk * jnp.exp(b_end - b)
    upd = jnp.dot(v.T.astype(BF16), kd.astype(BF16), preferred_element_type=F32)
    same_head = (lax.broadcasted_iota(jnp.int32, (HG_VW, 1), 0) // HG_V
                 == lax.broadcasted_iota(jnp.int32, (1, HG_KW), 1) // HG_K)
    st_ref[...] = st * jnp.exp(b_end) + jnp.where(same_head, upd, 0.0)
    return o


def _hgrn_body(qf_ref, zf_ref, vf_ref, qb_ref, zb_ref, vb_ref, lb_ref, e_ref, of_ref, ob_ref, stf_ref, stb_ref):
    @pl.when(pl.program_id(1) == 0)
    def _():
        stf_ref[...] = jnp.zeros(stf_ref.shape, F32)
        stb_ref[...] = jnp.zeros(stb_ref.shape, F32)

    e_mat = e_ref[...]
    of_ref[0] = _hgrn_dir(qf_ref[0], zf_ref[0], vf_ref[0], lb_ref[0:1], stf_ref, False, e_mat)
    ob_ref[0] = _hgrn_dir(qb_ref[0], zb_ref[0], vb_ref[0], lb_ref[1:2], stb_ref, True, e_mat)


def _hgrn(hg, lb2, e_mat, n_lat):
    b, t, _ = hg.shape
    c = HG_CHUNK
    nc, nlc = t // c, n_lat // c
    fwd = lambda i: (i + nlc) % nc
    bwd = lambda i: nc - 1 - i
    kb = HG_KW // HG_KW
    v_blk = 3 * HG_KW // HG_VW
    return pl.pallas_call(
        _hgrn_body,
        grid=(b, nc),
        in_specs=[
            pl.BlockSpec((1, c, HG_KW), lambda bi, i: (bi, fwd(i), 0)),
            pl.BlockSpec((1, c, HG_KW), lambda bi, i: (bi, fwd(i), 1)),
            pl.BlockSpec((1, c, HG_VW), lambda bi, i: (bi, fwd(i), v_blk)),
            pl.BlockSpec((1, c, HG_KW), lambda bi, i: (bi, bwd(i), 0)),
            pl.BlockSpec((1, c, HG_KW), lambda bi, i: (bi, bwd(i), 2)),
            pl.BlockSpec((1, c, HG_VW), lambda bi, i: (bi, bwd(i), v_blk)),
            _const_spec(lb2.shape), _const_spec(e_mat.shape),
        ],
        out_specs=(pl.BlockSpec((1, c, HG_VW), lambda bi, i: (bi, fwd(i), 0)),
                   pl.BlockSpec((1, c, HG_VW), lambda bi, i: (bi, bwd(i), 0))),
        out_shape=(jax.ShapeDtypeStruct((b, t, HG_VW), F32), jax.ShapeDtypeStruct((b, t, HG_VW), F32)),
        scratch_shapes=[pltpu.VMEM((HG_VW, HG_KW), F32), pltpu.VMEM((HG_VW, HG_KW), F32)],
        compiler_params=_cparams(("arbitrary", "arbitrary")),
        name="hgrn2",
    )(hg, hg, hg, hg, hg, hg, lb2, e_mat)


def _mix_mlp_body(n_lat, ff_chunk, final, x_ref, oda_ref, oml_ref, of_ref, ob_ref, gate_ref, modl_ref, modc_ref,
                  ghn_ref, gmlp_ref, gfin_ref, grp_ref, wda_ref, wml_ref, whg_ref, w1_ref, w2_ref, o_ref):
    tm = x_ref.shape[1]
    rows = pl.program_id(1) * tm + lax.broadcasted_iota(jnp.int32, (tm, 1), 0)
    is_ctx = rows >= n_lat
    modl, modc = modl_ref[0], modc_ref[...]

    oh = of_ref[0] + ob_ref[0]
    sq = oh * oh
    sq_hi = sq.astype(BF16)
    sq_lo = (sq - sq_hi.astype(F32)).astype(BF16)
    ms = (jnp.dot(sq_hi, grp_ref[...], preferred_element_type=F32)
          + jnp.dot(sq_lo, grp_ref[...], preferred_element_type=F32)) * (1.0 / HG_V)
    gt = gate_ref[0]
    yh = oh * lax.rsqrt(ms + EPS) * ghn_ref[...] * (gt * jax.nn.sigmoid(gt))

    mix = (jnp.dot(oda_ref[0], wda_ref[...], preferred_element_type=F32)
           + jnp.dot(oml_ref[0], wml_ref[...], preferred_element_type=F32)
           + jnp.dot(yh.astype(BF16), whg_ref[...], preferred_element_type=F32))
    g_mix = jnp.where(is_ctx, modc[2:3], modl[2:3])
    x1 = x_ref[0] + g_mix * mix

    hn = _ada_rows(x1, gmlp_ref[...], modl, modc, is_ctx, 3, 4).astype(BF16)
    dff = w1_ref.shape[1]
    acc = jnp.zeros(x1.shape, F32)
    for j in range(dff // ff_chunk):
        u = jnp.dot(hn, w1_ref[:, j * ff_chunk:(j + 1) * ff_chunk], preferred_element_type=F32)
        u = jnp.maximum(u, 0.0)
        acc = acc + jnp.dot((u * u).astype(BF16), w2_ref[j * ff_chunk:(j + 1) * ff_chunk, :],
                            preferred_element_type=F32)
    g_mlp = jnp.where(is_ctx, modc[5:6], modl[5:6])
    x2 = x1 + g_mlp * acc
    if final:
        x2 = _rms(x2, gfin_ref[...])
    o_ref[0] = x2


def _mix_mlp(xa, oda, oml, o_f, o_b, hg, modl, modc, ghn, gmlp, gfin, grp, wda, wml, whg, w1, w2, n_lat, final):
    b, t, d = xa.shape
    t_out = n_lat if final else t
    tm = _pick(t_out, (512, 384, 256, 128)) if final else _pick(t, (384, 256, 128))
    gate_blk = (3 * HG_KW + HG_VW) // HG_VW
    row = lambda w: pl.BlockSpec((1, tm, w), lambda bi, i: (bi, i, 0))
    consts = (modc, ghn, gmlp, gfin, grp, wda, wml, whg, w1, w2)
    return pl.pallas_call(
        functools.partial(_mix_mlp_body, n_lat, 1024, final),
        grid=(b, t_out // tm),
        in_specs=[row(d), row(DA_W), row(ML_W), row(HG_VW), row(HG_VW),
                  pl.BlockSpec((1, tm, HG_VW), lambda bi, i: (bi, i, gate_blk)),
                  pl.BlockSpec((1, N_MOD, d), lambda bi, i: (bi, 0, 0))]
                 + [_const_spec(a.shape) for a in consts],
        out_specs=row(d),
        out_shape=jax.ShapeDtypeStruct((b, t_out, d), F32),
        compiler_params=_cparams(("arbitrary", "arbitrary")),
        name="mix_mlp",
    )(xa, oda, oml, o_f, o_b, hg, modl, *consts)


def _swap_cols(w):
    n = w.shape[-1]
    j = jnp.arange(n)
    src = (j // 16) * 16 + (j % 16 + 8) % 16
    return w[..., src]


def _slot_cols(w, width):
    k, n = w.shape
    hs = n // width
    return jnp.pad(w.reshape(k, hs, width), ((0, 0), (0, 0), (0, LANE - width))).reshape(k, hs * LANE)


def _layer_weights(w_in, w_uq, w_ukv):
    o = 0
    parts = {}
    for name, wdt in (("da_q", DA_W), ("da_k", DA_W), ("da_v", DA_W), ("cq", ML_QR), ("ckv", ML_KVR), ("kr", ROT),
                      ("hq", HG_KW), ("hzf", HG_KW), ("hzb", HG_KW), ("hv", HG_VW), ("hgate", HG_VW)):
        parts[name] = w_in[:, o:o + wdt]
        o += wdt
    d = w_in.shape[0]
    w1 = jnp.concatenate([parts["da_q"], parts["da_k"], parts["kr"], jnp.zeros((d, LANE - ROT), F32)], axis=1)
    w2 = jnp.concatenate([_slot_cols(parts["da_v"], 2 * DA_D), parts["cq"], parts["ckv"]], axis=1)
    w3 = jnp.concatenate([parts[n] for n in ("hq", "hzf", "hzb", "hv", "hgate")], axis=1)
    uq = w_uq.reshape(ML_QR, ML_H, ML_NOPE + ROT)
    uq_sw = jnp.concatenate([jnp.zeros((ML_QR, ML_H, ML_NOPE), F32), _swap_cols(uq[..., ML_NOPE:])], axis=-1)
    pad_q = lambda a: jnp.pad(a, ((0, 0), (0, 0), (0, LANE - ML_NOPE - ROT))).reshape(ML_QR, ML_H * LANE)
    ukv = w_ukv.reshape(ML_KVR, ML_H, ML_NOPE + ML_V)
    wkn = ukv[..., :ML_NOPE].reshape(ML_KVR, ML_H * ML_NOPE)
    wvm = _slot_cols(ukv[..., ML_NOPE:].reshape(ML_KVR, ML_H * ML_V), ML_V)
    bf = lambda a: a.astype(BF16)
    return bf(w1), bf(_swap_cols(w1)), bf(w2), bf(w3), bf(pad_q(uq)), bf(pad_q(uq_sw)), bf(wkn), bf(wvm)


def _rope_tables(n_lat, n_ctx):
    rows = n_lat // GRID_W
    row = jnp.repeat(jnp.arange(rows, dtype=F32), GRID_W)
    col = jnp.tile(jnp.arange(GRID_W, dtype=F32), rows)
    n_freq = ROT // 4
    inv = ROPE_THETA ** (-jnp.arange(n_freq, dtype=F32) / n_freq)
    ar, ac = row[:, None] * inv, col[:, None] * inv
    c32 = jnp.concatenate([jnp.cos(ar), jnp.cos(ar), jnp.cos(ac), jnp.cos(ac)], axis=1)
    s32 = jnp.concatenate([-jnp.sin(ar), jnp.sin(ar), -jnp.sin(ac), jnp.sin(ac)], axis=1)
    c32 = jnp.concatenate([c32, jnp.ones((n_ctx, ROT), F32)], axis=0)
    s32 = jnp.concatenate([s32, jnp.zeros((n_ctx, ROT), F32)], axis=0)
    t = n_lat + n_ctx
    one, zero = jnp.ones((t, ML_NOPE), F32), jnp.zeros((t, ML_NOPE), F32)
    pad = jnp.zeros((t, LANE - ML_NOPE - ROT), F32)
    return jnp.concatenate([jnp.tile(c32, (1, LANE // ROT)), jnp.tile(s32, (1, LANE // ROT)),
                            one, c32, pad, zero, s32, pad], axis=1)


def kernel(x, c, ctx, c_ctx, w_mod, b_mod, g_mix, g_mlp, w_in, w_out, da_lambda, da_subln_g, mla_g_cq, mla_g_ckv,
           mla_w_uq, mla_w_ukv, hg_lower_bounds, hg_norm_g, w_ff1, w_ff2, g_final):
    bsz, n_lat, d = x.shape
    n_ctx = ctx.shape[1]
    depth = w_in.shape[0]
    t = n_lat + n_ctx

    lb = jax.nn.softmax(hg_lower_bounds.astype(F32), axis=1)
    lb = jnp.cumsum(lb, axis=1) - lb[:, :1]
    cond = jnp.concatenate([c, c_ctx[None]], axis=0)
    mod = _modulation(cond, w_mod, b_mod).reshape(depth, bsz + 1, N_MOD, d)
    rope = _rope_tables(n_lat, n_ctx)

    grp = (jnp.arange(HG_VW)[:, None] // HG_V == jnp.arange(HG_VW)[None, :] // HG_V).astype(BF16)
    e_mat = (jnp.arange(HG_KW)[:, None] // HG_K == jnp.arange(HG_VW)[None, :] // HG_V).astype(BF16)

    tq_da = _pick(n_lat, (512, 256, 128))
    tq_ml = _pick(n_lat, (1024, 512, 256, 128))
    tk = _pick(t, (768, 512, 384, 256, 128))
    tc = _pick(n_ctx, (256, 128))

    xa = jnp.concatenate([x, ctx], axis=1)
    for l in range(depth):
        last = l == depth - 1
        lam_init = 0.8 - 0.6 * math.exp(-0.3 * l)
        lam = (jnp.exp(jnp.sum(da_lambda[l, 0] * da_lambda[l, 1]))
               - jnp.exp(jnp.sum(da_lambda[l, 2] * da_lambda[l, 3])) + lam_init).astype(F32).reshape(1, 1)
        modl, modc = mod[l, :bsz], mod[l, bsz]
        w1, w1s, w2, w3, wuq, wuqs, wkn, wvm = _layer_weights(w_in[l], mla_w_uq[l], mla_w_ukv[l])
        wts = (w1, w1s, w2, w3, mla_g_cq[l][None], mla_g_ckv[l][None], wuq, wuqs, wkn, wvm)
        qda, kda, vda, qm, km, vm, hg = _project(xa, modl, modc, g_mix[l][None], rope, wts, n_lat)

        g_da = da_subln_g[l][None]
        da_kw = dict(group=2, diff_scale=1.0 - lam_init)
        ml_kw = dict(group=1, diff_scale=1.0)
        oda = _flash(qda, kda, vda, lam, g_da, t_q=n_lat, q_off=0, t_kv=t, kv_off=0, tq=tq_da, tk=tk, **da_kw)
        oml = _flash(qm, km, vm, lam, g_da, t_q=n_lat, q_off=0, t_kv=t, kv_off=0, tq=tq_ml, tk=tk, **ml_kw)
        if not last:
            ctx_kw = dict(t_q=n_ctx, q_off=n_lat, t_kv=n_ctx, kv_off=n_lat, tq=tc, tk=tc)
            oda = jnp.concatenate([oda, _flash(qda, kda, vda, lam, g_da, **ctx_kw, **da_kw)], axis=1)
            oml = jnp.concatenate([oml, _flash(qm, km, vm, lam, g_da, **ctx_kw, **ml_kw)], axis=1)

        o_f, o_b = _hgrn(hg, jnp.stack([lb[0, l], lb[1, l]]), e_mat, n_lat)

        ghn = jnp.tile(hg_norm_g[l], HG_H)[None]
        bf = lambda a: a.astype(BF16)
        xa = _mix_mlp(xa, oda, oml, o_f, o_b, hg, modl, modc, ghn, g_mlp[l][None], g_final[None], grp,
                      bf(w_out[l, :DA_W]), bf(w_out[l, DA_W:DA_W + ML_W]), bf(w_out[l, DA_W + ML_W:]),
                      bf(w_ff1[l]), bf(w_ff2[l]), n_lat, last)
    return xa
```

```python
import functools
import math

import jax
import jax.numpy as jnp
from jax import lax
from jax.experimental import pallas as pl
from jax.experimental.pallas import tpu as pltpu

F32 = jnp.float32
BF16 = jnp.bfloat16

EPS = 1e-6
ROPE_THETA = 10000.0
GRID_W = 64
ROT = 32
DA_H, DA_D = 6, 32
DA_W = DA_H * 2 * DA_D
ML_H, ML_QR, ML_KVR, ML_NOPE, ML_V = 6, 256, 128, 64, 64
ML_W = ML_H * ML_V
HG_H, HG_K, HG_V = 4, 128, 64
HG_KW, HG_VW = HG_H * HG_K, HG_H * HG_V
FORGET_FLOOR = 1e-30
N_MOD = 6
LOG2E = 1.4426950408889634

HG_CHUNK = 64
HG_SUB = 8
LANE = 128
VMEM_LIMIT = 56 * 1024 * 1024
TQ_DA = (512, 256, 128)
TQ_ML = (1024, 512, 256, 128)
TK = (768, 512, 384, 256, 128)
FLASH_ROWS = 512


def _cparams(sem):
    return pltpu.CompilerParams(dimension_semantics=sem, vmem_limit_bytes=VMEM_LIMIT)


def _const_spec(shape):
    nd = len(shape)
    return pl.BlockSpec(shape, lambda *_: (0,) * nd)


def _pick(n, cands):
    for c in cands:
        if n % c == 0:
            return c
    raise ValueError(f"no tile for {n} in {cands}")


def _mod_body(c_ref, w_ref, b_ref, o_ref):
    cc = c_ref[...]
    s = cc * jax.nn.sigmoid(cc)
    o_ref[0] = jnp.dot(s.astype(BF16), w_ref[0].astype(BF16), preferred_element_type=F32) + b_ref[0]


def _modulation(cond, w_mod, b_mod):
    depth, d, nm = w_mod.shape
    r = cond.shape[0]
    tn = _pick(nm, (1536, 1024, 512, 128))
    return pl.pallas_call(
        _mod_body,
        grid=(depth, nm // tn),
        in_specs=[pl.BlockSpec((r, d), lambda l, j: (0, 0)),
                  pl.BlockSpec((1, d, tn), lambda l, j: (l, 0, j)),
                  pl.BlockSpec((1, 1, tn), lambda l, j: (l, 0, j))],
        out_specs=pl.BlockSpec((1, r, tn), lambda l, j: (l, 0, j)),
        out_shape=jax.ShapeDtypeStruct((depth, r, nm), F32),
        compiler_params=_cparams(("arbitrary", "arbitrary")),
        name="modulation",
    )(cond, w_mod, b_mod.reshape(depth, 1, nm))


def _ada_rows(x, g, modl, modc, is_ctx, i_shift, i_scale):
    ms = jnp.mean(x * x, axis=-1, keepdims=True)
    y = x * lax.rsqrt(ms + EPS) * g
    shift = jnp.where(is_ctx, modc[i_shift:i_shift + 1], modl[i_shift:i_shift + 1])
    scale = jnp.where(is_ctx, modc[i_scale:i_scale + 1], modl[i_scale:i_scale + 1])
    return y * (1.0 + scale) + shift


def _rms(x, g):
    return x * lax.rsqrt(jnp.mean(x * x, axis=-1, keepdims=True) + EPS) * g


def _proj_body(n_lat, x_ref, modl_ref, modc_ref, g_ref, rope_ref, w1_ref, w1s_ref, w2_ref, w3_ref,
               gcq_ref, gckv_ref, wuq_ref, wuqs_ref, wkn_ref, wvm_ref,
               qda_ref, kda_ref, vda_ref, qm_ref, km_ref, vm_ref, hg_ref):
    tm = x_ref.shape[1]
    rows = pl.program_id(1) * tm + lax.broadcasted_iota(jnp.int32, (tm, 1), 0)
    is_ctx = rows >= n_lat
    h = _ada_rows(x_ref[0], g_ref[...], modl_ref[0], modc_ref[...], is_ctx, 0, 1).astype(BF16)

    rope = rope_ref[...]
    c4, s4, cq_t, sq_t = (rope[:, k * LANE:(k + 1) * LANE] for k in range(4))
    n1 = w1_ref.shape[1] // LANE
    p1 = jnp.dot(h, w1_ref[...], preferred_element_type=F32)
    p1s = jnp.dot(h, w1s_ref[...], preferred_element_type=F32)
    rot = p1 * jnp.concatenate([c4] * n1, axis=1) + p1s * jnp.concatenate([s4] * n1, axis=1)

    rq = rot[:, :DA_W] * (DA_D ** -0.5 * LOG2E)
    lane64 = lax.broadcasted_iota(jnp.int32, (1, 2 * DA_D), 1)
    for vh in range(2 * DA_H):
        hh, comp = vh // 2, vh % 2
        piece = rq[:, hh * 2 * DA_D:(hh + 1) * 2 * DA_D]
        keep = (lane64 >= DA_D) if comp else (lane64 < DA_D)
        qda_ref[0, vh] = jnp.where(keep, piece, 0.0).astype(BF16)
    rk_t = rot[:, DA_W:2 * DA_W].T
    kda_ref[0] = rk_t.reshape(DA_H, 2 * DA_D, tm).astype(BF16)
    kr_t = rot[:, 2 * DA_W:2 * DA_W + LANE].T

    p2 = jnp.dot(h, w2_ref[...], preferred_element_type=F32)
    one_at_v = (lax.broadcasted_iota(jnp.int32, (1, LANE), 1) == ML_V).astype(F32)
    nv = DA_H * LANE
    for hh in range(DA_H):
        vda_ref[0, hh] = (p2[:, hh * LANE:(hh + 1) * LANE] + one_at_v).astype(BF16)
    cqn = _rms(p2[:, nv:nv + ML_QR], gcq_ref[...]).astype(BF16)
    ckvn = _rms(p2[:, nv + ML_QR:nv + ML_QR + ML_KVR], gckv_ref[...]).astype(BF16)

    qu = jnp.dot(cqn, wuq_ref[...], preferred_element_type=F32)
    qus = jnp.dot(cqn, wuqs_ref[...], preferred_element_type=F32)
    qm = (qu * jnp.concatenate([cq_t] * ML_H, axis=1) + qus * jnp.concatenate([sq_t] * ML_H, axis=1))
    qm = qm * ((ML_NOPE + ROT) ** -0.5 * LOG2E)
    for hh in range(ML_H):
        qm_ref[0, hh] = qm[:, hh * LANE:(hh + 1) * LANE].astype(BF16)
    kn_t = jnp.dot(ckvn, wkn_ref[...], preferred_element_type=F32).T
    for hh in range(ML_H):
        km_ref[0, hh] = jnp.concatenate(
            [kn_t[hh * ML_NOPE:(hh + 1) * ML_NOPE], kr_t[:LANE - ML_NOPE]], axis=0).astype(BF16)
    vmp = jnp.dot(ckvn, wvm_ref[...], preferred_element_type=F32)
    for hh in range(ML_H):
        vm_ref[0, hh] = (vmp[:, hh * LANE:(hh + 1) * LANE] + one_at_v).astype(BF16)

    hg_ref[0] = jnp.dot(h, w3_ref[...], preferred_element_type=F32)


def _project(xa, modl, modc, g, rope, wts, n_lat):
    b, t, d = xa.shape
    tm = _pick(t, (384, 256, 128))
    w1, w1s, w2, w3, gcq, gckv, wuq, wuqs, wkn, wvm = wts
    nhg = w3.shape[1]
    out_shapes = (
        jax.ShapeDtypeStruct((b, 2 * DA_H, t, 2 * DA_D), BF16),
        jax.ShapeDtypeStruct((b, DA_H, 2 * DA_D, t), BF16),
        jax.ShapeDtypeStruct((b, DA_H, t, LANE), BF16),
        jax.ShapeDtypeStruct((b, ML_H, t, LANE), BF16),
        jax.ShapeDtypeStruct((b, ML_H, LANE, t), BF16),
        jax.ShapeDtypeStruct((b, ML_H, t, LANE), BF16),
        jax.ShapeDtypeStruct((b, t, nhg), F32),
    )
    out_specs = (
        pl.BlockSpec((1, 2 * DA_H, tm, 2 * DA_D), lambda bi, i: (bi, 0, i, 0)),
        pl.BlockSpec((1, DA_H, 2 * DA_D, tm), lambda bi, i: (bi, 0, 0, i)),
        pl.BlockSpec((1, DA_H, tm, LANE), lambda bi, i: (bi, 0, i, 0)),
        pl.BlockSpec((1, ML_H, tm, LANE), lambda bi, i: (bi, 0, i, 0)),
        pl.BlockSpec((1, ML_H, LANE, tm), lambda bi, i: (bi, 0, 0, i)),
        pl.BlockSpec((1, ML_H, tm, LANE), lambda bi, i: (bi, 0, i, 0)),
        pl.BlockSpec((1, tm, nhg), lambda bi, i: (bi, i, 0)),
    )
    in_specs = [
        pl.BlockSpec((1, tm, d), lambda bi, i: (bi, i, 0)),
        pl.BlockSpec((1, N_MOD, d), lambda bi, i: (bi, 0, 0)),
        _const_spec(modc.shape), _const_spec(g.shape),
        pl.BlockSpec((tm, rope.shape[1]), lambda bi, i: (i, 0)),
    ] + [_const_spec(w.shape) for w in wts]
    return pl.pallas_call(
        functools.partial(_proj_body, n_lat),
        grid=(b, t // tm),
        in_specs=in_specs, out_specs=out_specs, out_shape=out_shapes,
        compiler_params=_cparams(("arbitrary", "arbitrary")),
        name="project",
    )(xa, modl, modc, g, rope, *wts)


def _flash_body(n_kv, group, diff_scale, rc, lam_ref, g_ref, q_ref, kt_ref, v_ref, o_ref, m_ref, acc_ref):
    ki = pl.program_id(2)
    nk = pl.num_programs(2)
    tq = q_ref.shape[2]
    tk = kt_ref.shape[3]

    @pl.when(ki == 0)
    def _():
        m_ref[...] = jnp.full(m_ref.shape, -jnp.inf, F32)
        acc_ref[...] = jnp.zeros(acc_ref.shape, F32)

    chains = [(hh, r0) for hh in range(n_kv) for r0 in range(0, group * tq, rc)]

    def scores(hh, r0):
        q = q_ref[0, hh * group + r0 // tq, r0 % tq:r0 % tq + rc, :]
        return jnp.dot(q, kt_ref[0, hh], preferred_element_type=F32)

    s_next = scores(*chains[0])
    for ci, (hh, r0) in enumerate(chains):
        s = s_next
        if ci + 1 < len(chains):
            s_next = scores(*chains[ci + 1])
        rows = slice(r0, r0 + rc)
        m_prev = m_ref[hh, rows, :]
        m_new = jnp.maximum(m_prev, jnp.max(s, axis=-1, keepdims=True))
        p = jnp.exp2(s - jnp.concatenate([m_new] * (tk // LANE), axis=1))
        acc_ref[hh, rows, :] = jnp.exp2(m_prev - m_new) * acc_ref[hh, rows, :] + jnp.dot(
            p.astype(BF16), v_ref[0, hh], preferred_element_type=F32)
        m_ref[hh, rows, :] = m_new

    @pl.when(ki == nk - 1)
    def _():
        outs = []
        for hh in range(n_kv):
            acc = acc_ref[hh]
            o = acc[:, :ML_V] / acc[:, ML_V:ML_V + 1]
            if group == 2:
                od = o[:tq] - lam_ref[0, 0] * o[tq:]
                o = _rms(od, g_ref[...]) * diff_scale
            outs.append(o)
        o_ref[0] = jnp.concatenate(outs, axis=-1).astype(o_ref.dtype)


def _flash(q, kt, v, lam, g, *, group, diff_scale, t_q, q_off, t_kv, kv_off, tq, tk):
    b, hq, _, dk = q.shape
    n_kv = kt.shape[1]
    assert hq == n_kv * group and q_off % tq == 0 and kv_off % tk == 0
    qo, ko = q_off // tq, kv_off // tk
    return pl.pallas_call(
        functools.partial(_flash_body, n_kv, group, diff_scale, min(FLASH_ROWS, tq)),
        grid=(b, t_q // tq, t_kv // tk),
        in_specs=[
            pl.BlockSpec(memory_space=pltpu.SMEM),
            _const_spec(g.shape),
            pl.BlockSpec((1, hq, tq, dk), lambda bi, qi, ki: (bi, 0, qi + qo, 0)),
            pl.BlockSpec((1, n_kv, dk, tk), lambda bi, qi, ki: (bi, 0, 0, ki + ko)),
            pl.BlockSpec((1, n_kv, tk, LANE), lambda bi, qi, ki: (bi, 0, ki + ko, 0)),
        ],
        out_specs=pl.BlockSpec((1, tq, n_kv * ML_V), lambda bi, qi, ki: (bi, qi, 0)),
        out_shape=jax.ShapeDtypeStruct((b, t_q, n_kv * ML_V), BF16),
        scratch_shapes=[pltpu.VMEM((n_kv, group * tq, LANE), F32),
                        pltpu.VMEM((n_kv, group * tq, LANE), F32)],
        compiler_params=_cparams(("arbitrary", "arbitrary", "arbitrary")),
        name="flash_da" if group == 2 else "flash_mla",
    )(lam, g, q, kt, v)


def _cumsum_rows(x):
    n = x.shape[0]
    rows = lax.broadcasted_iota(jnp.int32, (n, 1), 0)
    d = 1
    while d < n:
        x = x + jnp.where(rows >= d, pltpu.roll(x, d, axis=0), 0.0)
        d *= 2
    return x


def _dot_nt(a, b):
    return lax.dot_general(a, b, (((1,), (1,)), ((), ())), preferred_element_type=F32)


def _pad_rows(x, start, total):
    parts = []
    if start:
        parts.append(jnp.zeros((start, x.shape[1]), x.dtype))
    parts.append(x)
    if total - start - x.shape[0]:
        parts.append(jnp.zeros((total - start - x.shape[0], x.shape[1]), x.dtype))
    return jnp.concatenate(parts, axis=0) if len(parts) > 1 else x


def _hgrn_body(qf_ref, zf_ref, vf_ref, qb_ref, zb_ref, vb_ref, lb_ref, e_ref, of_ref, ob_ref, stf_ref, stb_ref):
    @pl.when(pl.program_id(1) == 0)
    def _():
        stf_ref[...] = jnp.zeros(stf_ref.shape, F32)
        stb_ref[...] = jnp.zeros(stb_ref.shape, F32)

    c, nsub, kw = HG_CHUNK, HG_CHUNK // HG_SUB, HG_KW
    qs = (qf_ref[0], qb_ref[0])
    vs = (vf_ref[0], vb_ref[0])
    sts = (stf_ref[...], stb_ref[...])

    z = jnp.concatenate([zf_ref[0], zb_ref[0]], axis=1)
    lb = jnp.concatenate([lb_ref[0:1], lb_ref[1:2]], axis=1)
    f = lb + (1.0 - lb) * jax.nn.sigmoid(z)
    kk2 = (1.0 - lb) * jax.nn.sigmoid(-z)
    logf = jnp.log(jnp.maximum(f, FORGET_FLOOR))
    bf = _cumsum_rows(logf)
    kks = (kk2[:, :kw], kk2[:, kw:])
    bs = (bf[:, :kw] * LOG2E, (bf[c - 1:c, kw:] - bf[:, kw:] + logf[:, kw:]) * LOG2E)
    b_ends = (bs[0][c - 1:c], bs[1][0:1])

    outs = [_dot_nt((qs[d] * jnp.exp2(bs[d])).astype(BF16), sts[d].astype(BF16)) for d in range(2)]

    head_r = lax.broadcasted_iota(jnp.int32, (HG_H * c, 1), 0) // c
    same_blk = head_r == lax.broadcasted_iota(jnp.int32, (1, HG_H * c), 1) // c
    v_keep = head_r == lax.broadcasted_iota(jnp.int32, (1, HG_VW), 1) // HG_V
    a_mats = []
    for d in range(2):
        q, kk, b = qs[d], kks[d], bs[d]
        q_slots, k_slots = [], []
        for i in (range(nsub - 1) if d else range(1, nsub)):
            r0 = i * HG_SUB
            if d:
                m, k0, k1 = b[r0 + HG_SUB:r0 + HG_SUB + 1], r0 + HG_SUB, c
            else:
                m, k0, k1 = b[r0 - 1:r0], 0, r0
            q_slots.append(_pad_rows(q[r0:r0 + HG_SUB] * jnp.exp2(b[r0:r0 + HG_SUB] - m), r0, c).astype(BF16))
            k_slots.append(_pad_rows(kk[k0:k1] * jnp.exp2(m - b[k0:k1]), k0, c).astype(BF16))
        stack = lambda slots: jnp.concatenate(
            [jnp.concatenate([x[:, hh * HG_K:(hh + 1) * HG_K] for x in slots], axis=1) for hh in range(HG_H)], axis=0)
        a_mats.append(_dot_nt(stack(q_slots), stack(k_slots)))

    terms = []
    for d in range(2):
        q, kk, b = qs[d], kks[d], bs[d]
        for i in range(nsub):
            r0 = i * HG_SUB
            qb, bb, kb = q[r0:r0 + HG_SUB], b[r0:r0 + HG_SUB], kk[r0:r0 + HG_SUB]
            terms += [(qb * jnp.exp2(bb - bb[s:s + 1]) * kb[s:s + 1]).astype(BF16) for s in range(HG_SUB)]
    a_bc = jnp.dot(jnp.concatenate(terms, axis=0), e_ref[...], preferred_element_type=F32)

    for d in range(2):
        v_stack = jnp.where(v_keep, jnp.concatenate([vs[d]] * HG_H, axis=0), 0.0).astype(BF16)
        o_stack = jnp.dot(jnp.where(same_blk, a_mats[d], 0.0).astype(BF16), v_stack, preferred_element_type=F32)
        for hh in range(HG_H):
            outs[d] = outs[d] + o_stack[hh * c:(hh + 1) * c]

    pair = lax.broadcasted_iota(jnp.int32, (HG_SUB * HG_SUB, 1), 0)
    for d in range(2):
        keep = (pair % HG_SUB <= pair // HG_SUB) if d else (pair % HG_SUB >= pair // HG_SUB)
        keep = jnp.broadcast_to(keep, (HG_SUB * HG_SUB, HG_VW))
        diag = []
        for i in range(nsub):
            r0 = i * HG_SUB
            base = (d * nsub + i) * HG_SUB * HG_SUB
            blk = jnp.where(keep, a_bc[base:base + HG_SUB * HG_SUB], 0.0)
            vb = vs[d][r0:r0 + HG_SUB]
            od = blk[0:HG_SUB] * vb[0:1]
            for s in range(1, HG_SUB):
                od = od + blk[s * HG_SUB:(s + 1) * HG_SUB] * vb[s:s + 1]
            diag.append(od)
        outs[d] = outs[d] + jnp.concatenate(diag, axis=0)

    of_ref[0] = outs[0]
    ob_ref[0] = outs[1]

    same_head = (lax.broadcasted_iota(jnp.int32, (HG_VW, 1), 0) // HG_V
                 == lax.broadcasted_iota(jnp.int32, (1, HG_KW), 1) // HG_K)
    for d, st_ref in enumerate((stf_ref, stb_ref)):
        kd = kks[d] * jnp.exp2(b_ends[d] - bs[d])
        upd = jnp.dot(vs[d].T.astype(BF16), kd.astype(BF16), preferred_element_type=F32)
        st_ref[...] = sts[d] * jnp.exp2(b_ends[d]) + jnp.where(same_head, upd, 0.0)


def _hgrn(hg, lb2, e_mat, n_lat):
    b, t, _ = hg.shape
    c = HG_CHUNK
    nc, nlc = t // c, n_lat // c
    fwd = lambda i: (i + nlc) % nc
    bwd = lambda i: nc - 1 - i
    v_blk = 3 * HG_KW // HG_VW
    return pl.pallas_call(
        _hgrn_body,
        grid=(b, nc),
        in_specs=[
            pl.BlockSpec((1, c, HG_KW), lambda bi, i: (bi, fwd(i), 0)),
            pl.BlockSpec((1, c, HG_KW), lambda bi, i: (bi, fwd(i), 1)),
            pl.BlockSpec((1, c, HG_VW), lambda bi, i: (bi, fwd(i), v_blk)),
            pl.BlockSpec((1, c, HG_KW), lambda bi, i: (bi, bwd(i), 0)),
            pl.BlockSpec((1, c, HG_KW), lambda bi, i: (bi, bwd(i), 2)),
            pl.BlockSpec((1, c, HG_VW), lambda bi, i: (bi, bwd(i), v_blk)),
            _const_spec(lb2.shape), _const_spec(e_mat.shape),
        ],
        out_specs=(pl.BlockSpec((1, c, HG_VW), lambda bi, i: (bi, fwd(i), 0)),
                   pl.BlockSpec((1, c, HG_VW), lambda bi, i: (bi, bwd(i), 0))),
        out_shape=(jax.ShapeDtypeStruct((b, t, HG_VW), F32), jax.ShapeDtypeStruct((b, t, HG_VW), F32)),
        scratch_shapes=[pltpu.VMEM((HG_VW, HG_KW), F32), pltpu.VMEM((HG_VW, HG_KW), F32)],
        compiler_params=_cparams(("arbitrary", "arbitrary")),
        name="hgrn2",
    )(hg, hg, hg, hg, hg, hg, lb2, e_mat)


def _mix_mlp_body(n_lat, ff_chunk, final, x_ref, oda_ref, oml_ref, of_ref, ob_ref, gate_ref, modl_ref, modc_ref,
                  ghn_ref, gmlp_ref, gfin_ref, grp_ref, wda_ref, wml_ref, whg_ref, w1_ref, w2_ref, o_ref):
    tm = x_ref.shape[1]
    rows = pl.program_id(1) * tm + lax.broadcasted_iota(jnp.int32, (tm, 1), 0)
    is_ctx = rows >= n_lat
    modl, modc = modl_ref[0], modc_ref[...]

    oh = of_ref[0] + ob_ref[0]
    sq = oh * oh
    sq_hi = sq.astype(BF16)
    sq_lo = (sq - sq_hi.astype(F32)).astype(BF16)
    ms = (jnp.dot(sq_hi, grp_ref[...], preferred_element_type=F32)
          + jnp.dot(sq_lo, grp_ref[...], preferred_element_type=F32)) * (1.0 / HG_V)
    gt = gate_ref[0]
    yh = oh * lax.rsqrt(ms + EPS) * ghn_ref[...] * (gt * jax.nn.sigmoid(gt))

    mix = (jnp.dot(oda_ref[0], wda_ref[...], preferred_element_type=F32)
           + jnp.dot(oml_ref[0], wml_ref[...], preferred_element_type=F32)
           + jnp.dot(yh.astype(BF16), whg_ref[...], preferred_element_type=F32))
    g_mix = jnp.where(is_ctx, modc[2:3], modl[2:3])
    x1 = x_ref[0] + g_mix * mix

    hn = _ada_rows(x1, gmlp_ref[...], modl, modc, is_ctx, 3, 4).astype(BF16)
    dff = w1_ref.shape[1]
    acc = jnp.zeros(x1.shape, F32)
    for j in range(dff // ff_chunk):
        u = jnp.dot(hn, w1_ref[:, j * ff_chunk:(j + 1) * ff_chunk], preferred_element_type=F32)
        u = jnp.maximum(u, 0.0)
        acc = acc + jnp.dot((u * u).astype(BF16), w2_ref[j * ff_chunk:(j + 1) * ff_chunk, :],
                            preferred_element_type=F32)
    g_mlp = jnp.where(is_ctx, modc[5:6], modl[5:6])
    x2 = x1 + g_mlp * acc
    if final:
        x2 = _rms(x2, gfin_ref[...])
    o_ref[0] = x2


def _mix_mlp(xa, oda, oml, o_f, o_b, hg, modl, modc, ghn, gmlp, gfin, grp, wda, wml, whg, w1, w2, n_lat, final):
    b, t, d = xa.shape
    t_out = n_lat if final else t
    tm = _pick(t_out, (512, 384, 256, 128)) if final else _pick(t, (384, 256, 128))
    gate_blk = (3 * HG_KW + HG_VW) // HG_VW
    row = lambda w: pl.BlockSpec((1, tm, w), lambda bi, i: (bi, i, 0))
    consts = (modc, ghn, gmlp, gfin, grp, wda, wml, whg, w1, w2)
    return pl.pallas_call(
        functools.partial(_mix_mlp_body, n_lat, 1024, final),
        grid=(b, t_out // tm),
        in_specs=[row(d), row(DA_W), row(ML_W), row(HG_VW), row(HG_VW),
                  pl.BlockSpec((1, tm, HG_VW), lambda bi, i: (bi, i, gate_blk)),
                  pl.BlockSpec((1, N_MOD, d), lambda bi, i: (bi, 0, 0))]
                 + [_const_spec(a.shape) for a in consts],
        out_specs=row(d),
        out_shape=jax.ShapeDtypeStruct((b, t_out, d), F32),
        compiler_params=_cparams(("arbitrary", "arbitrary")),
        name="mix_mlp",
    )(xa, oda, oml, o_f, o_b, hg, modl, *consts)


def _swap_cols(w):
    n = w.shape[-1]
    j = jnp.arange(n)
    src = (j // 16) * 16 + (j % 16 + 8) % 16
    return w[..., src]


def _slot_cols(w, width):
    k, n = w.shape
    hs = n // width
    return jnp.pad(w.reshape(k, hs, width), ((0, 0), (0, 0), (0, LANE - width))).reshape(k, hs * LANE)


def _layer_weights(w_in, w_uq, w_ukv):
    o = 0
    parts = {}
    for name, wdt in (("da_q", DA_W), ("da_k", DA_W), ("da_v", DA_W), ("cq", ML_QR), ("ckv", ML_KVR), ("kr", ROT),
                      ("hq", HG_KW), ("hzf", HG_KW), ("hzb", HG_KW), ("hv", HG_VW), ("hgate", HG_VW)):
        parts[name] = w_in[:, o:o + wdt]
        o += wdt
    d = w_in.shape[0]
    w1 = jnp.concatenate([parts["da_q"], parts["da_k"], parts["kr"], jnp.zeros((d, LANE - ROT), F32)], axis=1)
    w2 = jnp.concatenate([_slot_cols(parts["da_v"], 2 * DA_D), parts["cq"], parts["ckv"]], axis=1)
    w3 = jnp.concatenate([parts[n] for n in ("hq", "hzf", "hzb", "hv", "hgate")], axis=1)
    uq = w_uq.reshape(ML_QR, ML_H, ML_NOPE + ROT)
    uq_sw = jnp.concatenate([jnp.zeros((ML_QR, ML_H, ML_NOPE), F32), _swap_cols(uq[..., ML_NOPE:])], axis=-1)
    pad_q = lambda a: jnp.pad(a, ((0, 0), (0, 0), (0, LANE - ML_NOPE - ROT))).reshape(ML_QR, ML_H * LANE)
    ukv = w_ukv.reshape(ML_KVR, ML_H, ML_NOPE + ML_V)
    wkn = ukv[..., :ML_NOPE].reshape(ML_KVR, ML_H * ML_NOPE)
    wvm = _slot_cols(ukv[..., ML_NOPE:].reshape(ML_KVR, ML_H * ML_V), ML_V)
    bf = lambda a: a.astype(BF16)
    return bf(w1), bf(_swap_cols(w1)), bf(w2), bf(w3), bf(pad_q(uq)), bf(pad_q(uq_sw)), bf(wkn), bf(wvm)


def _rope_tables(n_lat, n_ctx):
    rows = n_lat // GRID_W
    row = jnp.repeat(jnp.arange(rows, dtype=F32), GRID_W)
    col = jnp.tile(jnp.arange(GRID_W, dtype=F32), rows)
    n_freq = ROT // 4
    inv = ROPE_THETA ** (-jnp.arange(n_freq, dtype=F32) / n_freq)
    ar, ac = row[:, None] * inv, col[:, None] * inv
    c32 = jnp.concatenate([jnp.cos(ar), jnp.cos(ar), jnp.cos(ac), jnp.cos(ac)], axis=1)
    s32 = jnp.concatenate([-jnp.sin(ar), jnp.sin(ar), -jnp.sin(ac), jnp.sin(ac)], axis=1)
    c32 = jnp.concatenate([c32, jnp.ones((n_ctx, ROT), F32)], axis=0)
    s32 = jnp.concatenate([s32, jnp.zeros((n_ctx, ROT), F32)], axis=0)
    t = n_lat + n_ctx
    one, zero = jnp.ones((t, ML_NOPE), F32), jnp.zeros((t, ML_NOPE), F32)
    pad = jnp.zeros((t, LANE - ML_NOPE - ROT), F32)
    return jnp.concatenate([jnp.tile(c32, (1, LANE // ROT)), jnp.tile(s32, (1, LANE // ROT)),
                            one, c32, pad, zero, s32, pad], axis=1)


def kernel(x, c, ctx, c_ctx, w_mod, b_mod, g_mix, g_mlp, w_in, w_out, da_lambda, da_subln_g, mla_g_cq, mla_g_ckv,
           mla_w_uq, mla_w_ukv, hg_lower_bounds, hg_norm_g, w_ff1, w_ff2, g_final):
    bsz, n_lat, d = x.shape
    n_ctx = ctx.shape[1]
    depth = w_in.shape[0]
    t = n_lat + n_ctx

    lb = jax.nn.softmax(hg_lower_bounds.astype(F32), axis=1)
    lb = jnp.cumsum(lb, axis=1) - lb[:, :1]
    cond = jnp.concatenate([c, c_ctx[None]], axis=0)
    mod = _modulation(cond, w_mod, b_mod).reshape(depth, bsz + 1, N_MOD, d)
    rope = _rope_tables(n_lat, n_ctx)

    grp = (jnp.arange(HG_VW)[:, None] // HG_V == jnp.arange(HG_VW)[None, :] // HG_V).astype(BF16)
    e_mat = (jnp.arange(HG_KW)[:, None] // HG_K == jnp.arange(HG_VW)[None, :] // HG_V).astype(BF16)

    tq_da = _pick(n_lat, TQ_DA)
    tq_ml = _pick(n_lat, TQ_ML)
    tk = _pick(t, TK)
    tc = _pick(n_ctx, (256, 128))

    xa = jnp.concatenate([x, ctx], axis=1)
    for l in range(depth):
        last = l == depth - 1
        lam_init = 0.8 - 0.6 * math.exp(-0.3 * l)
        lam = (jnp.exp(jnp.sum(da_lambda[l, 0] * da_lambda[l, 1]))
               - jnp.exp(jnp.sum(da_lambda[l, 2] * da_lambda[l, 3])) + lam_init).astype(F32).reshape(1, 1)
        modl, modc = mod[l, :bsz], mod[l, bsz]
        w1, w1s, w2, w3, wuq, wuqs, wkn, wvm = _layer_weights(w_in[l], mla_w_uq[l], mla_w_ukv[l])
        wts = (w1, w1s, w2, w3, mla_g_cq[l][None], mla_g_ckv[l][None], wuq, wuqs, wkn, wvm)
        qda, kda, vda, qm, km, vm, hg = _project(xa, modl, modc, g_mix[l][None], rope, wts, n_lat)

        g_da = da_subln_g[l][None]
        da_kw = dict(group=2, diff_scale=1.0 - lam_init)
        ml_kw = dict(group=1, diff_scale=1.0)
        oda = _flash(qda, kda, vda, lam, g_da, t_q=n_lat, q_off=0, t_kv=t, kv_off=0, tq=tq_da, tk=tk, **da_kw)
        oml = _flash(qm, km, vm, lam, g_da, t_q=n_lat, q_off=0, t_kv=t, kv_off=0, tq=tq_ml, tk=tk, **ml_kw)
        if not last:
            ctx_kw = dict(t_q=n_ctx, q_off=n_lat, t_kv=n_ctx, kv_off=n_lat, tq=tc, tk=tc)
            oda = jnp.concatenate([oda, _flash(qda, kda, vda, lam, g_da, **ctx_kw, **da_kw)], axis=1)
            oml = jnp.concatenate([oml, _flash(qm, km, vm, lam, g_da, **ctx_kw, **ml_kw)], axis=1)

        o_f, o_b = _hgrn(hg, jnp.stack([lb[0, l], lb[1, l]]), e_mat, n_lat)

        ghn = jnp.tile(hg_norm_g[l], HG_H)[None]
        bf = lambda a: a.astype(BF16)
        xa = _mix_mlp(xa, oda, oml, o_f, o_b, hg, modl, modc, ghn, g_mlp[l][None], g_final[None], grp,
                      bf(w_out[l, :DA_W]), bf(w_out[l, DA_W:DA_W + ML_W]), bf(w_out[l, DA_W + ML_W:]),
                      bf(w_ff1[l]), bf(w_ff2[l]), n_lat, last)
    return xa
```

```python
import functools
import math

import jax
import jax.numpy as jnp
from jax import lax
from jax.experimental import pallas as pl
from jax.experimental.pallas import tpu as pltpu

F32 = jnp.float32
BF16 = jnp.bfloat16

EPS = 1e-6
ROPE_THETA = 10000.0
GRID_W = 64
ROT = 32
DA_H, DA_D = 6, 32
DA_W = DA_H * 2 * DA_D
ML_H, ML_QR, ML_KVR, ML_NOPE, ML_V = 6, 256, 128, 64, 64
ML_W = ML_H * ML_V
HG_H, HG_K, HG_V = 4, 128, 64
HG_KW, HG_VW = HG_H * HG_K, HG_H * HG_V
FORGET_FLOOR = 1e-30
N_MOD = 6
LOG2E = 1.4426950408889634

HG_CHUNK = 64
HG_SUB = 8
LANE = 128
VMEM_LIMIT = 56 * 1024 * 1024
TQ_DA = (1024, 512, 256, 128)
TQ_ML = (2048, 1024, 512, 256, 128)
TK = (768, 512, 384, 256, 128)
VT_ROWS = 80
LOOKAHEAD = 2
FLASH_ROWS = 512


def _cparams(sem):
    return pltpu.CompilerParams(dimension_semantics=sem, vmem_limit_bytes=VMEM_LIMIT)


def _const_spec(shape):
    nd = len(shape)
    return pl.BlockSpec(shape, lambda *_: (0,) * nd)


def _pick(n, cands):
    for c in cands:
        if n % c == 0:
            return c
    raise ValueError(f"no tile for {n} in {cands}")


def _mod_body(c_ref, w_ref, b_ref, o_ref):
    cc = c_ref[...]
    s = cc * jax.nn.sigmoid(cc)
    o_ref[0] = jnp.dot(s.astype(BF16), w_ref[0].astype(BF16), preferred_element_type=F32) + b_ref[0]


def _modulation(cond, w_mod, b_mod):
    depth, d, nm = w_mod.shape
    r = cond.shape[0]
    tn = _pick(nm, (1536, 1024, 512, 128))
    return pl.pallas_call(
        _mod_body,
        grid=(depth, nm // tn),
        in_specs=[pl.BlockSpec((r, d), lambda l, j: (0, 0)),
                  pl.BlockSpec((1, d, tn), lambda l, j: (l, 0, j)),
                  pl.BlockSpec((1, 1, tn), lambda l, j: (l, 0, j))],
        out_specs=pl.BlockSpec((1, r, tn), lambda l, j: (l, 0, j)),
        out_shape=jax.ShapeDtypeStruct((depth, r, nm), F32),
        compiler_params=_cparams(("arbitrary", "arbitrary")),
        name="modulation",
    )(cond, w_mod, b_mod.reshape(depth, 1, nm))


def _ada_rows(x, g, modl, modc, is_ctx, i_shift, i_scale):
    ms = jnp.mean(x * x, axis=-1, keepdims=True)
    y = x * lax.rsqrt(ms + EPS) * g
    shift = jnp.where(is_ctx, modc[i_shift:i_shift + 1], modl[i_shift:i_shift + 1])
    scale = jnp.where(is_ctx, modc[i_scale:i_scale + 1], modl[i_scale:i_scale + 1])
    return y * (1.0 + scale) + shift


def _rms(x, g):
    return x * lax.rsqrt(jnp.mean(x * x, axis=-1, keepdims=True) + EPS) * g


def _proj_body(n_lat, x_ref, modl_ref, modc_ref, g_ref, rope_ref, w1_ref, w1s_ref, w2_ref, w3_ref,
               gcq_ref, gckv_ref, wuq_ref, wuqs_ref, wkn_ref, wvm_ref,
               qda_ref, kda_ref, vda_ref, qm_ref, km_ref, vm_ref, hg_ref):
    tm = x_ref.shape[1]
    rows = pl.program_id(1) * tm + lax.broadcasted_iota(jnp.int32, (tm, 1), 0)
    is_ctx = rows >= n_lat
    h = _ada_rows(x_ref[0], g_ref[...], modl_ref[0], modc_ref[...], is_ctx, 0, 1).astype(BF16)

    rope = rope_ref[...]
    c4, s4, cq_t, sq_t = (rope[:, k * LANE:(k + 1) * LANE] for k in range(4))
    n1 = w1_ref.shape[1] // LANE
    p1 = jnp.dot(h, w1_ref[...], preferred_element_type=F32)
    p1s = jnp.dot(h, w1s_ref[...], preferred_element_type=F32)
    rot = p1 * jnp.concatenate([c4] * n1, axis=1) + p1s * jnp.concatenate([s4] * n1, axis=1)

    rq_t = (rot[:, :DA_W] * (DA_D ** -0.5 * LOG2E)).T
    row64 = lax.broadcasted_iota(jnp.int32, (2 * DA_D, 1), 0)
    for vh in range(2 * DA_H):
        hh, comp = vh // 2, vh % 2
        keep = (row64 >= DA_D) if comp else (row64 < DA_D)
        qda_ref[0, vh] = jnp.where(keep, rq_t[hh * 2 * DA_D:(hh + 1) * 2 * DA_D], 0.0).astype(BF16)
    for hh in range(DA_H):
        kda_ref[0, hh] = rot[:, DA_W + hh * 2 * DA_D:DA_W + (hh + 1) * 2 * DA_D].astype(BF16)
    kr_slot = rot[:, 2 * DA_W:2 * DA_W + LANE]

    ones_rows = (lax.broadcasted_iota(jnp.int32, (VT_ROWS - ML_V, tm), 0) == 0).astype(F32)
    p2 = jnp.dot(h, w2_ref[...], preferred_element_type=F32)
    v_t = p2[:, :DA_W].T
    for hh in range(DA_H):
        vda_ref[0, hh] = jnp.concatenate([v_t[hh * ML_V:(hh + 1) * ML_V], ones_rows], axis=0).astype(BF16)
    cqn = _rms(p2[:, DA_W:DA_W + ML_QR], gcq_ref[...]).astype(BF16)
    ckvn = _rms(p2[:, DA_W + ML_QR:DA_W + ML_QR + ML_KVR], gckv_ref[...]).astype(BF16)

    qu = jnp.dot(cqn, wuq_ref[...], preferred_element_type=F32)
    qus = jnp.dot(cqn, wuqs_ref[...], preferred_element_type=F32)
    qm = (qu * jnp.concatenate([cq_t] * ML_H, axis=1) + qus * jnp.concatenate([sq_t] * ML_H, axis=1))
    qm_ref[0] = (qm * ((ML_NOPE + ROT) ** -0.5 * LOG2E)).T.reshape(ML_H, LANE, tm).astype(BF16)
    kn = jnp.dot(ckvn, wkn_ref[...], preferred_element_type=F32)
    for hh in range(ML_H):
        km_ref[0, hh] = (kn[:, hh * LANE:(hh + 1) * LANE] + kr_slot).astype(BF16)
    vm_t = jnp.dot(ckvn, wvm_ref[...], preferred_element_type=F32).T
    for hh in range(ML_H):
        vm_ref[0, hh] = jnp.concatenate([vm_t[hh * ML_V:(hh + 1) * ML_V], ones_rows], axis=0).astype(BF16)

    hg_ref[0] = jnp.dot(h, w3_ref[...], preferred_element_type=F32)


def _project(xa, modl, modc, g, rope, wts, n_lat):
    b, t, d = xa.shape
    tm = _pick(t, (384, 256, 128))
    w1, w1s, w2, w3, gcq, gckv, wuq, wuqs, wkn, wvm = wts
    nhg = w3.shape[1]
    rows = lambda hs, w: (jax.ShapeDtypeStruct((b, hs, t, w), BF16),
                          pl.BlockSpec((1, hs, tm, w), lambda bi, i: (bi, 0, i, 0)))
    cols = lambda hs, r: (jax.ShapeDtypeStruct((b, hs, r, t), BF16),
                          pl.BlockSpec((1, hs, r, tm), lambda bi, i: (bi, 0, 0, i)))
    outs = (cols(2 * DA_H, 2 * DA_D), rows(DA_H, 2 * DA_D), cols(DA_H, VT_ROWS),
            cols(ML_H, LANE), rows(ML_H, LANE), cols(ML_H, VT_ROWS),
            (jax.ShapeDtypeStruct((b, t, nhg), F32), pl.BlockSpec((1, tm, nhg), lambda bi, i: (bi, i, 0))))
    out_shapes = tuple(o[0] for o in outs)
    out_specs = tuple(o[1] for o in outs)
    in_specs = [
        pl.BlockSpec((1, tm, d), lambda bi, i: (bi, i, 0)),
        pl.BlockSpec((1, N_MOD, d), lambda bi, i: (bi, 0, 0)),
        _const_spec(modc.shape), _const_spec(g.shape),
        pl.BlockSpec((tm, rope.shape[1]), lambda bi, i: (i, 0)),
    ] + [_const_spec(w.shape) for w in wts]
    return pl.pallas_call(
        functools.partial(_proj_body, n_lat),
        grid=(b, t // tm),
        in_specs=in_specs, out_specs=out_specs, out_shape=out_shapes,
        compiler_params=_cparams(("arbitrary", "arbitrary")),
        name="project",
    )(xa, modl, modc, g, rope, *wts)


def _flash_body(n_kv, group, diff_scale, rc, lam_ref, g_ref, qt_ref, k_ref, vt_ref, o_ref, m_ref, acc_ref):
    ki = pl.program_id(2)
    nk = pl.num_programs(2)
    tq = qt_ref.shape[3]

    @pl.when(ki == 0)
    def _():
        m_ref[...] = jnp.full(m_ref.shape, -jnp.inf, F32)
        acc_ref[...] = jnp.zeros(acc_ref.shape, F32)

    chains = [(hh, c0) for hh in range(n_kv) for c0 in range(0, group * tq, rc)]

    def scores(hh, c0):
        qt = qt_ref[0, hh * group + c0 // tq, :, c0 % tq:c0 % tq + rc]
        return jnp.dot(k_ref[0, hh], qt, preferred_element_type=F32)

    def softmax(hh, c0, s):
        cols = slice(c0, c0 + rc)
        m_prev = m_ref[hh, :, cols]
        m_new = jnp.maximum(m_prev, jnp.max(s, axis=0, keepdims=True))
        m_ref[hh, :, cols] = m_new
        return jnp.exp2(s - m_new[0:1]).astype(BF16), jnp.exp2(m_prev - m_new)[0:1]

    def accumulate(hh, c0, p, alpha):
        cols = slice(c0, c0 + rc)
        acc_ref[hh, :, cols] = alpha * acc_ref[hh, :, cols] + jnp.dot(
            vt_ref[0, hh], p, preferred_element_type=F32)

    n = len(chains)
    s_q = {i: scores(*chains[i]) for i in range(min(LOOKAHEAD, n))}
    p_q = {0: softmax(*chains[0], s_q.pop(0))}
    for ci in range(n):
        if ci + LOOKAHEAD < n:
            s_q[ci + LOOKAHEAD] = scores(*chains[ci + LOOKAHEAD])
        if ci + 1 < n:
            p_q[ci + 1] = softmax(*chains[ci + 1], s_q.pop(ci + 1))
        accumulate(*chains[ci], *p_q.pop(ci))

    @pl.when(ki == nk - 1)
    def _():
        outs = []
        for hh in range(n_kv):
            acc = acc_ref[hh]
            o = acc[:ML_V] / acc[ML_V:ML_V + 1]
            if group == 2:
                od = o[:, :tq] - lam_ref[0, 0] * o[:, tq:]
                ms = jnp.mean(od * od, axis=0, keepdims=True)
                o = od * lax.rsqrt(ms + EPS) * jnp.concatenate([g_ref[...]] * (tq // LANE), axis=1) * diff_scale
            outs.append(o.T)
        o_ref[0] = jnp.concatenate(outs, axis=-1).astype(o_ref.dtype)


def _flash(qt, k, vt, lam, g, *, group, diff_scale, t_q, q_off, t_kv, kv_off, tq, tk):
    b, hq, dk, _ = qt.shape
    n_kv = k.shape[1]
    assert hq == n_kv * group and q_off % tq == 0 and kv_off % tk == 0
    qo, ko = q_off // tq, kv_off // tk
    return pl.pallas_call(
        functools.partial(_flash_body, n_kv, group, diff_scale, min(FLASH_ROWS, tq)),
        grid=(b, t_q // tq, t_kv // tk),
        in_specs=[
            pl.BlockSpec(memory_space=pltpu.SMEM),
            _const_spec(g.shape),
            pl.BlockSpec((1, hq, dk, tq), lambda bi, qi, ki: (bi, 0, 0, qi + qo)),
            pl.BlockSpec((1, n_kv, tk, dk), lambda bi, qi, ki: (bi, 0, ki + ko, 0)),
            pl.BlockSpec((1, n_kv, VT_ROWS, tk), lambda bi, qi, ki: (bi, 0, 0, ki + ko)),
        ],
        out_specs=pl.BlockSpec((1, tq, n_kv * ML_V), lambda bi, qi, ki: (bi, qi, 0)),
        out_shape=jax.ShapeDtypeStruct((b, t_q, n_kv * ML_V), BF16),
        scratch_shapes=[pltpu.VMEM((n_kv, 8, group * tq), F32),
                        pltpu.VMEM((n_kv, VT_ROWS, group * tq), F32)],
        compiler_params=_cparams(("arbitrary", "arbitrary", "arbitrary")),
        name="flash_da" if group == 2 else "flash_mla",
    )(lam, g, qt, k, vt)


def _cumsum_rows(x):
    n = x.shape[0]
    rows = lax.broadcasted_iota(jnp.int32, (n, 1), 0)
    d = 1
    while d < n:
        x = x + jnp.where(rows >= d, pltpu.roll(x, d, axis=0), 0.0)
        d *= 2
    return x


def _dot_nt(a, b):
    return lax.dot_general(a, b, (((1,), (1,)), ((), ())), preferred_element_type=F32)


def _pad_rows(x, start, total):
    parts = []
    if start:
        parts.append(jnp.zeros((start, x.shape[1]), x.dtype))
    parts.append(x)
    if total - start - x.shape[0]:
        parts.append(jnp.zeros((total - start - x.shape[0], x.shape[1]), x.dtype))
    return jnp.concatenate(parts, axis=0) if len(parts) > 1 else x


def _hgrn_body(qf_ref, zf_ref, vf_ref, qb_ref, zb_ref, vb_ref, lb_ref, e_ref, of_ref, ob_ref, stf_ref, stb_ref):
    @pl.when(pl.program_id(1) == 0)
    def _():
        stf_ref[...] = jnp.zeros(stf_ref.shape, F32)
        stb_ref[...] = jnp.zeros(stb_ref.shape, F32)

    c, nsub, kw = HG_CHUNK, HG_CHUNK // HG_SUB, HG_KW
    qs = (qf_ref[0], qb_ref[0])
    vs = (vf_ref[0], vb_ref[0])
    sts = (stf_ref[...], stb_ref[...])

    z = jnp.concatenate([zf_ref[0], zb_ref[0]], axis=1)
    lb = jnp.concatenate([lb_ref[0:1], lb_ref[1:2]], axis=1)
    f = lb + (1.0 - lb) * jax.nn.sigmoid(z)
    kk2 = (1.0 - lb) * jax.nn.sigmoid(-z)
    logf = jnp.log(jnp.maximum(f, FORGET_FLOOR))
    bf = _cumsum_rows(logf)
    kks = (kk2[:, :kw], kk2[:, kw:])
    bs = (bf[:, :kw] * LOG2E, (bf[c - 1:c, kw:] - bf[:, kw:] + logf[:, kw:]) * LOG2E)
    b_ends = (bs[0][c - 1:c], bs[1][0:1])

    outs = [_dot_nt((qs[d] * jnp.exp2(bs[d])).astype(BF16), sts[d].astype(BF16)) for d in range(2)]

    head_r = lax.broadcasted_iota(jnp.int32, (HG_H * c, 1), 0) // c
    same_blk = head_r == lax.broadcasted_iota(jnp.int32, (1, HG_H * c), 1) // c
    v_keep = head_r == lax.broadcasted_iota(jnp.int32, (1, HG_VW), 1) // HG_V
    a_mats = []
    for d in range(2):
        q, kk, b = qs[d], kks[d], bs[d]
        q_slots, k_slots = [], []
        for i in (range(nsub - 1) if d else range(1, nsub)):
            r0 = i * HG_SUB
            if d:
                m, k0, k1 = b[r0 + HG_SUB:r0 + HG_SUB + 1], r0 + HG_SUB, c
            else:
                m, k0, k1 = b[r0 - 1:r0], 0, r0
            q_slots.append(_pad_rows(q[r0:r0 + HG_SUB] * jnp.exp2(b[r0:r0 + HG_SUB] - m), r0, c).astype(BF16))
            k_slots.append(_pad_rows(kk[k0:k1] * jnp.exp2(m - b[k0:k1]), k0, c).astype(BF16))
        stack = lambda slots: jnp.concatenate(
            [jnp.concatenate([x[:, hh * HG_K:(hh + 1) * HG_K] for x in slots], axis=1) for hh in range(HG_H)], axis=0)
        a_mats.append(_dot_nt(stack(q_slots), stack(k_slots)))

    terms = []
    for d in range(2):
        q, kk, b = qs[d], kks[d], bs[d]
        for i in range(nsub):
            r0 = i * HG_SUB
            qb, bb, kb = q[r0:r0 + HG_SUB], b[r0:r0 + HG_SUB], kk[r0:r0 + HG_SUB]
            terms += [(qb * jnp.exp2(bb - bb[s:s + 1]) * kb[s:s + 1]).astype(BF16) for s in range(HG_SUB)]
    a_bc = jnp.dot(jnp.concatenate(terms, axis=0), e_ref[...], preferred_element_type=F32)

    for d in range(2):
        v_stack = jnp.where(v_keep, jnp.concatenate([vs[d]] * HG_H, axis=0), 0.0).astype(BF16)
        o_stack = jnp.dot(jnp.where(same_blk, a_mats[d], 0.0).astype(BF16), v_stack, preferred_element_type=F32)
        for hh in range(HG_H):
            outs[d] = outs[d] + o_stack[hh * c:(hh + 1) * c]

    pair = lax.broadcasted_iota(jnp.int32, (HG_SUB * HG_SUB, 1), 0)
    for d in range(2):
        keep = (pair % HG_SUB <= pair // HG_SUB) if d else (pair % HG_SUB >= pair // HG_SUB)
        keep = jnp.broadcast_to(keep, (HG_SUB * HG_SUB, HG_VW))
        diag = []
        for i in range(nsub):
            r0 = i * HG_SUB
            base = (d * nsub + i) * HG_SUB * HG_SUB
            blk = jnp.where(keep, a_bc[base:base + HG_SUB * HG_SUB], 0.0)
            vb = vs[d][r0:r0 + HG_SUB]
            od = blk[0:HG_SUB] * vb[0:1]
            for s in range(1, HG_SUB):
                od = od + blk[s * HG_SUB:(s + 1) * HG_SUB] * vb[s:s + 1]
            diag.append(od)
        outs[d] = outs[d] + jnp.concatenate(diag, axis=0)

    of_ref[0] = outs[0]
    ob_ref[0] = outs[1]

    same_head = (lax.broadcasted_iota(jnp.int32, (HG_VW, 1), 0) // HG_V
                 == lax.broadcasted_iota(jnp.int32, (1, HG_KW), 1) // HG_K)
    for d, st_ref in enumerate((stf_ref, stb_ref)):
        kd = kks[d] * jnp.exp2(b_ends[d] - bs[d])
        upd = jnp.dot(vs[d].T.astype(BF16), kd.astype(BF16), preferred_element_type=F32)
        st_ref[...] = sts[d] * jnp.exp2(b_ends[d]) + jnp.where(same_head, upd, 0.0)


def _hgrn(hg, lb2, e_mat, n_lat):
    b, t, _ = hg.shape
    c = HG_CHUNK
    nc, nlc = t // c, n_lat // c
    fwd = lambda i: (i + nlc) % nc
    bwd = lambda i: nc - 1 - i
    v_blk = 3 * HG_KW // HG_VW
    return pl.pallas_call(
        _hgrn_body,
        grid=(b, nc),
        in_specs=[
            pl.BlockSpec((1, c, HG_KW), lambda bi, i: (bi, fwd(i), 0)),
            pl.BlockSpec((1, c, HG_KW), lambda bi, i: (bi, fwd(i), 1)),
            pl.BlockSpec((1, c, HG_VW), lambda bi, i: (bi, fwd(i), v_blk)),
            pl.BlockSpec((1, c, HG_KW), lambda bi, i: (bi, bwd(i), 0)),
            pl.BlockSpec((1, c, HG_KW), lambda bi, i: (bi, bwd(i), 2)),
            pl.BlockSpec((1, c, HG_VW), lambda bi, i: (bi, bwd(i), v_blk)),
            _const_spec(lb2.shape), _const_spec(e_mat.shape),
        ],
        out_specs=(pl.BlockSpec((1, c, HG_VW), lambda bi, i: (bi, fwd(i), 0)),
                   pl.BlockSpec((1, c, HG_VW), lambda bi, i: (bi, bwd(i), 0))),
        out_shape=(jax.ShapeDtypeStruct((b, t, HG_VW), F32), jax.ShapeDtypeStruct((b, t, HG_VW), F32)),
        scratch_shapes=[pltpu.VMEM((HG_VW, HG_KW), F32), pltpu.VMEM((HG_VW, HG_KW), F32)],
        compiler_params=_cparams(("arbitrary", "arbitrary")),
        name="hgrn2",
    )(hg, hg, hg, hg, hg, hg, lb2, e_mat)


def _mix_mlp_body(n_lat, ff_chunk, final, x_ref, oda_ref, oml_ref, of_ref, ob_ref, gate_ref, modl_ref, modc_ref,
                  ghn_ref, gmlp_ref, gfin_ref, grp_ref, wda_ref, wml_ref, whg_ref, w1_ref, w2_ref, o_ref):
    tm = x_ref.shape[1]
    rows = pl.program_id(1) * tm + lax.broadcasted_iota(jnp.int32, (tm, 1), 0)
    is_ctx = rows >= n_lat
    modl, modc = modl_ref[0], modc_ref[...]

    oh = of_ref[0] + ob_ref[0]
    sq = oh * oh
    sq_hi = sq.astype(BF16)
    sq_lo = (sq - sq_hi.astype(F32)).astype(BF16)
    ms = (jnp.dot(sq_hi, grp_ref[...], preferred_element_type=F32)
          + jnp.dot(sq_lo, grp_ref[...], preferred_element_type=F32)) * (1.0 / HG_V)
    gt = gate_ref[0]
    yh = oh * lax.rsqrt(ms + EPS) * ghn_ref[...] * (gt * jax.nn.sigmoid(gt))

    mix = (jnp.dot(oda_ref[0], wda_ref[...], preferred_element_type=F32)
           + jnp.dot(oml_ref[0], wml_ref[...], preferred_element_type=F32)
           + jnp.dot(yh.astype(BF16), whg_ref[...], preferred_element_type=F32))
    g_mix = jnp.where(is_ctx, modc[2:3], modl[2:3])
    x1 = x_ref[0] + g_mix * mix

    hn = _ada_rows(x1, gmlp_ref[...], modl, modc, is_ctx, 3, 4).astype(BF16)
    dff = w1_ref.shape[1]
    acc = jnp.zeros(x1.shape, F32)
    for j in range(dff // ff_chunk):
        u = jnp.dot(hn, w1_ref[:, j * ff_chunk:(j + 1) * ff_chunk], preferred_element_type=F32)
        u = jnp.maximum(u, 0.0)
        acc = acc + jnp.dot((u * u).astype(BF16), w2_ref[j * ff_chunk:(j + 1) * ff_chunk, :],
                            preferred_element_type=F32)
    g_mlp = jnp.where(is_ctx, modc[5:6], modl[5:6])
    x2 = x1 + g_mlp * acc
    if final:
        x2 = _rms(x2, gfin_ref[...])
    o_ref[0] = x2


def _mix_mlp(xa, oda, oml, o_f, o_b, hg, modl, modc, ghn, gmlp, gfin, grp, wda, wml, whg, w1, w2, n_lat, final):
    b, t, d = xa.shape
    t_out = n_lat if final else t
    tm = _pick(t_out, (512, 384, 256, 128)) if final else _pick(t, (384, 256, 128))
    gate_blk = (3 * HG_KW + HG_VW) // HG_VW
    row = lambda w: pl.BlockSpec((1, tm, w), lambda bi, i: (bi, i, 0))
    consts = (modc, ghn, gmlp, gfin, grp, wda, wml, whg, w1, w2)
    return pl.pallas_call(
        functools.partial(_mix_mlp_body, n_lat, 1024, final),
        grid=(b, t_out // tm),
        in_specs=[row(d), row(DA_W), row(ML_W), row(HG_VW), row(HG_VW),
                  pl.BlockSpec((1, tm, HG_VW), lambda bi, i: (bi, i, gate_blk)),
                  pl.BlockSpec((1, N_MOD, d), lambda bi, i: (bi, 0, 0))]
                 + [_const_spec(a.shape) for a in consts],
        out_specs=row(d),
        out_shape=jax.ShapeDtypeStruct((b, t_out, d), F32),
        compiler_params=_cparams(("arbitrary", "arbitrary")),
        name="mix_mlp",
    )(xa, oda, oml, o_f, o_b, hg, modl, *consts)


def _swap_cols(w):
    n = w.shape[-1]
    j = jnp.arange(n)
    src = (j // 16) * 16 + (j % 16 + 8) % 16
    return w[..., src]


def _slot_cols(w, width):
    k, n = w.shape
    hs = n // width
    return jnp.pad(w.reshape(k, hs, width), ((0, 0), (0, 0), (0, LANE - width))).reshape(k, hs * LANE)


def _layer_weights(w_in, w_uq, w_ukv):
    o = 0
    parts = {}
    for name, wdt in (("da_q", DA_W), ("da_k", DA_W), ("da_v", DA_W), ("cq", ML_QR), ("ckv", ML_KVR), ("kr", ROT),
                      ("hq", HG_KW), ("hzf", HG_KW), ("hzb", HG_KW), ("hv", HG_VW), ("hgate", HG_VW)):
        parts[name] = w_in[:, o:o + wdt]
        o += wdt
    d = w_in.shape[0]
    w1 = jnp.concatenate([parts["da_q"], parts["da_k"], jnp.zeros((d, ML_NOPE), F32), parts["kr"],
                          jnp.zeros((d, LANE - ML_NOPE - ROT), F32)], axis=1)
    w2 = jnp.concatenate([parts["da_v"], parts["cq"], parts["ckv"]], axis=1)
    w3 = jnp.concatenate([parts[n] for n in ("hq", "hzf", "hzb", "hv", "hgate")], axis=1)
    uq = w_uq.reshape(ML_QR, ML_H, ML_NOPE + ROT)
    uq_sw = jnp.concatenate([jnp.zeros((ML_QR, ML_H, ML_NOPE), F32), _swap_cols(uq[..., ML_NOPE:])], axis=-1)
    pad_q = lambda a: jnp.pad(a, ((0, 0), (0, 0), (0, LANE - ML_NOPE - ROT))).reshape(ML_QR, ML_H * LANE)
    ukv = w_ukv.reshape(ML_KVR, ML_H, ML_NOPE + ML_V)
    wkn = _slot_cols(ukv[..., :ML_NOPE].reshape(ML_KVR, ML_H * ML_NOPE), ML_NOPE)
    wvm = ukv[..., ML_NOPE:].reshape(ML_KVR, ML_H * ML_V)
    bf = lambda a: a.astype(BF16)
    return bf(w1), bf(_swap_cols(w1)), bf(w2), bf(w3), bf(pad_q(uq)), bf(pad_q(uq_sw)), bf(wkn), bf(wvm)


def _rope_tables(n_lat, n_ctx):
    rows = n_lat // GRID_W
    row = jnp.repeat(jnp.arange(rows, dtype=F32), GRID_W)
    col = jnp.tile(jnp.arange(GRID_W, dtype=F32), rows)
    n_freq = ROT // 4
    inv = ROPE_THETA ** (-jnp.arange(n_freq, dtype=F32) / n_freq)
    ar, ac = row[:, None] * inv, col[:, None] * inv
    c32 = jnp.concatenate([jnp.cos(ar), jnp.cos(ar), jnp.cos(ac), jnp.cos(ac)], axis=1)
    s32 = jnp.concatenate([-jnp.sin(ar), jnp.sin(ar), -jnp.sin(ac), jnp.sin(ac)], axis=1)
    c32 = jnp.concatenate([c32, jnp.ones((n_ctx, ROT), F32)], axis=0)
    s32 = jnp.concatenate([s32, jnp.zeros((n_ctx, ROT), F32)], axis=0)
    t = n_lat + n_ctx
    one, zero = jnp.ones((t, ML_NOPE), F32), jnp.zeros((t, ML_NOPE), F32)
    pad = jnp.zeros((t, LANE - ML_NOPE - ROT), F32)
    return jnp.concatenate([jnp.tile(c32, (1, LANE // ROT)), jnp.tile(s32, (1, LANE // ROT)),
                            one, c32, pad, zero, s32, pad], axis=1)


def kernel(x, c, ctx, c_ctx, w_mod, b_mod, g_mix, g_mlp, w_in, w_out, da_lambda, da_subln_g, mla_g_cq, mla_g_ckv,
           mla_w_uq, mla_w_ukv, hg_lower_bounds, hg_norm_g, w_ff1, w_ff2, g_final):
    bsz, n_lat, d = x.shape
    n_ctx = ctx.shape[1]
    depth = w_in.shape[0]
    t = n_lat + n_ctx

    lb = jax.nn.softmax(hg_lower_bounds.astype(F32), axis=1)
    lb = jnp.cumsum(lb, axis=1) - lb[:, :1]
    cond = jnp.concatenate([c, c_ctx[None]], axis=0)
    mod = _modulation(cond, w_mod, b_mod).reshape(depth, bsz + 1, N_MOD, d)
    rope = _rope_tables(n_lat, n_ctx)

    grp = (jnp.arange(HG_VW)[:, None] // HG_V == jnp.arange(HG_VW)[None, :] // HG_V).astype(BF16)
    e_mat = (jnp.arange(HG_KW)[:, None] // HG_K == jnp.arange(HG_VW)[None, :] // HG_V).astype(BF16)

    tq_da = _pick(n_lat, TQ_DA)
    tq_ml = _pick(n_lat, TQ_ML)
    tk = _pick(t, TK)
    tc = _pick(n_ctx, (256, 128))

    xa = jnp.concatenate([x, ctx], axis=1)
    for l in range(depth):
        last = l == depth - 1
        lam_init = 0.8 - 0.6 * math.exp(-0.3 * l)
        lam = (jnp.exp(jnp.sum(da_lambda[l, 0] * da_lambda[l, 1]))
               - jnp.exp(jnp.sum(da_lambda[l, 2] * da_lambda[l, 3])) + lam_init).astype(F32).reshape(1, 1)
        modl, modc = mod[l, :bsz], mod[l, bsz]
        w1, w1s, w2, w3, wuq, wuqs, wkn, wvm = _layer_weights(w_in[l], mla_w_uq[l], mla_w_ukv[l])
        wts = (w1, w1s, w2, w3, mla_g_cq[l][None], mla_g_ckv[l][None], wuq, wuqs, wkn, wvm)
        qda, kda, vda, qm, km, vm, hg = _project(xa, modl, modc, g_mix[l][None], rope, wts, n_lat)

        g_da = jnp.broadcast_to(da_subln_g[l][:, None], (2 * DA_D, LANE))
        da_kw = dict(group=2, diff_scale=1.0 - lam_init)
        ml_kw = dict(group=1, diff_scale=1.0)
        oda = _flash(qda, kda, vda, lam, g_da, t_q=n_lat, q_off=0, t_kv=t, kv_off=0, tq=tq_da, tk=tk, **da_kw)
        oml = _flash(qm, km, vm, lam, g_da, t_q=n_lat, q_off=0, t_kv=t, kv_off=0, tq=tq_ml, tk=tk, **ml_kw)
        if not last:
            ctx_kw = dict(t_q=n_ctx, q_off=n_lat, t_kv=n_ctx, kv_off=n_lat, tq=tc, tk=tc)
            oda = jnp.concatenate([oda, _flash(qda, kda, vda, lam, g_da, **ctx_kw, **da_kw)], axis=1)
            oml = jnp.concatenate([oml, _flash(qm, km, vm, lam, g_da, **ctx_kw, **ml_kw)], axis=1)

        o_f, o_b = _hgrn(hg, jnp.stack([lb[0, l], lb[1, l]]), e_mat, n_lat)

        ghn = jnp.tile(hg_norm_g[l], HG_H)[None]
        bf = lambda a: a.astype(BF16)
        xa = _mix_mlp(xa, oda, oml, o_f, o_b, hg, modl, modc, ghn, g_mlp[l][None], g_final[None], grp,
                      bf(w_out[l, :DA_W]), bf(w_out[l, DA_W:DA_W + ML_W]), bf(w_out[l, DA_W + ML_W:]),
                      bf(w_ff1[l]), bf(w_ff2[l]), n_lat, last)
    return xa
```

```python
import functools
import math

import jax
import jax.numpy as jnp
from jax import lax
from jax.experimental import pallas as pl
from jax.experimental.pallas import tpu as pltpu

F32 = jnp.float32
BF16 = jnp.bfloat16

EPS = 1e-6
ROPE_THETA = 10000.0
GRID_W = 64
ROT = 32
DA_H, DA_D = 6, 32
DA_W = DA_H * 2 * DA_D
ML_H, ML_QR, ML_KVR, ML_NOPE, ML_V = 6, 256, 128, 64, 64
ML_W = ML_H * ML_V
HG_H, HG_K, HG_V = 4, 128, 64
HG_KW, HG_VW = HG_H * HG_K, HG_H * HG_V
FORGET_FLOOR = 1e-30
N_MOD = 6
LOG2E = 1.4426950408889634

HG_CHUNK = 64
HG_SUB = 8
LANE = 128
VMEM_LIMIT = 56 * 1024 * 1024
TQ_DA = (1024, 512, 256, 128)
TQ_ML = (2048, 1024, 512, 256, 128)
TK = (768, 512, 384, 256, 128)
VT_ROWS = 80
LOOKAHEAD = 2
FLASH_KEYS = 256
FLASH_ROWS = 512


def _cparams(sem):
    return pltpu.CompilerParams(dimension_semantics=sem, vmem_limit_bytes=VMEM_LIMIT)


def _const_spec(shape):
    nd = len(shape)
    return pl.BlockSpec(shape, lambda *_: (0,) * nd)


def _pick(n, cands):
    for c in cands:
        if n % c == 0:
            return c
    raise ValueError(f"no tile for {n} in {cands}")


def _mod_body(c_ref, w_ref, b_ref, o_ref):
    cc = c_ref[...]
    s = cc * jax.nn.sigmoid(cc)
    o_ref[0] = jnp.dot(s.astype(BF16), w_ref[0].astype(BF16), preferred_element_type=F32) + b_ref[0]


def _modulation(cond, w_mod, b_mod):
    depth, d, nm = w_mod.shape
    r = cond.shape[0]
    tn = _pick(nm, (1536, 1024, 512, 128))
    return pl.pallas_call(
        _mod_body,
        grid=(depth, nm // tn),
        in_specs=[pl.BlockSpec((r, d), lambda l, j: (0, 0)),
                  pl.BlockSpec((1, d, tn), lambda l, j: (l, 0, j)),
                  pl.BlockSpec((1, 1, tn), lambda l, j: (l, 0, j))],
        out_specs=pl.BlockSpec((1, r, tn), lambda l, j: (l, 0, j)),
        out_shape=jax.ShapeDtypeStruct((depth, r, nm), F32),
        compiler_params=_cparams(("arbitrary", "arbitrary")),
        name="modulation",
    )(cond, w_mod, b_mod.reshape(depth, 1, nm))


def _ada_rows(x, g, modl, modc, is_ctx, i_shift, i_scale):
    ms = jnp.mean(x * x, axis=-1, keepdims=True)
    y = x * lax.rsqrt(ms + EPS) * g
    shift = jnp.where(is_ctx, modc[i_shift:i_shift + 1], modl[i_shift:i_shift + 1])
    scale = jnp.where(is_ctx, modc[i_scale:i_scale + 1], modl[i_scale:i_scale + 1])
    return y * (1.0 + scale) + shift


def _rms(x, g):
    return x * lax.rsqrt(jnp.mean(x * x, axis=-1, keepdims=True) + EPS) * g


def _proj_body(n_lat, x_ref, modl_ref, modc_ref, g_ref, rope_ref, w1_ref, w1s_ref, w2_ref, w3_ref,
               gcq_ref, gckv_ref, wuq_ref, wuqs_ref, wkn_ref, wvm_ref,
               qda_ref, kda_ref, vda_ref, qm_ref, km_ref, vm_ref, hg_ref):
    tm = x_ref.shape[1]
    rows = pl.program_id(1) * tm + lax.broadcasted_iota(jnp.int32, (tm, 1), 0)
    is_ctx = rows >= n_lat
    h = _ada_rows(x_ref[0], g_ref[...], modl_ref[0], modc_ref[...], is_ctx, 0, 1).astype(BF16)

    rope = rope_ref[...]
    c4, s4, cq_t, sq_t = (rope[:, k * LANE:(k + 1) * LANE] for k in range(4))
    n1 = w1_ref.shape[1] // LANE
    p1 = jnp.dot(h, w1_ref[...], preferred_element_type=F32)
    p1s = jnp.dot(h, w1s_ref[...], preferred_element_type=F32)
    rot = p1 * jnp.concatenate([c4] * n1, axis=1) + p1s * jnp.concatenate([s4] * n1, axis=1)

    rq_t = (rot[:, :DA_W] * (DA_D ** -0.5 * LOG2E)).T
    row64 = lax.broadcasted_iota(jnp.int32, (2 * DA_D, 1), 0)
    for vh in range(2 * DA_H):
        hh, comp = vh // 2, vh % 2
        keep = (row64 >= DA_D) if comp else (row64 < DA_D)
        qda_ref[0, vh] = jnp.where(keep, rq_t[hh * 2 * DA_D:(hh + 1) * 2 * DA_D], 0.0).astype(BF16)
    for hh in range(DA_H):
        kda_ref[0, hh] = rot[:, DA_W + hh * 2 * DA_D:DA_W + (hh + 1) * 2 * DA_D].astype(BF16)
    kr_slot = rot[:, 2 * DA_W:2 * DA_W + LANE]

    ones_rows = (lax.broadcasted_iota(jnp.int32, (VT_ROWS - ML_V, tm), 0) == 0).astype(F32)
    p2 = jnp.dot(h, w2_ref[...], preferred_element_type=F32)
    v_t = p2[:, :DA_W].T
    for hh in range(DA_H):
        vda_ref[0, hh] = jnp.concatenate([v_t[hh * ML_V:(hh + 1) * ML_V], ones_rows], axis=0).astype(BF16)
    cqn = _rms(p2[:, DA_W:DA_W + ML_QR], gcq_ref[...]).astype(BF16)
    ckvn = _rms(p2[:, DA_W + ML_QR:DA_W + ML_QR + ML_KVR], gckv_ref[...]).astype(BF16)

    qu = jnp.dot(cqn, wuq_ref[...], preferred_element_type=F32)
    qus = jnp.dot(cqn, wuqs_ref[...], preferred_element_type=F32)
    qm = (qu * jnp.concatenate([cq_t] * ML_H, axis=1) + qus * jnp.concatenate([sq_t] * ML_H, axis=1))
    qm_ref[0] = (qm * ((ML_NOPE + ROT) ** -0.5 * LOG2E)).T.reshape(ML_H, LANE, tm).astype(BF16)
    kn = jnp.dot(ckvn, wkn_ref[...], preferred_element_type=F32)
    for hh in range(ML_H):
        km_ref[0, hh] = (kn[:, hh * LANE:(hh + 1) * LANE] + kr_slot).astype(BF16)
    vm_t = jnp.dot(ckvn, wvm_ref[...], preferred_element_type=F32).T
    for hh in range(ML_H):
        vm_ref[0, hh] = jnp.concatenate([vm_t[hh * ML_V:(hh + 1) * ML_V], ones_rows], axis=0).astype(BF16)

    hg_ref[0] = jnp.dot(h, w3_ref[...], preferred_element_type=F32)


def _project(xa, modl, modc, g, rope, wts, n_lat):
    b, t, d = xa.shape
    tm = _pick(t, (384, 256, 128))
    w1, w1s, w2, w3, gcq, gckv, wuq, wuqs, wkn, wvm = wts
    nhg = w3.shape[1]
    rows = lambda hs, w: (jax.ShapeDtypeStruct((b, hs, t, w), BF16),
                          pl.BlockSpec((1, hs, tm, w), lambda bi, i: (bi, 0, i, 0)))
    cols = lambda hs, r: (jax.ShapeDtypeStruct((b, hs, r, t), BF16),
                          pl.BlockSpec((1, hs, r, tm), lambda bi, i: (bi, 0, 0, i)))
    outs = (cols(2 * DA_H, 2 * DA_D), rows(DA_H, 2 * DA_D), cols(DA_H, VT_ROWS),
            cols(ML_H, LANE), rows(ML_H, LANE), cols(ML_H, VT_ROWS),
            (jax.ShapeDtypeStruct((b, t, nhg), F32), pl.BlockSpec((1, tm, nhg), lambda bi, i: (bi, i, 0))))
    out_shapes = tuple(o[0] for o in outs)
    out_specs = tuple(o[1] for o in outs)
    in_specs = [
        pl.BlockSpec((1, tm, d), lambda bi, i: (bi, i, 0)),
        pl.BlockSpec((1, N_MOD, d), lambda bi, i: (bi, 0, 0)),
        _const_spec(modc.shape), _const_spec(g.shape),
        pl.BlockSpec((tm, rope.shape[1]), lambda bi, i: (i, 0)),
    ] + [_const_spec(w.shape) for w in wts]
    return pl.pallas_call(
        functools.partial(_proj_body, n_lat),
        grid=(b, t // tm),
        in_specs=in_specs, out_specs=out_specs, out_shape=out_shapes,
        compiler_params=_cparams(("arbitrary", "arbitrary")),
        name="project",
    )(xa, modl, modc, g, rope, *wts)


def _flash_body(n_kv, group, diff_scale, rc, lam_ref, g_ref, qt_ref, k_ref, vt_ref, o_ref, m_ref, acc_ref):
    ki = pl.program_id(2)
    nk = pl.num_programs(2)
    tq = qt_ref.shape[3]

    @pl.when(ki == 0)
    def _():
        m_ref[...] = jnp.full(m_ref.shape, -jnp.inf, F32)
        acc_ref[...] = jnp.zeros(acc_ref.shape, F32)

    tk = k_ref.shape[2]
    ks = min(FLASH_KEYS, tk)
    chains = [(hh, c0, k0) for hh in range(n_kv) for c0 in range(0, group * tq, rc) for k0 in range(0, tk, ks)]

    def scores(hh, c0, k0):
        qt = qt_ref[0, hh * group + c0 // tq, :, c0 % tq:c0 % tq + rc]
        return jnp.dot(k_ref[0, hh, k0:k0 + ks, :], qt, preferred_element_type=F32)

    def softmax(hh, c0, k0, s):
        cols = slice(c0, c0 + rc)
        m_prev = m_ref[hh, :, cols]
        m_new = jnp.maximum(m_prev, jnp.max(s, axis=0, keepdims=True))
        m_ref[hh, :, cols] = m_new
        return jnp.exp2(s - m_new[0:1]).astype(BF16), jnp.exp2(m_prev - m_new)[0:1]

    def accumulate(hh, c0, k0, p, alpha):
        cols = slice(c0, c0 + rc)
        acc_ref[hh, :, cols] = alpha * acc_ref[hh, :, cols] + jnp.dot(
            vt_ref[0, hh, :, k0:k0 + ks], p, preferred_element_type=F32)

    n = len(chains)
    s_q = {i: scores(*chains[i]) for i in range(min(LOOKAHEAD, n))}
    p_q = {0: softmax(*chains[0], s_q.pop(0))}
    for ci in range(n):
        if ci + LOOKAHEAD < n:
            s_q[ci + LOOKAHEAD] = scores(*chains[ci + LOOKAHEAD])
        if ci + 1 < n:
            p_q[ci + 1] = softmax(*chains[ci + 1], s_q.pop(ci + 1))
        accumulate(*chains[ci], *p_q.pop(ci))

    @pl.when(ki == nk - 1)
    def _():
        outs = []
        for hh in range(n_kv):
            acc = acc_ref[hh]
            o = acc[:ML_V] / acc[ML_V:ML_V + 1]
            if group == 2:
                od = o[:, :tq] - lam_ref[0, 0] * o[:, tq:]
                ms = jnp.mean(od * od, axis=0, keepdims=True)
                o = od * lax.rsqrt(ms + EPS) * jnp.concatenate([g_ref[...]] * (tq // LANE), axis=1) * diff_scale
            outs.append(o.T)
        o_ref[0] = jnp.concatenate(outs, axis=-1).astype(o_ref.dtype)


def _flash(qt, k, vt, lam, g, *, group, diff_scale, t_q, q_off, t_kv, kv_off, tq, tk):
    b, hq, dk, _ = qt.shape
    n_kv = k.shape[1]
    assert hq == n_kv * group and q_off % tq == 0 and kv_off % tk == 0
    qo, ko = q_off // tq, kv_off // tk
    return pl.pallas_call(
        functools.partial(_flash_body, n_kv, group, diff_scale, min(FLASH_ROWS, tq)),
        grid=(b, t_q // tq, t_kv // tk),
        in_specs=[
            pl.BlockSpec(memory_space=pltpu.SMEM),
            _const_spec(g.shape),
            pl.BlockSpec((1, hq, dk, tq), lambda bi, qi, ki: (bi, 0, 0, qi + qo)),
            pl.BlockSpec((1, n_kv, tk, dk), lambda bi, qi, ki: (bi, 0, ki + ko, 0)),
            pl.BlockSpec((1, n_kv, VT_ROWS, tk), lambda bi, qi, ki: (bi, 0, 0, ki + ko)),
        ],
        out_specs=pl.BlockSpec((1, tq, n_kv * ML_V), lambda bi, qi, ki: (bi, qi, 0)),
        out_shape=jax.ShapeDtypeStruct((b, t_q, n_kv * ML_V), BF16),
        scratch_shapes=[pltpu.VMEM((n_kv, 8, group * tq), F32),
                        pltpu.VMEM((n_kv, VT_ROWS, group * tq), F32)],
        compiler_params=_cparams(("arbitrary", "arbitrary", "arbitrary")),
        name="flash_da" if group == 2 else "flash_mla",
    )(lam, g, qt, k, vt)


def _cumsum_rows(x):
    n = x.shape[0]
    rows = lax.broadcasted_iota(jnp.int32, (n, 1), 0)
    d = 1
    while d < n:
        x = x + jnp.where(rows >= d, pltpu.roll(x, d, axis=0), 0.0)
        d *= 2
    return x


def _dot_nt(a, b):
    return lax.dot_general(a, b, (((1,), (1,)), ((), ())), preferred_element_type=F32)


def _pad_rows(x, start, total):
    parts = []
    if start:
        parts.append(jnp.zeros((start, x.shape[1]), x.dtype))
    parts.append(x)
    if total - start - x.shape[0]:
        parts.append(jnp.zeros((total - start - x.shape[0], x.shape[1]), x.dtype))
    return jnp.concatenate(parts, axis=0) if len(parts) > 1 else x


def _hgrn_body(qf_ref, zf_ref, vf_ref, qb_ref, zb_ref, vb_ref, lb_ref, e_ref, of_ref, ob_ref, stf_ref, stb_ref):
    @pl.when(pl.program_id(1) == 0)
    def _():
        stf_ref[...] = jnp.zeros(stf_ref.shape, F32)
        stb_ref[...] = jnp.zeros(stb_ref.shape, F32)

    c, nsub, kw = HG_CHUNK, HG_CHUNK // HG_SUB, HG_KW
    qs = (qf_ref[0], qb_ref[0])
    vs = (vf_ref[0], vb_ref[0])
    sts = (stf_ref[...], stb_ref[...])

    z = jnp.concatenate([zf_ref[0], zb_ref[0]], axis=1)
    lb = jnp.concatenate([lb_ref[0:1], lb_ref[1:2]], axis=1)
    f = lb + (1.0 - lb) * jax.nn.sigmoid(z)
    kk2 = (1.0 - lb) * jax.nn.sigmoid(-z)
    logf = jnp.log(jnp.maximum(f, FORGET_FLOOR))
    bf = _cumsum_rows(logf)
    kks = (kk2[:, :kw], kk2[:, kw:])
    bs = (bf[:, :kw] * LOG2E, (bf[c - 1:c, kw:] - bf[:, kw:] + logf[:, kw:]) * LOG2E)
    b_ends = (bs[0][c - 1:c], bs[1][0:1])

    outs = [_dot_nt((qs[d] * jnp.exp2(bs[d])).astype(BF16), sts[d].astype(BF16)) for d in range(2)]

    head_r = lax.broadcasted_iota(jnp.int32, (HG_H * c, 1), 0) // c
    same_blk = head_r == lax.broadcasted_iota(jnp.int32, (1, HG_H * c), 1) // c
    v_keep = head_r == lax.broadcasted_iota(jnp.int32, (1, HG_VW), 1) // HG_V
    a_mats = []
    for d in range(2):
        q, kk, b = qs[d], kks[d], bs[d]
        q_slots, k_slots = [], []
        for i in (range(nsub - 1) if d else range(1, nsub)):
            r0 = i * HG_SUB
            if d:
                m, k0, k1 = b[r0 + HG_SUB:r0 + HG_SUB + 1], r0 + HG_SUB, c
            else:
                m, k0, k1 = b[r0 - 1:r0], 0, r0
            q_slots.append(_pad_rows(q[r0:r0 + HG_SUB] * jnp.exp2(b[r0:r0 + HG_SUB] - m), r0, c).astype(BF16))
            k_slots.append(_pad_rows(kk[k0:k1] * jnp.exp2(m - b[k0:k1]), k0, c).astype(BF16))
        stack = lambda slots: jnp.concatenate(
            [jnp.concatenate([x[:, hh * HG_K:(hh + 1) * HG_K] for x in slots], axis=1) for hh in range(HG_H)], axis=0)
        a_mats.append(_dot_nt(stack(q_slots), stack(k_slots)))

    terms = []
    for d in range(2):
        q, kk, b = qs[d], kks[d], bs[d]
        for i in range(nsub):
            r0 = i * HG_SUB
            qb, bb, kb = q[r0:r0 + HG_SUB], b[r0:r0 + HG_SUB], kk[r0:r0 + HG_SUB]
            terms += [(qb * jnp.exp2(bb - bb[s:s + 1]) * kb[s:s + 1]).astype(BF16) for s in range(HG_SUB)]
    a_bc = jnp.dot(jnp.concatenate(terms, axis=0), e_ref[...], preferred_element_type=F32)

    for d in range(2):
        v_stack = jnp.where(v_keep, jnp.concatenate([vs[d]] * HG_H, axis=0), 0.0).astype(BF16)
        o_stack = jnp.dot(jnp.where(same_blk, a_mats[d], 0.0).astype(BF16), v_stack, preferred_element_type=F32)
        for hh in range(HG_H):
            outs[d] = outs[d] + o_stack[hh * c:(hh + 1) * c]

    pair = lax.broadcasted_iota(jnp.int32, (HG_SUB * HG_SUB, 1), 0)
    for d in range(2):
        keep = (pair % HG_SUB <= pair // HG_SUB) if d else (pair % HG_SUB >= pair // HG_SUB)
        keep = jnp.broadcast_to(keep, (HG_SUB * HG_SUB, HG_VW))
        diag = []
        for i in range(nsub):
            r0 = i * HG_SUB
            base = (d * nsub + i) * HG_SUB * HG_SUB
            blk = jnp.where(keep, a_bc[base:base + HG_SUB * HG_SUB], 0.0)
            vb = vs[d][r0:r0 + HG_SUB]
            od = blk[0:HG_SUB] * vb[0:1]
            for s in range(1, HG_SUB):
                od = od + blk[s * HG_SUB:(s + 1) * HG_SUB] * vb[s:s + 1]
            diag.append(od)
        outs[d] = outs[d] + jnp.concatenate(diag, axis=0)

    of_ref[0] = outs[0]
    ob_ref[0] = outs[1]

    same_head = (lax.broadcasted_iota(jnp.int32, (HG_VW, 1), 0) // HG_V
                 == lax.broadcasted_iota(jnp.int32, (1, HG_KW), 1) // HG_K)
    for d, st_ref in enumerate((stf_ref, stb_ref)):
        kd = kks[d] * jnp.exp2(b_ends[d] - bs[d])
        upd = jnp.dot(vs[d].T.astype(BF16), kd.astype(BF16), preferred_element_type=F32)
        st_ref[...] = sts[d] * jnp.exp2(b_ends[d]) + jnp.where(same_head, upd, 0.0)


def _hgrn(hg, lb2, e_mat, n_lat):
    b, t, _ = hg.shape
    c = HG_CHUNK
    nc, nlc = t // c, n_lat // c
    fwd = lambda i: (i + nlc) % nc
    bwd = lambda i: nc - 1 - i
    v_blk = 3 * HG_KW // HG_VW
    return pl.pallas_call(
        _hgrn_body,
        grid=(b, nc),
        in_specs=[
            pl.BlockSpec((1, c, HG_KW), lambda bi, i: (bi, fwd(i), 0)),
            pl.BlockSpec((1, c, HG_KW), lambda bi, i: (bi, fwd(i), 1)),
            pl.BlockSpec((1, c, HG_VW), lambda bi, i: (bi, fwd(i), v_blk)),
            pl.BlockSpec((1, c, HG_KW), lambda bi, i: (bi, bwd(i), 0)),
            pl.BlockSpec((1, c, HG_KW), lambda bi, i: (bi, bwd(i), 2)),
            pl.BlockSpec((1, c, HG_VW), lambda bi, i: (bi, bwd(i), v_blk)),
            _const_spec(lb2.shape), _const_spec(e_mat.shape),
        ],
        out_specs=(pl.BlockSpec((1, c, HG_VW), lambda bi, i: (bi, fwd(i), 0)),
                   pl.BlockSpec((1, c, HG_VW), lambda bi, i: (bi, bwd(i), 0))),
        out_shape=(jax.ShapeDtypeStruct((b, t, HG_VW), F32), jax.ShapeDtypeStruct((b, t, HG_VW), F32)),
        scratch_shapes=[pltpu.VMEM((HG_VW, HG_KW), F32), pltpu.VMEM((HG_VW, HG_KW), F32)],
        compiler_params=_cparams(("arbitrary", "arbitrary")),
        name="hgrn2",
    )(hg, hg, hg, hg, hg, hg, lb2, e_mat)


def _mix_mlp_body(n_lat, ff_chunk, final, x_ref, oda_ref, oml_ref, of_ref, ob_ref, gate_ref, modl_ref, modc_ref,
                  ghn_ref, gmlp_ref, gfin_ref, grp_ref, wda_ref, wml_ref, whg_ref, w1_ref, w2_ref, o_ref):
    tm = x_ref.shape[1]
    rows = pl.program_id(1) * tm + lax.broadcasted_iota(jnp.int32, (tm, 1), 0)
    is_ctx = rows >= n_lat
    modl, modc = modl_ref[0], modc_ref[...]

    oh = of_ref[0] + ob_ref[0]
    sq = oh * oh
    sq_hi = sq.astype(BF16)
    sq_lo = (sq - sq_hi.astype(F32)).astype(BF16)
    ms = (jnp.dot(sq_hi, grp_ref[...], preferred_element_type=F32)
          + jnp.dot(sq_lo, grp_ref[...], preferred_element_type=F32)) * (1.0 / HG_V)
    gt = gate_ref[0]
    yh = oh * lax.rsqrt(ms + EPS) * ghn_ref[...] * (gt * jax.nn.sigmoid(gt))

    mix = (jnp.dot(oda_ref[0], wda_ref[...], preferred_element_type=F32)
           + jnp.dot(oml_ref[0], wml_ref[...], preferred_element_type=F32)
           + jnp.dot(yh.astype(BF16), whg_ref[...], preferred_element_type=F32))
    g_mix = jnp.where(is_ctx, modc[2:3], modl[2:3])
    x1 = x_ref[0] + g_mix * mix

    hn = _ada_rows(x1, gmlp_ref[...], modl, modc, is_ctx, 3, 4).astype(BF16)
    dff = w1_ref.shape[1]
    acc = jnp.zeros(x1.shape, F32)
    for j in range(dff // ff_chunk):
        u = jnp.dot(hn, w1_ref[:, j * ff_chunk:(j + 1) * ff_chunk], preferred_element_type=F32)
        u = jnp.maximum(u, 0.0)
        acc = acc + jnp.dot((u * u).astype(BF16), w2_ref[j * ff_chunk:(j + 1) * ff_chunk, :],
                            preferred_element_type=F32)
    g_mlp = jnp.where(is_ctx, modc[5:6], modl[5:6])
    x2 = x1 + g_mlp * acc
    if final:
        x2 = _rms(x2, gfin_ref[...])
    o_ref[0] = x2


def _mix_mlp(xa, oda, oml, o_f, o_b, hg, modl, modc, ghn, gmlp, gfin, grp, wda, wml, whg, w1, w2, n_lat, final):
    b, t, d = xa.shape
    t_out = n_lat if final else t
    tm = _pick(t_out, (512, 384, 256, 128)) if final else _pick(t, (384, 256, 128))
    gate_blk = (3 * HG_KW + HG_VW) // HG_VW
    row = lambda w: pl.BlockSpec((1, tm, w), lambda bi, i: (bi, i, 0))
    consts = (modc, ghn, gmlp, gfin, grp, wda, wml, whg, w1, w2)
    return pl.pallas_call(
        functools.partial(_mix_mlp_body, n_lat, 1024, final),
        grid=(b, t_out // tm),
        in_specs=[row(d), row(DA_W), row(ML_W), row(HG_VW), row(HG_VW),
                  pl.BlockSpec((1, tm, HG_VW), lambda bi, i: (bi, i, gate_blk)),
                  pl.BlockSpec((1, N_MOD, d), lambda bi, i: (bi, 0, 0))]
                 + [_const_spec(a.shape) for a in consts],
        out_specs=row(d),
        out_shape=jax.ShapeDtypeStruct((b, t_out, d), F32),
        compiler_params=_cparams(("arbitrary", "arbitrary")),
        name="mix_mlp",
    )(xa, oda, oml, o_f, o_b, hg, modl, *consts)


def _swap_cols(w):
    n = w.shape[-1]
    j = jnp.arange(n)
    src = (j // 16) * 16 + (j % 16 + 8) % 16
    return w[..., src]


def _slot_cols(w, width):
    k, n = w.shape
    hs = n // width
    return jnp.pad(w.reshape(k, hs, width), ((0, 0), (0, 0), (0, LANE - width))).reshape(k, hs * LANE)


def _layer_weights(w_in, w_uq, w_ukv):
    o = 0
    parts = {}
    for name, wdt in (("da_q", DA_W), ("da_k", DA_W), ("da_v", DA_W), ("cq", ML_QR), ("ckv", ML_KVR), ("kr", ROT),
                      ("hq", HG_KW), ("hzf", HG_KW), ("hzb", HG_KW), ("hv", HG_VW), ("hgate", HG_VW)):
        parts[name] = w_in[:, o:o + wdt]
        o += wdt
    d = w_in.shape[0]
    w1 = jnp.concatenate([parts["da_q"], parts["da_k"], jnp.zeros((d, ML_NOPE), F32), parts["kr"],
                          jnp.zeros((d, LANE - ML_NOPE - ROT), F32)], axis=1)
    w2 = jnp.concatenate([parts["da_v"], parts["cq"], parts["ckv"]], axis=1)
    w3 = jnp.concatenate([parts[n] for n in ("hq", "hzf", "hzb", "hv", "hgate")], axis=1)
    uq = w_uq.reshape(ML_QR, ML_H, ML_NOPE + ROT)
    uq_sw = jnp.concatenate([jnp.zeros((ML_QR, ML_H, ML_NOPE), F32), _swap_cols(uq[..., ML_NOPE:])], axis=-1)
    pad_q = lambda a: jnp.pad(a, ((0, 0), (0, 0), (0, LANE - ML_NOPE - ROT))).reshape(ML_QR, ML_H * LANE)
    ukv = w_ukv.reshape(ML_KVR, ML_H, ML_NOPE + ML_V)
    wkn = _slot_cols(ukv[..., :ML_NOPE].reshape(ML_KVR, ML_H * ML_NOPE), ML_NOPE)
    wvm = ukv[..., ML_NOPE:].reshape(ML_KVR, ML_H * ML_V)
    bf = lambda a: a.astype(BF16)
    return bf(w1), bf(_swap_cols(w1)), bf(w2), bf(w3), bf(pad_q(uq)), bf(pad_q(uq_sw)), bf(wkn), bf(wvm)


def _rope_tables(n_lat, n_ctx):
    rows = n_lat // GRID_W
    row = jnp.repeat(jnp.arange(rows, dtype=F32), GRID_W)
    col = jnp.tile(jnp.arange(GRID_W, dtype=F32), rows)
    n_freq = ROT // 4
    inv = ROPE_THETA ** (-jnp.arange(n_freq, dtype=F32) / n_freq)
    ar, ac = row[:, None] * inv, col[:, None] * inv
    c32 = jnp.concatenate([jnp.cos(ar), jnp.cos(ar), jnp.cos(ac), jnp.cos(ac)], axis=1)
    s32 = jnp.concatenate([-jnp.sin(ar), jnp.sin(ar), -jnp.sin(ac), jnp.sin(ac)], axis=1)
    c32 = jnp.concatenate([c32, jnp.ones((n_ctx, ROT), F32)], axis=0)
    s32 = jnp.concatenate([s32, jnp.zeros((n_ctx, ROT), F32)], axis=0)
    t = n_lat + n_ctx
    one, zero = jnp.ones((t, ML_NOPE), F32), jnp.zeros((t, ML_NOPE), F32)
    pad = jnp.zeros((t, LANE - ML_NOPE - ROT), F32)
    return jnp.concatenate([jnp.tile(c32, (1, LANE // ROT)), jnp.tile(s32, (1, LANE // ROT)),
                            one, c32, pad, zero, s32, pad], axis=1)


def kernel(x, c, ctx, c_ctx, w_mod, b_mod, g_mix, g_mlp, w_in, w_out, da_lambda, da_subln_g, mla_g_cq, mla_g_ckv,
           mla_w_uq, mla_w_ukv, hg_lower_bounds, hg_norm_g, w_ff1, w_ff2, g_final):
    bsz, n_lat, d = x.shape
    n_ctx = ctx.shape[1]
    depth = w_in.shape[0]
    t = n_lat + n_ctx

    lb = jax.nn.softmax(hg_lower_bounds.astype(F32), axis=1)
    lb = jnp.cumsum(lb, axis=1) - lb[:, :1]
    cond = jnp.concatenate([c, c_ctx[None]], axis=0)
    mod = _modulation(cond, w_mod, b_mod).reshape(depth, bsz + 1, N_MOD, d)
    rope = _rope_tables(n_lat, n_ctx)

    grp = (jnp.arange(HG_VW)[:, None] // HG_V == jnp.arange(HG_VW)[None, :] // HG_V).astype(BF16)
    e_mat = (jnp.arange(HG_KW)[:, None] // HG_K == jnp.arange(HG_VW)[None, :] // HG_V).astype(BF16)

    tq_da = _pick(n_lat, TQ_DA)
    tq_ml = _pick(n_lat, TQ_ML)
    tk = _pick(t, TK)
    tc = _pick(n_ctx, (256, 128))

    xa = jnp.concatenate([x, ctx], axis=1)
    for l in range(depth):
        last = l == depth - 1
        lam_init = 0.8 - 0.6 * math.exp(-0.3 * l)
        lam = (jnp.exp(jnp.sum(da_lambda[l, 0] * da_lambda[l, 1]))
               - jnp.exp(jnp.sum(da_lambda[l, 2] * da_lambda[l, 3])) + lam_init).astype(F32).reshape(1, 1)
        modl, modc = mod[l, :bsz], mod[l, bsz]
        w1, w1s, w2, w3, wuq, wuqs, wkn, wvm = _layer_weights(w_in[l], mla_w_uq[l], mla_w_ukv[l])
        wts = (w1, w1s, w2, w3, mla_g_cq[l][None], mla_g_ckv[l][None], wuq, wuqs, wkn, wvm)
        qda, kda, vda, qm, km, vm, hg = _project(xa, modl, modc, g_mix[l][None], rope, wts, n_lat)

        g_da = jnp.broadcast_to(da_subln_g[l][:, None], (2 * DA_D, LANE))
        da_kw = dict(group=2, diff_scale=1.0 - lam_init)
        ml_kw = dict(group=1, diff_scale=1.0)
        oda = _flash(qda, kda, vda, lam, g_da, t_q=n_lat, q_off=0, t_kv=t, kv_off=0, tq=tq_da, tk=tk, **da_kw)
        oml = _flash(qm, km, vm, lam, g_da, t_q=n_lat, q_off=0, t_kv=t, kv_off=0, tq=tq_ml, tk=tk, **ml_kw)
        if not last:
            ctx_kw = dict(t_q=n_ctx, q_off=n_lat, t_kv=n_ctx, kv_off=n_lat, tq=tc, tk=tc)
            oda = jnp.concatenate([oda, _flash(qda, kda, vda, lam, g_da, **ctx_kw, **da_kw)], axis=1)
            oml = jnp.concatenate([oml, _flash(qm, km, vm, lam, g_da, **ctx_kw, **ml_kw)], axis=1)

        o_f, o_b = _hgrn(hg, jnp.stack([lb[0, l], lb[1, l]]), e_mat, n_lat)

        ghn = jnp.tile(hg_norm_g[l], HG_H)[None]
        bf = lambda a: a.astype(BF16)
        xa = _mix_mlp(xa, oda, oml, o_f, o_b, hg, modl, modc, ghn, g_mlp[l][None], g_final[None], grp,
                      bf(w_out[l, :DA_W]), bf(w_out[l, DA_W:DA_W + ML_W]), bf(w_out[l, DA_W + ML_W:]),
                      bf(w_ff1[l]), bf(w_ff2[l]), n_lat, last)
    return xa
```

```python
import functools
import math

import jax
import jax.numpy as jnp
from jax import lax
from jax.experimental import pallas as pl
from jax.experimental.pallas import tpu as pltpu

F32 = jnp.float32
BF16 = jnp.bfloat16

EPS = 1e-6
ROPE_THETA = 10000.0
GRID_W = 64
ROT = 32
DA_H, DA_D = 6, 32
DA_W = DA_H * 2 * DA_D
ML_H, ML_QR, ML_KVR, ML_NOPE, ML_V = 6, 256, 128, 64, 64
ML_W = ML_H * ML_V
HG_H, HG_K, HG_V = 4, 128, 64
HG_KW, HG_VW = HG_H * HG_K, HG_H * HG_V
FORGET_FLOOR = 1e-30
N_MOD = 6
LOG2E = 1.4426950408889634

HG_CHUNK = 64
HG_SUB = 8
LANE = 128
VMEM_LIMIT = 56 * 1024 * 1024
TQ_DA = (1024, 512, 256, 128)
TQ_ML = (2048, 1024, 512, 256, 128)
TK = (768, 512, 384, 256, 128)
VT_ROWS = 80
LOOKAHEAD = 6
FLASH_KEYS = 256
FLASH_ROWS = 256


def _cparams(sem):
    return pltpu.CompilerParams(dimension_semantics=sem, vmem_limit_bytes=VMEM_LIMIT)


def _const_spec(shape):
    nd = len(shape)
    return pl.BlockSpec(shape, lambda *_: (0,) * nd)


def _pick(n, cands):
    for c in cands:
        if n % c == 0:
            return c
    raise ValueError(f"no tile for {n} in {cands}")


def _mod_body(c_ref, w_ref, b_ref, o_ref):
    cc = c_ref[...]
    s = cc * jax.nn.sigmoid(cc)
    o_ref[0] = jnp.dot(s.astype(BF16), w_ref[0].astype(BF16), preferred_element_type=F32) + b_ref[0]


def _modulation(cond, w_mod, b_mod):
    depth, d, nm = w_mod.shape
    r = cond.shape[0]
    tn = _pick(nm, (1536, 1024, 512, 128))
    return pl.pallas_call(
        _mod_body,
        grid=(depth, nm // tn),
        in_specs=[pl.BlockSpec((r, d), lambda l, j: (0, 0)),
                  pl.BlockSpec((1, d, tn), lambda l, j: (l, 0, j)),
                  pl.BlockSpec((1, 1, tn), lambda l, j: (l, 0, j))],
        out_specs=pl.BlockSpec((1, r, tn), lambda l, j: (l, 0, j)),
        out_shape=jax.ShapeDtypeStruct((depth, r, nm), F32),
        compiler_params=_cparams(("arbitrary", "arbitrary")),
        name="modulation",
    )(cond, w_mod, b_mod.reshape(depth, 1, nm))


def _ada_rows(x, g, modl, modc, is_ctx, i_shift, i_scale):
    ms = jnp.mean(x * x, axis=-1, keepdims=True)
    y = x * lax.rsqrt(ms + EPS) * g
    shift = jnp.where(is_ctx, modc[i_shift:i_shift + 1], modl[i_shift:i_shift + 1])
    scale = jnp.where(is_ctx, modc[i_scale:i_scale + 1], modl[i_scale:i_scale + 1])
    return y * (1.0 + scale) + shift


def _rms(x, g):
    return x * lax.rsqrt(jnp.mean(x * x, axis=-1, keepdims=True) + EPS) * g


def _proj_body(n_lat, x_ref, modl_ref, modc_ref, g_ref, rope_ref, w1_ref, w1s_ref, w2_ref, w3_ref,
               gcq_ref, gckv_ref, wuq_ref, wuqs_ref, wkn_ref, wvm_ref,
               qda_ref, kda_ref, vda_ref, qm_ref, km_ref, vm_ref, hg_ref):
    tm = x_ref.shape[1]
    rows = pl.program_id(1) * tm + lax.broadcasted_iota(jnp.int32, (tm, 1), 0)
    is_ctx = rows >= n_lat
    h = _ada_rows(x_ref[0], g_ref[...], modl_ref[0], modc_ref[...], is_ctx, 0, 1).astype(BF16)

    rope = rope_ref[...]
    c4, s4, cq_t, sq_t = (rope[:, k * LANE:(k + 1) * LANE] for k in range(4))
    n1 = w1_ref.shape[1] // LANE
    p1 = jnp.dot(h, w1_ref[...], preferred_element_type=F32)
    p1s = jnp.dot(h, w1s_ref[...], preferred_element_type=F32)
    rot = p1 * jnp.concatenate([c4] * n1, axis=1) + p1s * jnp.concatenate([s4] * n1, axis=1)

    rq_t = (rot[:, :DA_W] * (DA_D ** -0.5 * LOG2E)).T
    row64 = lax.broadcasted_iota(jnp.int32, (2 * DA_D, 1), 0)
    for vh in range(2 * DA_H):
        hh, comp = vh // 2, vh % 2
        keep = (row64 >= DA_D) if comp else (row64 < DA_D)
        qda_ref[0, vh] = jnp.where(keep, rq_t[hh * 2 * DA_D:(hh + 1) * 2 * DA_D], 0.0).astype(BF16)
    for hh in range(DA_H):
        kda_ref[0, hh] = rot[:, DA_W + hh * 2 * DA_D:DA_W + (hh + 1) * 2 * DA_D].astype(BF16)
    kr_slot = rot[:, 2 * DA_W:2 * DA_W + LANE]

    ones_rows = (lax.broadcasted_iota(jnp.int32, (VT_ROWS - ML_V, tm), 0) == 0).astype(F32)
    p2 = jnp.dot(h, w2_ref[...], preferred_element_type=F32)
    v_t = p2[:, :DA_W].T
    for hh in range(DA_H):
        vda_ref[0, hh] = jnp.concatenate([v_t[hh * ML_V:(hh + 1) * ML_V], ones_rows], axis=0).astype(BF16)
    cqn = _rms(p2[:, DA_W:DA_W + ML_QR], gcq_ref[...]).astype(BF16)
    ckvn = _rms(p2[:, DA_W + ML_QR:DA_W + ML_QR + ML_KVR], gckv_ref[...]).astype(BF16)

    qu = jnp.dot(cqn, wuq_ref[...], preferred_element_type=F32)
    qus = jnp.dot(cqn, wuqs_ref[...], preferred_element_type=F32)
    qm = (qu * jnp.concatenate([cq_t] * ML_H, axis=1) + qus * jnp.concatenate([sq_t] * ML_H, axis=1))
    qm_ref[0] = (qm * ((ML_NOPE + ROT) ** -0.5 * LOG2E)).T.reshape(ML_H, LANE, tm).astype(BF16)
    kn = jnp.dot(ckvn, wkn_ref[...], preferred_element_type=F32)
    for hh in range(ML_H):
        km_ref[0, hh] = (kn[:, hh * LANE:(hh + 1) * LANE] + kr_slot).astype(BF16)
    vm_t = jnp.dot(ckvn, wvm_ref[...], preferred_element_type=F32).T
    for hh in range(ML_H):
        vm_ref[0, hh] = jnp.concatenate([vm_t[hh * ML_V:(hh + 1) * ML_V], ones_rows], axis=0).astype(BF16)

    hg_ref[0] = jnp.dot(h, w3_ref[...], preferred_element_type=F32)


def _project(xa, modl, modc, g, rope, wts, n_lat):
    b, t, d = xa.shape
    tm = _pick(t, (384, 256, 128))
    w1, w1s, w2, w3, gcq, gckv, wuq, wuqs, wkn, wvm = wts
    nhg = w3.shape[1]
    rows = lambda hs, w: (jax.ShapeDtypeStruct((b, hs, t, w), BF16),
                          pl.BlockSpec((1, hs, tm, w), lambda bi, i: (bi, 0, i, 0)))
    cols = lambda hs, r: (jax.ShapeDtypeStruct((b, hs, r, t), BF16),
                          pl.BlockSpec((1, hs, r, tm), lambda bi, i: (bi, 0, 0, i)))
    outs = (cols(2 * DA_H, 2 * DA_D), rows(DA_H, 2 * DA_D), cols(DA_H, VT_ROWS),
            cols(ML_H, LANE), rows(ML_H, LANE), cols(ML_H, VT_ROWS),
            (jax.ShapeDtypeStruct((b, t, nhg), F32), pl.BlockSpec((1, tm, nhg), lambda bi, i: (bi, i, 0))))
    out_shapes = tuple(o[0] for o in outs)
    out_specs = tuple(o[1] for o in outs)
    in_specs = [
        pl.BlockSpec((1, tm, d), lambda bi, i: (bi, i, 0)),
        pl.BlockSpec((1, N_MOD, d), lambda bi, i: (bi, 0, 0)),
        _const_spec(modc.shape), _const_spec(g.shape),
        pl.BlockSpec((tm, rope.shape[1]), lambda bi, i: (i, 0)),
    ] + [_const_spec(w.shape) for w in wts]
    return pl.pallas_call(
        functools.partial(_proj_body, n_lat),
        grid=(b, t // tm),
        in_specs=in_specs, out_specs=out_specs, out_shape=out_shapes,
        compiler_params=_cparams(("arbitrary", "arbitrary")),
        name="project",
    )(xa, modl, modc, g, rope, *wts)


def _flash_body(n_kv, group, diff_scale, rc, lam_ref, g_ref, qt_ref, k_ref, vt_ref, o_ref, m_ref, acc_ref):
    ki = pl.program_id(2)
    nk = pl.num_programs(2)
    tq = qt_ref.shape[3]

    @pl.when(ki == 0)
    def _():
        m_ref[...] = jnp.full(m_ref.shape, -jnp.inf, F32)
        acc_ref[...] = jnp.zeros(acc_ref.shape, F32)

    tk = k_ref.shape[2]
    ks = min(FLASH_KEYS, tk)
    chains = [(hh, c0, k0) for hh in range(n_kv) for c0 in range(0, group * tq, rc) for k0 in range(0, tk, ks)]

    def scores(hh, c0, k0):
        qt = qt_ref[0, hh * group + c0 // tq, :, c0 % tq:c0 % tq + rc]
        return jnp.dot(k_ref[0, hh, k0:k0 + ks, :], qt, preferred_element_type=F32)

    def softmax(hh, c0, k0, s):
        cols = slice(c0, c0 + rc)
        m_prev = m_ref[hh, :, cols]
        m_new = jnp.maximum(m_prev, jnp.max(s, axis=0, keepdims=True))
        m_ref[hh, :, cols] = m_new
        return jnp.exp2(s - m_new[0:1]).astype(BF16), jnp.exp2(m_prev - m_new)[0:1]

    def accumulate(hh, c0, k0, p, alpha):
        cols = slice(c0, c0 + rc)
        acc_ref[hh, :, cols] = alpha * acc_ref[hh, :, cols] + jnp.dot(
            vt_ref[0, hh, :, k0:k0 + ks], p, preferred_element_type=F32)

    n = len(chains)
    s_q = {i: scores(*chains[i]) for i in range(min(LOOKAHEAD, n))}
    p_q = {0: softmax(*chains[0], s_q.pop(0))}
    for ci in range(n):
        if ci + LOOKAHEAD < n:
            s_q[ci + LOOKAHEAD] = scores(*chains[ci + LOOKAHEAD])
        if ci + 1 < n:
            p_q[ci + 1] = softmax(*chains[ci + 1], s_q.pop(ci + 1))
        accumulate(*chains[ci], *p_q.pop(ci))

    @pl.when(ki == nk - 1)
    def _():
        outs = []
        for hh in range(n_kv):
            acc = acc_ref[hh]
            o = acc[:ML_V] / acc[ML_V:ML_V + 1]
            if group == 2:
                od = o[:, :tq] - lam_ref[0, 0] * o[:, tq:]
                ms = jnp.mean(od * od, axis=0, keepdims=True)
                o = od * lax.rsqrt(ms + EPS) * jnp.concatenate([g_ref[...]] * (tq // LANE), axis=1) * diff_scale
            outs.append(o.T)
        o_ref[0] = jnp.concatenate(outs, axis=-1).astype(o_ref.dtype)


def _flash(qt, k, vt, lam, g, *, group, diff_scale, t_q, q_off, t_kv, kv_off, tq, tk):
    b, hq, dk, _ = qt.shape
    n_kv = k.shape[1]
    assert hq == n_kv * group and q_off % tq == 0 and kv_off % tk == 0
    qo, ko = q_off // tq, kv_off // tk
    return pl.pallas_call(
        functools.partial(_flash_body, n_kv, group, diff_scale, min(FLASH_ROWS, tq)),
        grid=(b, t_q // tq, t_kv // tk),
        in_specs=[
            pl.BlockSpec(memory_space=pltpu.SMEM),
            _const_spec(g.shape),
            pl.BlockSpec((1, hq, dk, tq), lambda bi, qi, ki: (bi, 0, 0, qi + qo)),
            pl.BlockSpec((1, n_kv, tk, dk), lambda bi, qi, ki: (bi, 0, ki + ko, 0)),
            pl.BlockSpec((1, n_kv, VT_ROWS, tk), lambda bi, qi, ki: (bi, 0, 0, ki + ko)),
        ],
        out_specs=pl.BlockSpec((1, tq, n_kv * ML_V), lambda bi, qi, ki: (bi, qi, 0)),
        out_shape=jax.ShapeDtypeStruct((b, t_q, n_kv * ML_V), BF16),
        scratch_shapes=[pltpu.VMEM((n_kv, 8, group * tq), F32),
                        pltpu.VMEM((n_kv, VT_ROWS, group * tq), F32)],
        compiler_params=_cparams(("arbitrary", "arbitrary", "arbitrary")),
        name="flash_da" if group == 2 else "flash_mla",
    )(lam, g, qt, k, vt)


def _cumsum_rows(x):
    n, w = x.shape
    hi = x.astype(BF16)
    r1 = x - hi.astype(F32)
    mid = r1.astype(BF16)
    lo = (r1 - mid.astype(F32)).astype(BF16)
    tril = (lax.broadcasted_iota(jnp.int32, (n, n), 0) >= lax.broadcasted_iota(jnp.int32, (n, n), 1)).astype(BF16)
    parts = jnp.dot(tril, jnp.concatenate([hi, mid, lo], axis=1), preferred_element_type=F32)
    return parts[:, :w] + parts[:, w:2 * w] + parts[:, 2 * w:]


def _dot_nt(a, b):
    return lax.dot_general(a, b, (((1,), (1,)), ((), ())), preferred_element_type=F32)


def _pad_rows(x, start, total):
    parts = []
    if start:
        parts.append(jnp.zeros((start, x.shape[1]), x.dtype))
    parts.append(x)
    if total - start - x.shape[0]:
        parts.append(jnp.zeros((total - start - x.shape[0], x.shape[1]), x.dtype))
    return jnp.concatenate(parts, axis=0) if len(parts) > 1 else x


def _hgrn_body(qf_ref, zf_ref, vf_ref, qb_ref, zb_ref, vb_ref, lb_ref, e_ref, of_ref, ob_ref, stf_ref, stb_ref):
    @pl.when(pl.program_id(1) == 0)
    def _():
        stf_ref[...] = jnp.zeros(stf_ref.shape, F32)
        stb_ref[...] = jnp.zeros(stb_ref.shape, F32)

    c, nsub, kw = HG_CHUNK, HG_CHUNK // HG_SUB, HG_KW
    qs = (qf_ref[0], qb_ref[0])
    vs = (vf_ref[0], vb_ref[0])
    sts = (stf_ref[...], stb_ref[...])

    z = jnp.concatenate([zf_ref[0], zb_ref[0]], axis=1)
    lb = jnp.concatenate([lb_ref[0:1], lb_ref[1:2]], axis=1)
    f = lb + (1.0 - lb) * jax.nn.sigmoid(z)
    kk2 = (1.0 - lb) * jax.nn.sigmoid(-z)
    logf = jnp.log(jnp.maximum(f, FORGET_FLOOR))
    bf = _cumsum_rows(logf)
    kks = (kk2[:, :kw], kk2[:, kw:])
    bs = (bf[:, :kw] * LOG2E, (bf[c - 1:c, kw:] - bf[:, kw:] + logf[:, kw:]) * LOG2E)
    b_ends = (bs[0][c - 1:c], bs[1][0:1])

    def block_diag(st):
        st = st.astype(BF16)
        zero = jnp.zeros((HG_V, HG_K), BF16)
        return jnp.concatenate([jnp.concatenate(
            [st[:, hh * HG_K:(hh + 1) * HG_K] if j == hh else zero for j in range(HG_H)], axis=1)
            for hh in range(HG_H)], axis=0)

    outs = [_dot_nt((qs[d] * jnp.exp2(bs[d])).astype(BF16), block_diag(sts[d])) for d in range(2)]

    head_r = lax.broadcasted_iota(jnp.int32, (HG_H * c, 1), 0) // c
    same_blk = head_r == lax.broadcasted_iota(jnp.int32, (1, HG_H * c), 1) // c
    v_keep = head_r == lax.broadcasted_iota(jnp.int32, (1, HG_VW), 1) // HG_V
    a_mats = []
    for d in range(2):
        q, kk, b = qs[d], kks[d], bs[d]
        q_slots, k_slots = [], []
        for i in (range(nsub - 1) if d else range(1, nsub)):
            r0 = i * HG_SUB
            if d:
                m, k0, k1 = b[r0 + HG_SUB:r0 + HG_SUB + 1], r0 + HG_SUB, c
            else:
                m, k0, k1 = b[r0 - 1:r0], 0, r0
            q_slots.append(_pad_rows(q[r0:r0 + HG_SUB] * jnp.exp2(b[r0:r0 + HG_SUB] - m), r0, c).astype(BF16))
            k_slots.append(_pad_rows(kk[k0:k1] * jnp.exp2(m - b[k0:k1]), k0, c).astype(BF16))
        stack = lambda slots: jnp.concatenate(
            [jnp.concatenate([x[:, hh * HG_K:(hh + 1) * HG_K] for x in slots], axis=1) for hh in range(HG_H)], axis=0)
        a_mats.append(_dot_nt(stack(q_slots), stack(k_slots)))

    terms = []
    for d in range(2):
        q, kk, b = qs[d], kks[d], bs[d]
        for i in range(nsub):
            r0 = i * HG_SUB
            qb, bb, kb = q[r0:r0 + HG_SUB], b[r0:r0 + HG_SUB], kk[r0:r0 + HG_SUB]
            terms += [(qb * jnp.exp2(bb - bb[s:s + 1]) * kb[s:s + 1]).astype(BF16) for s in range(HG_SUB)]
    a_bc = jnp.dot(jnp.concatenate(terms, axis=0), e_ref[...], preferred_element_type=F32)

    for d in range(2):
        v_stack = jnp.where(v_keep, jnp.concatenate([vs[d]] * HG_H, axis=0), 0.0).astype(BF16)
        o_stack = jnp.dot(jnp.where(same_blk, a_mats[d], 0.0).astype(BF16), v_stack, preferred_element_type=F32)
        for hh in range(HG_H):
            outs[d] = outs[d] + o_stack[hh * c:(hh + 1) * c]

    pair = lax.broadcasted_iota(jnp.int32, (HG_SUB * HG_SUB, 1), 0)
    for d in range(2):
        keep = (pair % HG_SUB <= pair // HG_SUB) if d else (pair % HG_SUB >= pair // HG_SUB)
        keep = jnp.broadcast_to(keep, (HG_SUB * HG_SUB, HG_VW))
        diag = []
        for i in range(nsub):
            r0 = i * HG_SUB
            base = (d * nsub + i) * HG_SUB * HG_SUB
            blk = jnp.where(keep, a_bc[base:base + HG_SUB * HG_SUB], 0.0)
            vb = vs[d][r0:r0 + HG_SUB]
            od = blk[0:HG_SUB] * vb[0:1]
            for s in range(1, HG_SUB):
                od = od + blk[s * HG_SUB:(s + 1) * HG_SUB] * vb[s:s + 1]
            diag.append(od)
        outs[d] = outs[d] + jnp.concatenate(diag, axis=0)

    of_ref[0] = outs[0]
    ob_ref[0] = outs[1]

    for d, st_ref in enumerate((stf_ref, stb_ref)):
        kd = kks[d] * jnp.exp2(b_ends[d] - bs[d])
        upd = jnp.dot(vs[d].T.astype(BF16), kd.astype(BF16), preferred_element_type=F32)
        upd = jnp.concatenate([upd[hh * HG_V:(hh + 1) * HG_V, hh * HG_K:(hh + 1) * HG_K] for hh in range(HG_H)], axis=1)
        st_ref[...] = sts[d] * jnp.exp2(b_ends[d]) + upd


def _hgrn(hg, lb2, e_mat, n_lat):
    b, t, _ = hg.shape
    c = HG_CHUNK
    nc, nlc = t // c, n_lat // c
    fwd = lambda i: (i + nlc) % nc
    bwd = lambda i: nc - 1 - i
    v_blk = 3 * HG_KW // HG_VW
    return pl.pallas_call(
        _hgrn_body,
        grid=(b, nc),
        in_specs=[
            pl.BlockSpec((1, c, HG_KW), lambda bi, i: (bi, fwd(i), 0)),
            pl.BlockSpec((1, c, HG_KW), lambda bi, i: (bi, fwd(i), 1)),
            pl.BlockSpec((1, c, HG_VW), lambda bi, i: (bi, fwd(i), v_blk)),
            pl.BlockSpec((1, c, HG_KW), lambda bi, i: (bi, bwd(i), 0)),
            pl.BlockSpec((1, c, HG_KW), lambda bi, i: (bi, bwd(i), 2)),
            pl.BlockSpec((1, c, HG_VW), lambda bi, i: (bi, bwd(i), v_blk)),
            _const_spec(lb2.shape), _const_spec(e_mat.shape),
        ],
        out_specs=(pl.BlockSpec((1, c, HG_VW), lambda bi, i: (bi, fwd(i), 0)),
                   pl.BlockSpec((1, c, HG_VW), lambda bi, i: (bi, bwd(i), 0))),
        out_shape=(jax.ShapeDtypeStruct((b, t, HG_VW), F32), jax.ShapeDtypeStruct((b, t, HG_VW), F32)),
        scratch_shapes=[pltpu.VMEM((HG_V, HG_KW), F32), pltpu.VMEM((HG_V, HG_KW), F32)],
        compiler_params=_cparams(("arbitrary", "arbitrary")),
        name="hgrn2",
    )(hg, hg, hg, hg, hg, hg, lb2, e_mat)


def _mix_mlp_body(n_lat, ff_chunk, final, x_ref, oda_ref, oml_ref, of_ref, ob_ref, gate_ref, modl_ref, modc_ref,
                  ghn_ref, gmlp_ref, gfin_ref, grp_ref, wda_ref, wml_ref, whg_ref, w1_ref, w2_ref, o_ref):
    tm = x_ref.shape[1]
    rows = pl.program_id(1) * tm + lax.broadcasted_iota(jnp.int32, (tm, 1), 0)
    is_ctx = rows >= n_lat
    modl, modc = modl_ref[0], modc_ref[...]

    oh = of_ref[0] + ob_ref[0]
    sq = oh * oh
    sq_hi = sq.astype(BF16)
    sq_lo = (sq - sq_hi.astype(F32)).astype(BF16)
    ms = (jnp.dot(sq_hi, grp_ref[...], preferred_element_type=F32)
          + jnp.dot(sq_lo, grp_ref[...], preferred_element_type=F32)) * (1.0 / HG_V)
    gt = gate_ref[0]
    yh = oh * lax.rsqrt(ms + EPS) * ghn_ref[...] * (gt * jax.nn.sigmoid(gt))

    mix = (jnp.dot(oda_ref[0], wda_ref[...], preferred_element_type=F32)
           + jnp.dot(oml_ref[0], wml_ref[...], preferred_element_type=F32)
           + jnp.dot(yh.astype(BF16), whg_ref[...], preferred_element_type=F32))
    g_mix = jnp.where(is_ctx, modc[2:3], modl[2:3])
    x1 = x_ref[0] + g_mix * mix

    hn = _ada_rows(x1, gmlp_ref[...], modl, modc, is_ctx, 3, 4).astype(BF16)
    dff = w1_ref.shape[1]
    acc = jnp.zeros(x1.shape, F32)
    for j in range(dff // ff_chunk):
        u = jnp.dot(hn, w1_ref[:, j * ff_chunk:(j + 1) * ff_chunk], preferred_element_type=F32)
        u = jnp.maximum(u, 0.0)
        acc = acc + jnp.dot((u * u).astype(BF16), w2_ref[j * ff_chunk:(j + 1) * ff_chunk, :],
                            preferred_element_type=F32)
    g_mlp = jnp.where(is_ctx, modc[5:6], modl[5:6])
    x2 = x1 + g_mlp * acc
    if final:
        x2 = _rms(x2, gfin_ref[...])
    o_ref[0] = x2


def _mix_mlp(xa, oda, oml, o_f, o_b, hg, modl, modc, ghn, gmlp, gfin, grp, wda, wml, whg, w1, w2, n_lat, final):
    b, t, d = xa.shape
    t_out = n_lat if final else t
    tm = _pick(t_out, (512, 384, 256, 128)) if final else _pick(t, (384, 256, 128))
    gate_blk = (3 * HG_KW + HG_VW) // HG_VW
    row = lambda w: pl.BlockSpec((1, tm, w), lambda bi, i: (bi, i, 0))
    consts = (modc, ghn, gmlp, gfin, grp, wda, wml, whg, w1, w2)
    return pl.pallas_call(
        functools.partial(_mix_mlp_body, n_lat, 1024, final),
        grid=(b, t_out // tm),
        in_specs=[row(d), row(DA_W), row(ML_W), row(HG_VW), row(HG_VW),
                  pl.BlockSpec((1, tm, HG_VW), lambda bi, i: (bi, i, gate_blk)),
                  pl.BlockSpec((1, N_MOD, d), lambda bi, i: (bi, 0, 0))]
                 + [_const_spec(a.shape) for a in consts],
        out_specs=row(d),
        out_shape=jax.ShapeDtypeStruct((b, t_out, d), F32),
        compiler_params=_cparams(("arbitrary", "arbitrary")),
        name="mix_mlp",
    )(xa, oda, oml, o_f, o_b, hg, modl, *consts)


def _swap_cols(w):
    n = w.shape[-1]
    j = jnp.arange(n)
    src = (j // 16) * 16 + (j % 16 + 8) % 16
    return w[..., src]


def _slot_cols(w, width):
    k, n = w.shape
    hs = n // width
    return jnp.pad(w.reshape(k, hs, width), ((0, 0), (0, 0), (0, LANE - width))).reshape(k, hs * LANE)


def _layer_weights(w_in, w_uq, w_ukv):
    o = 0
    parts = {}
    for name, wdt in (("da_q", DA_W), ("da_k", DA_W), ("da_v", DA_W), ("cq", ML_QR), ("ckv", ML_KVR), ("kr", ROT),
                      ("hq", HG_KW), ("hzf", HG_KW), ("hzb", HG_KW), ("hv", HG_VW), ("hgate", HG_VW)):
        parts[name] = w_in[:, o:o + wdt]
        o += wdt
    d = w_in.shape[0]
    w1 = jnp.concatenate([parts["da_q"], parts["da_k"], jnp.zeros((d, ML_NOPE), F32), parts["kr"],
                          jnp.zeros((d, LANE - ML_NOPE - ROT), F32)], axis=1)
    w2 = jnp.concatenate([parts["da_v"], parts["cq"], parts["ckv"]], axis=1)
    w3 = jnp.concatenate([parts[n] for n in ("hq", "hzf", "hzb", "hv", "hgate")], axis=1)
    uq = w_uq.reshape(ML_QR, ML_H, ML_NOPE + ROT)
    uq_sw = jnp.concatenate([jnp.zeros((ML_QR, ML_H, ML_NOPE), F32), _swap_cols(uq[..., ML_NOPE:])], axis=-1)
    pad_q = lambda a: jnp.pad(a, ((0, 0), (0, 0), (0, LANE - ML_NOPE - ROT))).reshape(ML_QR, ML_H * LANE)
    ukv = w_ukv.reshape(ML_KVR, ML_H, ML_NOPE + ML_V)
    wkn = _slot_cols(ukv[..., :ML_NOPE].reshape(ML_KVR, ML_H * ML_NOPE), ML_NOPE)
    wvm = ukv[..., ML_NOPE:].reshape(ML_KVR, ML_H * ML_V)
    bf = lambda a: a.astype(BF16)
    return bf(w1), bf(_swap_cols(w1)), bf(w2), bf(w3), bf(pad_q(uq)), bf(pad_q(uq_sw)), bf(wkn), bf(wvm)


def _rope_tables(n_lat, n_ctx):
    rows = n_lat // GRID_W
    row = jnp.repeat(jnp.arange(rows, dtype=F32), GRID_W)
    col = jnp.tile(jnp.arange(GRID_W, dtype=F32), rows)
    n_freq = ROT // 4
    inv = ROPE_THETA ** (-jnp.arange(n_freq, dtype=F32) / n_freq)
    ar, ac = row[:, None] * inv, col[:, None] * inv
    c32 = jnp.concatenate([jnp.cos(ar), jnp.cos(ar), jnp.cos(ac), jnp.cos(ac)], axis=1)
    s32 = jnp.concatenate([-jnp.sin(ar), jnp.sin(ar), -jnp.sin(ac), jnp.sin(ac)], axis=1)
    c32 = jnp.concatenate([c32, jnp.ones((n_ctx, ROT), F32)], axis=0)
    s32 = jnp.concatenate([s32, jnp.zeros((n_ctx, ROT), F32)], axis=0)
    t = n_lat + n_ctx
    one, zero = jnp.ones((t, ML_NOPE), F32), jnp.zeros((t, ML_NOPE), F32)
    pad = jnp.zeros((t, LANE - ML_NOPE - ROT), F32)
    return jnp.concatenate([jnp.tile(c32, (1, LANE // ROT)), jnp.tile(s32, (1, LANE // ROT)),
                            one, c32, pad, zero, s32, pad], axis=1)


def kernel(x, c, ctx, c_ctx, w_mod, b_mod, g_mix, g_mlp, w_in, w_out, da_lambda, da_subln_g, mla_g_cq, mla_g_ckv,
           mla_w_uq, mla_w_ukv, hg_lower_bounds, hg_norm_g, w_ff1, w_ff2, g_final):
    bsz, n_lat, d = x.shape
    n_ctx = ctx.shape[1]
    depth = w_in.shape[0]
    t = n_lat + n_ctx

    lb = jax.nn.softmax(hg_lower_bounds.astype(F32), axis=1)
    lb = jnp.cumsum(lb, axis=1) - lb[:, :1]
    cond = jnp.concatenate([c, c_ctx[None]], axis=0)
    mod = _modulation(cond, w_mod, b_mod).reshape(depth, bsz + 1, N_MOD, d)
    rope = _rope_tables(n_lat, n_ctx)

    grp = (jnp.arange(HG_VW)[:, None] // HG_V == jnp.arange(HG_VW)[None, :] // HG_V).astype(BF16)
    e_mat = (jnp.arange(HG_KW)[:, None] // HG_K == jnp.arange(HG_VW)[None, :] // HG_V).astype(BF16)

    tq_da = _pick(n_lat, TQ_DA)
    tq_ml = _pick(n_lat, TQ_ML)
    tk = _pick(t, TK)
    tc = _pick(n_ctx, (256, 128))

    xa = jnp.concatenate([x, ctx], axis=1)
    for l in range(depth):
        last = l == depth - 1
        lam_init = 0.8 - 0.6 * math.exp(-0.3 * l)
        lam = (jnp.exp(jnp.sum(da_lambda[l, 0] * da_lambda[l, 1]))
               - jnp.exp(jnp.sum(da_lambda[l, 2] * da_lambda[l, 3])) + lam_init).astype(F32).reshape(1, 1)
        modl, modc = mod[l, :bsz], mod[l, bsz]
        w1, w1s, w2, w3, wuq, wuqs, wkn, wvm = _layer_weights(w_in[l], mla_w_uq[l], mla_w_ukv[l])
        wts = (w1, w1s, w2, w3, mla_g_cq[l][None], mla_g_ckv[l][None], wuq, wuqs, wkn, wvm)
        qda, kda, vda, qm, km, vm, hg = _project(xa, modl, modc, g_mix[l][None], rope, wts, n_lat)

        g_da = jnp.broadcast_to(da_subln_g[l][:, None], (2 * DA_D, LANE))
        da_kw = dict(group=2, diff_scale=1.0 - lam_init)
        ml_kw = dict(group=1, diff_scale=1.0)
        oda = _flash(qda, kda, vda, lam, g_da, t_q=n_lat, q_off=0, t_kv=t, kv_off=0, tq=tq_da, tk=tk, **da_kw)
        oml = _flash(qm, km, vm, lam, g_da, t_q=n_lat, q_off=0, t_kv=t, kv_off=0, tq=tq_ml, tk=tk, **ml_kw)
        if not last:
            ctx_kw = dict(t_q=n_ctx, q_off=n_lat, t_kv=n_ctx, kv_off=n_lat, tq=tc, tk=tc)
            oda = jnp.concatenate([oda, _flash(qda, kda, vda, lam, g_da, **ctx_kw, **da_kw)], axis=1)
            oml = jnp.concatenate([oml, _flash(qm, km, vm, lam, g_da, **ctx_kw, **ml_kw)], axis=1)

        o_f, o_b = _hgrn(hg, jnp.stack([lb[0, l], lb[1, l]]), e_mat, n_lat)

        ghn = jnp.tile(hg_norm_g[l], HG_H)[None]
        bf = lambda a: a.astype(BF16)
        xa = _mix_mlp(xa, oda, oml, o_f, o_b, hg, modl, modc, ghn, g_mlp[l][None], g_final[None], grp,
                      bf(w_out[l, :DA_W]), bf(w_out[l, DA_W:DA_W + ML_W]), bf(w_out[l, DA_W + ML_W:]),
                      bf(w_ff1[l]), bf(w_ff2[l]), n_lat, last)
    return xa
```

```python
import functools
import math

import jax
import jax.numpy as jnp
from jax import lax
from jax.experimental import pallas as pl
from jax.experimental.pallas import tpu as pltpu

F32 = jnp.float32
BF16 = jnp.bfloat16

EPS = 1e-6
ROPE_THETA = 10000.0
GRID_W = 64
ROT = 32
DA_H, DA_D = 6, 32
DA_W = DA_H * 2 * DA_D
ML_H, ML_QR, ML_KVR, ML_NOPE, ML_V = 6, 256, 128, 64, 64
ML_W = ML_H * ML_V
HG_H, HG_K, HG_V = 4, 128, 64
HG_KW, HG_VW = HG_H * HG_K, HG_H * HG_V
FORGET_FLOOR = 1e-30
N_MOD = 6
LOG2E = 1.4426950408889634

HG_CHUNK = 64
HG_BATCH = (4, 2, 1)
HG_SUB = 8
LANE = 128
VMEM_LIMIT = 56 * 1024 * 1024
TQ_DA = (1024, 512, 256, 128)
TQ_ML = (2048, 1024, 512, 256, 128)
TK = (768, 512, 384, 256, 128)
VT_ROWS = 80
LOOKAHEAD = 6
FLASH_KEYS = 256
FLASH_ROWS = 256


def _cparams(sem):
    return pltpu.CompilerParams(dimension_semantics=sem, vmem_limit_bytes=VMEM_LIMIT)


def _const_spec(shape):
    nd = len(shape)
    return pl.BlockSpec(shape, lambda *_: (0,) * nd)


def _pick(n, cands):
    for c in cands:
        if n % c == 0:
            return c
    raise ValueError(f"no tile for {n} in {cands}")


def _mod_body(c_ref, w_ref, b_ref, o_ref):
    cc = c_ref[...]
    s = cc * jax.nn.sigmoid(cc)
    o_ref[0] = jnp.dot(s.astype(BF16), w_ref[0].astype(BF16), preferred_element_type=F32) + b_ref[0]


def _modulation(cond, w_mod, b_mod):
    depth, d, nm = w_mod.shape
    r = cond.shape[0]
    tn = _pick(nm, (1536, 1024, 512, 128))
    return pl.pallas_call(
        _mod_body,
        grid=(depth, nm // tn),
        in_specs=[pl.BlockSpec((r, d), lambda l, j: (0, 0)),
                  pl.BlockSpec((1, d, tn), lambda l, j: (l, 0, j)),
                  pl.BlockSpec((1, 1, tn), lambda l, j: (l, 0, j))],
        out_specs=pl.BlockSpec((1, r, tn), lambda l, j: (l, 0, j)),
        out_shape=jax.ShapeDtypeStruct((depth, r, nm), F32),
        compiler_params=_cparams(("arbitrary", "arbitrary")),
        name="modulation",
    )(cond, w_mod, b_mod.reshape(depth, 1, nm))


def _ada_rows(x, g, modl, modc, is_ctx, i_shift, i_scale):
    ms = jnp.mean(x * x, axis=-1, keepdims=True)
    y = x * lax.rsqrt(ms + EPS) * g
    shift = jnp.where(is_ctx, modc[i_shift:i_shift + 1], modl[i_shift:i_shift + 1])
    scale = jnp.where(is_ctx, modc[i_scale:i_scale + 1], modl[i_scale:i_scale + 1])
    return y * (1.0 + scale) + shift


def _rms(x, g):
    return x * lax.rsqrt(jnp.mean(x * x, axis=-1, keepdims=True) + EPS) * g


def _proj_body(n_lat, x_ref, modl_ref, modc_ref, g_ref, rope_ref, w1_ref, w1s_ref, w2_ref, w3_ref,
               gcq_ref, gckv_ref, wuq_ref, wuqs_ref, wkn_ref, wvm_ref,
               qda_ref, kda_ref, vda_ref, qm_ref, km_ref, vm_ref, hg_ref):
    tm = x_ref.shape[1]
    rows = pl.program_id(1) * tm + lax.broadcasted_iota(jnp.int32, (tm, 1), 0)
    is_ctx = rows >= n_lat
    h = _ada_rows(x_ref[0], g_ref[...], modl_ref[0], modc_ref[...], is_ctx, 0, 1).astype(BF16)

    rope = rope_ref[...]
    c4, s4, cq_t, sq_t = (rope[:, k * LANE:(k + 1) * LANE] for k in range(4))
    n1 = w1_ref.shape[1] // LANE
    p1 = jnp.dot(h, w1_ref[...], preferred_element_type=F32)
    p1s = jnp.dot(h, w1s_ref[...], preferred_element_type=F32)
    rot = p1 * jnp.concatenate([c4] * n1, axis=1) + p1s * jnp.concatenate([s4] * n1, axis=1)

    rq_t = (rot[:, :DA_W] * (DA_D ** -0.5 * LOG2E)).T
    row64 = lax.broadcasted_iota(jnp.int32, (2 * DA_D, 1), 0)
    for vh in range(2 * DA_H):
        hh, comp = vh // 2, vh % 2
        keep = (row64 >= DA_D) if comp else (row64 < DA_D)
        qda_ref[0, vh] = jnp.where(keep, rq_t[hh * 2 * DA_D:(hh + 1) * 2 * DA_D], 0.0).astype(BF16)
    for hh in range(DA_H):
        kda_ref[0, hh] = rot[:, DA_W + hh * 2 * DA_D:DA_W + (hh + 1) * 2 * DA_D].astype(BF16)
    kr_slot = rot[:, 2 * DA_W:2 * DA_W + LANE]

    ones_rows = (lax.broadcasted_iota(jnp.int32, (VT_ROWS - ML_V, tm), 0) == 0).astype(F32)
    p2 = jnp.dot(h, w2_ref[...], preferred_element_type=F32)
    v_t = p2[:, :DA_W].T
    for hh in range(DA_H):
        vda_ref[0, hh] = jnp.concatenate([v_t[hh * ML_V:(hh + 1) * ML_V], ones_rows], axis=0).astype(BF16)
    cqn = _rms(p2[:, DA_W:DA_W + ML_QR], gcq_ref[...]).astype(BF16)
    ckvn = _rms(p2[:, DA_W + ML_QR:DA_W + ML_QR + ML_KVR], gckv_ref[...]).astype(BF16)

    qu = jnp.dot(cqn, wuq_ref[...], preferred_element_type=F32)
    qus = jnp.dot(cqn, wuqs_ref[...], preferred_element_type=F32)
    qm = (qu * jnp.concatenate([cq_t] * ML_H, axis=1) + qus * jnp.concatenate([sq_t] * ML_H, axis=1))
    qm_ref[0] = (qm * ((ML_NOPE + ROT) ** -0.5 * LOG2E)).T.reshape(ML_H, LANE, tm).astype(BF16)
    kn = jnp.dot(ckvn, wkn_ref[...], preferred_element_type=F32)
    for hh in range(ML_H):
        km_ref[0, hh] = (kn[:, hh * LANE:(hh + 1) * LANE] + kr_slot).astype(BF16)
    vm_t = jnp.dot(ckvn, wvm_ref[...], preferred_element_type=F32).T
    for hh in range(ML_H):
        vm_ref[0, hh] = jnp.concatenate([vm_t[hh * ML_V:(hh + 1) * ML_V], ones_rows], axis=0).astype(BF16)

    hg_ref[0] = jnp.dot(h, w3_ref[...], preferred_element_type=F32)


def _project(xa, modl, modc, g, rope, wts, n_lat):
    b, t, d = xa.shape
    tm = _pick(t, (384, 256, 128))
    w1, w1s, w2, w3, gcq, gckv, wuq, wuqs, wkn, wvm = wts
    nhg = w3.shape[1]
    rows = lambda hs, w: (jax.ShapeDtypeStruct((b, hs, t, w), BF16),
                          pl.BlockSpec((1, hs, tm, w), lambda bi, i: (bi, 0, i, 0)))
    cols = lambda hs, r: (jax.ShapeDtypeStruct((b, hs, r, t), BF16),
                          pl.BlockSpec((1, hs, r, tm), lambda bi, i: (bi, 0, 0, i)))
    outs = (cols(2 * DA_H, 2 * DA_D), rows(DA_H, 2 * DA_D), cols(DA_H, VT_ROWS),
            cols(ML_H, LANE), rows(ML_H, LANE), cols(ML_H, VT_ROWS),
            (jax.ShapeDtypeStruct((b, t, nhg), F32), pl.BlockSpec((1, tm, nhg), lambda bi, i: (bi, i, 0))))
    out_shapes = tuple(o[0] for o in outs)
    out_specs = tuple(o[1] for o in outs)
    in_specs = [
        pl.BlockSpec((1, tm, d), lambda bi, i: (bi, i, 0)),
        pl.BlockSpec((1, N_MOD, d), lambda bi, i: (bi, 0, 0)),
        _const_spec(modc.shape), _const_spec(g.shape),
        pl.BlockSpec((tm, rope.shape[1]), lambda bi, i: (i, 0)),
    ] + [_const_spec(w.shape) for w in wts]
    return pl.pallas_call(
        functools.partial(_proj_body, n_lat),
        grid=(b, t // tm),
        in_specs=in_specs, out_specs=out_specs, out_shape=out_shapes,
        compiler_params=_cparams(("arbitrary", "arbitrary")),
        name="project",
    )(xa, modl, modc, g, rope, *wts)


def _flash_body(n_kv, group, diff_scale, rc, lam_ref, g_ref, qt_ref, k_ref, vt_ref, o_ref, m_ref, acc_ref):
    ki = pl.program_id(2)
    nk = pl.num_programs(2)
    tq = qt_ref.shape[3]

    @pl.when(ki == 0)
    def _():
        m_ref[...] = jnp.full(m_ref.shape, -jnp.inf, F32)
        acc_ref[...] = jnp.zeros(acc_ref.shape, F32)

    tk = k_ref.shape[2]
    ks = min(FLASH_KEYS, tk)
    chains = [(hh, c0, k0) for hh in range(n_kv) for c0 in range(0, group * tq, rc) for k0 in range(0, tk, ks)]

    def scores(hh, c0, k0):
        qt = qt_ref[0, hh * group + c0 // tq, :, c0 % tq:c0 % tq + rc]
        return jnp.dot(k_ref[0, hh, k0:k0 + ks, :], qt, preferred_element_type=F32)

    def softmax(hh, c0, k0, s):
        cols = slice(c0, c0 + rc)
        m_prev = m_ref[hh, :, cols]
        m_new = jnp.maximum(m_prev, jnp.max(s, axis=0, keepdims=True))
        m_ref[hh, :, cols] = m_new
        return jnp.exp2(s - m_new[0:1]).astype(BF16), jnp.exp2(m_prev - m_new)[0:1]

    def accumulate(hh, c0, k0, p, alpha):
        cols = slice(c0, c0 + rc)
        acc_ref[hh, :, cols] = alpha * acc_ref[hh, :, cols] + jnp.dot(
            vt_ref[0, hh, :, k0:k0 + ks], p, preferred_element_type=F32)

    n = len(chains)
    s_q = {i: scores(*chains[i]) for i in range(min(LOOKAHEAD, n))}
    p_q = {0: softmax(*chains[0], s_q.pop(0))}
    for ci in range(n):
        if ci + LOOKAHEAD < n:
            s_q[ci + LOOKAHEAD] = scores(*chains[ci + LOOKAHEAD])
        if ci + 1 < n:
            p_q[ci + 1] = softmax(*chains[ci + 1], s_q.pop(ci + 1))
        accumulate(*chains[ci], *p_q.pop(ci))

    @pl.when(ki == nk - 1)
    def _():
        outs = []
        for hh in range(n_kv):
            acc = acc_ref[hh]
            o = acc[:ML_V] / acc[ML_V:ML_V + 1]
            if group == 2:
                od = o[:, :tq] - lam_ref[0, 0] * o[:, tq:]
                ms = jnp.mean(od * od, axis=0, keepdims=True)
                o = od * lax.rsqrt(ms + EPS) * jnp.concatenate([g_ref[...]] * (tq // LANE), axis=1) * diff_scale
            outs.append(o.T)
        o_ref[0] = jnp.concatenate(outs, axis=-1).astype(o_ref.dtype)


def _flash(qt, k, vt, lam, g, *, group, diff_scale, t_q, q_off, t_kv, kv_off, tq, tk):
    b, hq, dk, _ = qt.shape
    n_kv = k.shape[1]
    assert hq == n_kv * group and q_off % tq == 0 and kv_off % tk == 0
    qo, ko = q_off // tq, kv_off // tk
    return pl.pallas_call(
        functools.partial(_flash_body, n_kv, group, diff_scale, min(FLASH_ROWS, tq)),
        grid=(b, t_q // tq, t_kv // tk),
        in_specs=[
            pl.BlockSpec(memory_space=pltpu.SMEM),
            _const_spec(g.shape),
            pl.BlockSpec((1, hq, dk, tq), lambda bi, qi, ki: (bi, 0, 0, qi + qo)),
            pl.BlockSpec((1, n_kv, tk, dk), lambda bi, qi, ki: (bi, 0, ki + ko, 0)),
            pl.BlockSpec((1, n_kv, VT_ROWS, tk), lambda bi, qi, ki: (bi, 0, 0, ki + ko)),
        ],
        out_specs=pl.BlockSpec((1, tq, n_kv * ML_V), lambda bi, qi, ki: (bi, qi, 0)),
        out_shape=jax.ShapeDtypeStruct((b, t_q, n_kv * ML_V), BF16),
        scratch_shapes=[pltpu.VMEM((n_kv, 8, group * tq), F32),
                        pltpu.VMEM((n_kv, VT_ROWS, group * tq), F32)],
        compiler_params=_cparams(("arbitrary", "arbitrary", "arbitrary")),
        name="flash_da" if group == 2 else "flash_mla",
    )(lam, g, qt, k, vt)


def _cumsum_rows(x):
    n, w = x.shape
    hi = x.astype(BF16)
    r1 = x - hi.astype(F32)
    mid = r1.astype(BF16)
    lo = (r1 - mid.astype(F32)).astype(BF16)
    tril = (lax.broadcasted_iota(jnp.int32, (n, n), 0) >= lax.broadcasted_iota(jnp.int32, (n, n), 1)).astype(BF16)
    parts = jnp.dot(tril, jnp.concatenate([hi, mid, lo], axis=1), preferred_element_type=F32)
    return parts[:, :w] + parts[:, w:2 * w] + parts[:, 2 * w:]


def _dot_nt(a, b):
    return lax.dot_general(a, b, (((1,), (1,)), ((), ())), preferred_element_type=F32)


def _pad_rows(x, start, total):
    parts = []
    if start:
        parts.append(jnp.zeros((start, x.shape[1]), x.dtype))
    parts.append(x)
    if total - start - x.shape[0]:
        parts.append(jnp.zeros((total - start - x.shape[0], x.shape[1]), x.dtype))
    return jnp.concatenate(parts, axis=0) if len(parts) > 1 else x


def _hgrn_body(nb, qf_ref, zf_ref, vf_ref, qb_ref, zb_ref, vb_ref, lb_ref, e_ref, of_ref, ob_ref, stf_ref, stb_ref):
    @pl.when(pl.program_id(1) == 0)
    def _():
        stf_ref[...] = jnp.zeros(stf_ref.shape, F32)
        stb_ref[...] = jnp.zeros(stb_ref.shape, F32)

    c, nsub, kw = HG_CHUNK, HG_CHUNK // HG_SUB, HG_KW
    streams = [(bi, d) for bi in range(nb) for d in range(2)]
    n = len(streams)
    qs = [(qb_ref if d else qf_ref)[bi] for bi, d in streams]
    vs = [(vb_ref if d else vf_ref)[bi] for bi, d in streams]
    st_refs = [(stb_ref if d else stf_ref) for bi, d in streams]
    sts = [st_refs[i][bi] for i, (bi, d) in enumerate(streams)]

    z = jnp.concatenate([(zb_ref if d else zf_ref)[bi] for bi, d in streams], axis=1)
    lb = jnp.concatenate([lb_ref[d:d + 1] for bi, d in streams], axis=1)
    f = lb + (1.0 - lb) * jax.nn.sigmoid(z)
    kk_all = (1.0 - lb) * jax.nn.sigmoid(-z)
    logf = jnp.log(jnp.maximum(f, FORGET_FLOOR))
    bf = _cumsum_rows(logf)
    kks, bs, b_ends = [], [], []
    for i, (bi, d) in enumerate(streams):
        sl = slice(i * kw, (i + 1) * kw)
        kks.append(kk_all[:, sl])
        b = (bf[c - 1:c, sl] - bf[:, sl] + logf[:, sl]) * LOG2E if d else bf[:, sl] * LOG2E
        bs.append(b)
        b_ends.append(b[0:1] if d else b[c - 1:c])

    def block_diag(st):
        st = st.astype(BF16)
        zero = jnp.zeros((HG_V, HG_K), BF16)
        return jnp.concatenate([jnp.concatenate(
            [st[:, hh * HG_K:(hh + 1) * HG_K] if j == hh else zero for j in range(HG_H)], axis=1)
            for hh in range(HG_H)], axis=0)

    outs = [_dot_nt((qs[i] * jnp.exp2(bs[i])).astype(BF16), block_diag(sts[i])) for i in range(n)]

    head_r = lax.broadcasted_iota(jnp.int32, (HG_H * c, 1), 0) // c
    same_blk = head_r == lax.broadcasted_iota(jnp.int32, (1, HG_H * c), 1) // c
    v_keep = head_r == lax.broadcasted_iota(jnp.int32, (1, HG_VW), 1) // HG_V
    a_mats = []
    for i, (bi, d) in enumerate(streams):
        q, kk, b = qs[i], kks[i], bs[i]
        q_slots, k_slots = [], []
        for j in (range(nsub - 1) if d else range(1, nsub)):
            r0 = j * HG_SUB
            if d:
                m, k0, k1 = b[r0 + HG_SUB:r0 + HG_SUB + 1], r0 + HG_SUB, c
            else:
                m, k0, k1 = b[r0 - 1:r0], 0, r0
            q_slots.append(_pad_rows(q[r0:r0 + HG_SUB] * jnp.exp2(b[r0:r0 + HG_SUB] - m), r0, c).astype(BF16))
            k_slots.append(_pad_rows(kk[k0:k1] * jnp.exp2(m - b[k0:k1]), k0, c).astype(BF16))
        stack = lambda slots: jnp.concatenate(
            [jnp.concatenate([x[:, hh * HG_K:(hh + 1) * HG_K] for x in slots], axis=1) for hh in range(HG_H)], axis=0)
        a_mats.append(_dot_nt(stack(q_slots), stack(k_slots)))

    terms = []
    for i in range(n):
        q, kk, b = qs[i], kks[i], bs[i]
        for j in range(nsub):
            r0 = j * HG_SUB
            qb, bb, kb = q[r0:r0 + HG_SUB], b[r0:r0 + HG_SUB], kk[r0:r0 + HG_SUB]
            terms += [(qb * jnp.exp2(bb - bb[s:s + 1]) * kb[s:s + 1]).astype(BF16) for s in range(HG_SUB)]
    a_bc = jnp.dot(jnp.concatenate(terms, axis=0), e_ref[...], preferred_element_type=F32)

    for i in range(n):
        v_stack = jnp.where(v_keep, jnp.concatenate([vs[i]] * HG_H, axis=0), 0.0).astype(BF16)
        o_stack = jnp.dot(jnp.where(same_blk, a_mats[i], 0.0).astype(BF16), v_stack, preferred_element_type=F32)
        for hh in range(HG_H):
            outs[i] = outs[i] + o_stack[hh * c:(hh + 1) * c]

    pair = lax.broadcasted_iota(jnp.int32, (HG_SUB * HG_SUB, 1), 0)
    keeps = [jnp.broadcast_to((pair % HG_SUB <= pair // HG_SUB) if d else (pair % HG_SUB >= pair // HG_SUB),
                              (HG_SUB * HG_SUB, HG_VW)) for d in range(2)]
    for i, (bi, d) in enumerate(streams):
        diag = []
        for j in range(nsub):
            r0 = j * HG_SUB
            base = (i * nsub + j) * HG_SUB * HG_SUB
            blk = jnp.where(keeps[d], a_bc[base:base + HG_SUB * HG_SUB], 0.0)
            vb = vs[i][r0:r0 + HG_SUB]
            od = blk[0:HG_SUB] * vb[0:1]
            for s in range(1, HG_SUB):
                od = od + blk[s * HG_SUB:(s + 1) * HG_SUB] * vb[s:s + 1]
            diag.append(od)
        (ob_ref if d else of_ref)[bi] = outs[i] + jnp.concatenate(diag, axis=0)

    for i, (bi, d) in enumerate(streams):
        kd = kks[i] * jnp.exp2(b_ends[i] - bs[i])
        upd = jnp.dot(vs[i].T.astype(BF16), kd.astype(BF16), preferred_element_type=F32)
        upd = jnp.concatenate([upd[hh * HG_V:(hh + 1) * HG_V, hh * HG_K:(hh + 1) * HG_K] for hh in range(HG_H)], axis=1)
        st_refs[i][bi] = sts[i] * jnp.exp2(b_ends[i]) + upd


def _hgrn(hg, lb2, e_mat, n_lat):
    b, t, _ = hg.shape
    c = HG_CHUNK
    nb = _pick(b, HG_BATCH)
    nc, nlc = t // c, n_lat // c
    fwd = lambda i: (i + nlc) % nc
    bwd = lambda i: nc - 1 - i
    v_blk = 3 * HG_KW // HG_VW
    return pl.pallas_call(
        functools.partial(_hgrn_body, nb),
        grid=(b // nb, nc),
        in_specs=[
            pl.BlockSpec((nb, c, HG_KW), lambda bi, i: (bi, fwd(i), 0)),
            pl.BlockSpec((nb, c, HG_KW), lambda bi, i: (bi, fwd(i), 1)),
            pl.BlockSpec((nb, c, HG_VW), lambda bi, i: (bi, fwd(i), v_blk)),
            pl.BlockSpec((nb, c, HG_KW), lambda bi, i: (bi, bwd(i), 0)),
            pl.BlockSpec((nb, c, HG_KW), lambda bi, i: (bi, bwd(i), 2)),
            pl.BlockSpec((nb, c, HG_VW), lambda bi, i: (bi, bwd(i), v_blk)),
            _const_spec(lb2.shape), _const_spec(e_mat.shape),
        ],
        out_specs=(pl.BlockSpec((nb, c, HG_VW), lambda bi, i: (bi, fwd(i), 0)),
                   pl.BlockSpec((nb, c, HG_VW), lambda bi, i: (bi, bwd(i), 0))),
        out_shape=(jax.ShapeDtypeStruct((b, t, HG_VW), F32), jax.ShapeDtypeStruct((b, t, HG_VW), F32)),
        scratch_shapes=[pltpu.VMEM((nb, HG_V, HG_KW), F32), pltpu.VMEM((nb, HG_V, HG_KW), F32)],
        compiler_params=_cparams(("arbitrary", "arbitrary")),
        name="hgrn2",
    )(hg, hg, hg, hg, hg, hg, lb2, e_mat)


def _mix_mlp_body(n_lat, ff_chunk, final, x_ref, oda_ref, oml_ref, of_ref, ob_ref, gate_ref, modl_ref, modc_ref,
                  ghn_ref, gmlp_ref, gfin_ref, grp_ref, wda_ref, wml_ref, whg_ref, w1_ref, w2_ref, o_ref):
    tm = x_ref.shape[1]
    rows = pl.program_id(1) * tm + lax.broadcasted_iota(jnp.int32, (tm, 1), 0)
    is_ctx = rows >= n_lat
    modl, modc = modl_ref[0], modc_ref[...]

    oh = of_ref[0] + ob_ref[0]
    sq = oh * oh
    sq_hi = sq.astype(BF16)
    sq_lo = (sq - sq_hi.astype(F32)).astype(BF16)
    ms = (jnp.dot(sq_hi, grp_ref[...], preferred_element_type=F32)
          + jnp.dot(sq_lo, grp_ref[...], preferred_element_type=F32)) * (1.0 / HG_V)
    gt = gate_ref[0]
    yh = oh * lax.rsqrt(ms + EPS) * ghn_ref[...] * (gt * jax.nn.sigmoid(gt))

    mix = (jnp.dot(oda_ref[0], wda_ref[...], preferred_element_type=F32)
           + jnp.dot(oml_ref[0], wml_ref[...], preferred_element_type=F32)
           + jnp.dot(yh.astype(BF16), whg_ref[...], preferred_element_type=F32))
    g_mix = jnp.where(is_ctx, modc[2:3], modl[2:3])
    x1 = x_ref[0] + g_mix * mix

    hn = _ada_rows(x1, gmlp_ref[...], modl, modc, is_ctx, 3, 4).astype(BF16)
    dff = w1_ref.shape[1]
    acc = jnp.zeros(x1.shape, F32)
    for j in range(dff // ff_chunk):
        u = jnp.dot(hn, w1_ref[:, j * ff_chunk:(j + 1) * ff_chunk], preferred_element_type=F32)
        u = jnp.maximum(u, 0.0)
        acc = acc + jnp.dot((u * u).astype(BF16), w2_ref[j * ff_chunk:(j + 1) * ff_chunk, :],
                            preferred_element_type=F32)
    g_mlp = jnp.where(is_ctx, modc[5:6], modl[5:6])
    x2 = x1 + g_mlp * acc
    if final:
        x2 = _rms(x2, gfin_ref[...])
    o_ref[0] = x2


def _mix_mlp(xa, oda, oml, o_f, o_b, hg, modl, modc, ghn, gmlp, gfin, grp, wda, wml, whg, w1, w2, n_lat, final):
    b, t, d = xa.shape
    t_out = n_lat if final else t
    tm = _pick(t_out, (512, 384, 256, 128)) if final else _pick(t, (384, 256, 128))
    gate_blk = (3 * HG_KW + HG_VW) // HG_VW
    row = lambda w: pl.BlockSpec((1, tm, w), lambda bi, i: (bi, i, 0))
    consts = (modc, ghn, gmlp, gfin, grp, wda, wml, whg, w1, w2)
    return pl.pallas_call(
        functools.partial(_mix_mlp_body, n_lat, 1024, final),
        grid=(b, t_out // tm),
        in_specs=[row(d), row(DA_W), row(ML_W), row(HG_VW), row(HG_VW),
                  pl.BlockSpec((1, tm, HG_VW), lambda bi, i: (bi, i, gate_blk)),
                  pl.BlockSpec((1, N_MOD, d), lambda bi, i: (bi, 0, 0))]
                 + [_const_spec(a.shape) for a in consts],
        out_specs=row(d),
        out_shape=jax.ShapeDtypeStruct((b, t_out, d), F32),
        compiler_params=_cparams(("arbitrary", "arbitrary")),
        name="mix_mlp",
    )(xa, oda, oml, o_f, o_b, hg, modl, *consts)


def _swap_cols(w):
    n = w.shape[-1]
    j = jnp.arange(n)
    src = (j // 16) * 16 + (j % 16 + 8) % 16
    return w[..., src]


def _slot_cols(w, width):
    k, n = w.shape
    hs = n // width
    return jnp.pad(w.reshape(k, hs, width), ((0, 0), (0, 0), (0, LANE - width))).reshape(k, hs * LANE)


def _layer_weights(w_in, w_uq, w_ukv):
    o = 0
    parts = {}
    for name, wdt in (("da_q", DA_W), ("da_k", DA_W), ("da_v", DA_W), ("cq", ML_QR), ("ckv", ML_KVR), ("kr", ROT),
                      ("hq", HG_KW), ("hzf", HG_KW), ("hzb", HG_KW), ("hv", HG_VW), ("hgate", HG_VW)):
        parts[name] = w_in[:, o:o + wdt]
        o += wdt
    d = w_in.shape[0]
    w1 = jnp.concatenate([parts["da_q"], parts["da_k"], jnp.zeros((d, ML_NOPE), F32), parts["kr"],
                          jnp.zeros((d, LANE - ML_NOPE - ROT), F32)], axis=1)
    w2 = jnp.concatenate([parts["da_v"], parts["cq"], parts["ckv"]], axis=1)
    w3 = jnp.concatenate([parts[n] for n in ("hq", "hzf", "hzb", "hv", "hgate")], axis=1)
    uq = w_uq.reshape(ML_QR, ML_H, ML_NOPE + ROT)
    uq_sw = jnp.concatenate([jnp.zeros((ML_QR, ML_H, ML_NOPE), F32), _swap_cols(uq[..., ML_NOPE:])], axis=-1)
    pad_q = lambda a: jnp.pad(a, ((0, 0), (0, 0), (0, LANE - ML_NOPE - ROT))).reshape(ML_QR, ML_H * LANE)
    ukv = w_ukv.reshape(ML_KVR, ML_H, ML_NOPE + ML_V)
    wkn = _slot_cols(ukv[..., :ML_NOPE].reshape(ML_KVR, ML_H * ML_NOPE), ML_NOPE)
    wvm = ukv[..., ML_NOPE:].reshape(ML_KVR, ML_H * ML_V)
    bf = lambda a: a.astype(BF16)
    return bf(w1), bf(_swap_cols(w1)), bf(w2), bf(w3), bf(pad_q(uq)), bf(pad_q(uq_sw)), bf(wkn), bf(wvm)


def _rope_tables(n_lat, n_ctx):
    rows = n_lat // GRID_W
    row = jnp.repeat(jnp.arange(rows, dtype=F32), GRID_W)
    col = jnp.tile(jnp.arange(GRID_W, dtype=F32), rows)
    n_freq = ROT // 4
    inv = ROPE_THETA ** (-jnp.arange(n_freq, dtype=F32) / n_freq)
    ar, ac = row[:, None] * inv, col[:, None] * inv
    c32 = jnp.concatenate([jnp.cos(ar), jnp.cos(ar), jnp.cos(ac), jnp.cos(ac)], axis=1)
    s32 = jnp.concatenate([-jnp.sin(ar), jnp.sin(ar), -jnp.sin(ac), jnp.sin(ac)], axis=1)
    c32 = jnp.concatenate([c32, jnp.ones((n_ctx, ROT), F32)], axis=0)
    s32 = jnp.concatenate([s32, jnp.zeros((n_ctx, ROT), F32)], axis=0)
    t = n_lat + n_ctx
    one, zero = jnp.ones((t, ML_NOPE), F32), jnp.zeros((t, ML_NOPE), F32)
    pad = jnp.zeros((t, LANE - ML_NOPE - ROT), F32)
    return jnp.concatenate([jnp.tile(c32, (1, LANE // ROT)), jnp.tile(s32, (1, LANE // ROT)),
                            one, c32, pad, zero, s32, pad], axis=1)


def kernel(x, c, ctx, c_ctx, w_mod, b_mod, g_mix, g_mlp, w_in, w_out, da_lambda, da_subln_g, mla_g_cq, mla_g_ckv,
           mla_w_uq, mla_w_ukv, hg_lower_bounds, hg_norm_g, w_ff1, w_ff2, g_final):
    bsz, n_lat, d = x.shape
    n_ctx = ctx.shape[1]
    depth = w_in.shape[0]
    t = n_lat + n_ctx

    lb = jax.nn.softmax(hg_lower_bounds.astype(F32), axis=1)
    lb = jnp.cumsum(lb, axis=1) - lb[:, :1]
    cond = jnp.concatenate([c, c_ctx[None]], axis=0)
    mod = _modulation(cond, w_mod, b_mod).reshape(depth, bsz + 1, N_MOD, d)
    rope = _rope_tables(n_lat, n_ctx)

    grp = (jnp.arange(HG_VW)[:, None] // HG_V == jnp.arange(HG_VW)[None, :] // HG_V).astype(BF16)
    e_mat = (jnp.arange(HG_KW)[:, None] // HG_K == jnp.arange(HG_VW)[None, :] // HG_V).astype(BF16)

    tq_da = _pick(n_lat, TQ_DA)
    tq_ml = _pick(n_lat, TQ_ML)
    tk = _pick(t, TK)
    tc = _pick(n_ctx, (256, 128))

    xa = jnp.concatenate([x, ctx], axis=1)
    for l in range(depth):
        last = l == depth - 1
        lam_init = 0.8 - 0.6 * math.exp(-0.3 * l)
        lam = (jnp.exp(jnp.sum(da_lambda[l, 0] * da_lambda[l, 1]))
               - jnp.exp(jnp.sum(da_lambda[l, 2] * da_lambda[l, 3])) + lam_init).astype(F32).reshape(1, 1)
        modl, modc = mod[l, :bsz], mod[l, bsz]
        w1, w1s, w2, w3, wuq, wuqs, wkn, wvm = _layer_weights(w_in[l], mla_w_uq[l], mla_w_ukv[l])
        wts = (w1, w1s, w2, w3, mla_g_cq[l][None], mla_g_ckv[l][None], wuq, wuqs, wkn, wvm)
        qda, kda, vda, qm, km, vm, hg = _project(xa, modl, modc, g_mix[l][None], rope, wts, n_lat)

        g_da = jnp.broadcast_to(da_subln_g[l][:, None], (2 * DA_D, LANE))
        da_kw = dict(group=2, diff_scale=1.0 - lam_init)
        ml_kw = dict(group=1, diff_scale=1.0)
        oda = _flash(qda, kda, vda, lam, g_da, t_q=n_lat, q_off=0, t_kv=t, kv_off=0, tq=tq_da, tk=tk, **da_kw)
        oml = _flash(qm, km, vm, lam, g_da, t_q=n_lat, q_off=0, t_kv=t, kv_off=0, tq=tq_ml, tk=tk, **ml_kw)
        if not last:
            ctx_kw = dict(t_q=n_ctx, q_off=n_lat, t_kv=n_ctx, kv_off=n_lat, tq=tc, tk=tc)
            oda = jnp.concatenate([oda, _flash(qda, kda, vda, lam, g_da, **ctx_kw, **da_kw)], axis=1)
            oml = jnp.concatenate([oml, _flash(qm, km, vm, lam, g_da, **ctx_kw, **ml_kw)], axis=1)

        o_f, o_b = _hgrn(hg, jnp.stack([lb[0, l], lb[1, l]]), e_mat, n_lat)

        ghn = jnp.tile(hg_norm_g[l], HG_H)[None]
        bf = lambda a: a.astype(BF16)
        xa = _mix_mlp(xa, oda, oml, o_f, o_b, hg, modl, modc, ghn, g_mlp[l][None], g_final[None], grp,
                      bf(w_out[l, :DA_W]), bf(w_out[l, DA_W:DA_W + ML_W]), bf(w_out[l, DA_W + ML_W:]),
                      bf(w_ff1[l]), bf(w_ff2[l]), n_lat, last)
    return xa
```

```python
import functools
import math

import jax
import jax.numpy as jnp
from jax import lax
from jax.experimental import pallas as pl
from jax.experimental.pallas import tpu as pltpu

F32 = jnp.float32
BF16 = jnp.bfloat16

EPS = 1e-6
ROPE_THETA = 10000.0
GRID_W = 64
ROT = 32
DA_H, DA_D = 6, 32
DA_W = DA_H * 2 * DA_D
ML_H, ML_QR, ML_KVR, ML_NOPE, ML_V = 6, 256, 128, 64, 64
ML_W = ML_H * ML_V
HG_H, HG_K, HG_V = 4, 128, 64
HG_KW, HG_VW = HG_H * HG_K, HG_H * HG_V
FORGET_FLOOR = 1e-30
N_MOD = 6
LOG2E = 1.4426950408889634

HG_CHUNK = 64
HG_BATCH = (4, 2, 1)
HG_SUB = 8
LANE = 128
VMEM_LIMIT = 56 * 1024 * 1024
TQ_DA = (1024, 512, 256, 128)
TQ_ML = (2048, 1024, 512, 256, 128)
TK = (768, 512, 384, 256, 128)
VT_ROWS = 80
LOOKAHEAD = 6
FLASH_KEYS = 256
FLASH_ROWS = 256


def _cparams(sem):
    return pltpu.CompilerParams(dimension_semantics=sem, vmem_limit_bytes=VMEM_LIMIT)


def _const_spec(shape):
    nd = len(shape)
    return pl.BlockSpec(shape, lambda *_: (0,) * nd, pipeline_mode=pl.Buffered(1))


def _pick(n, cands):
    for c in cands:
        if n % c == 0:
            return c
    raise ValueError(f"no tile for {n} in {cands}")


def _mod_body(c_ref, w_ref, b_ref, o_ref):
    cc = c_ref[...]
    s = cc * jax.nn.sigmoid(cc)
    o_ref[0] = jnp.dot(s.astype(BF16), w_ref[0].astype(BF16), preferred_element_type=F32) + b_ref[0]


def _modulation(cond, w_mod, b_mod):
    depth, d, nm = w_mod.shape
    r = cond.shape[0]
    tn = _pick(nm, (1536, 1024, 512, 128))
    return pl.pallas_call(
        _mod_body,
        grid=(depth, nm // tn),
        in_specs=[pl.BlockSpec((r, d), lambda l, j: (0, 0)),
                  pl.BlockSpec((1, d, tn), lambda l, j: (l, 0, j)),
                  pl.BlockSpec((1, 1, tn), lambda l, j: (l, 0, j))],
        out_specs=pl.BlockSpec((1, r, tn), lambda l, j: (l, 0, j)),
        out_shape=jax.ShapeDtypeStruct((depth, r, nm), F32),
        compiler_params=_cparams(("arbitrary", "arbitrary")),
        name="modulation",
    )(cond, w_mod, b_mod.reshape(depth, 1, nm))


def _ada_rows(x, g, modl, modc, is_ctx, i_shift, i_scale):
    ms = jnp.mean(x * x, axis=-1, keepdims=True)
    y = x * lax.rsqrt(ms + EPS) * g
    shift = jnp.where(is_ctx, modc[i_shift:i_shift + 1], modl[i_shift:i_shift + 1])
    scale = jnp.where(is_ctx, modc[i_scale:i_scale + 1], modl[i_scale:i_scale + 1])
    return y * (1.0 + scale) + shift


def _rms(x, g):
    return x * lax.rsqrt(jnp.mean(x * x, axis=-1, keepdims=True) + EPS) * g


def _proj_body(n_lat, x_ref, modl_ref, modc_ref, g_ref, rope_ref, w1_ref, w1s_ref, w2_ref, w3_ref,
               gcq_ref, gckv_ref, wuq_ref, wuqs_ref, wkn_ref, wvm_ref,
               qda_ref, kda_ref, vda_ref, qm_ref, km_ref, vm_ref, hg_ref):
    tm = x_ref.shape[1]
    rows = pl.program_id(1) * tm + lax.broadcasted_iota(jnp.int32, (tm, 1), 0)
    is_ctx = rows >= n_lat
    h = _ada_rows(x_ref[0], g_ref[...], modl_ref[0], modc_ref[...], is_ctx, 0, 1).astype(BF16)

    rope = rope_ref[...]
    c4, s4, cq_t, sq_t = (rope[:, k * LANE:(k + 1) * LANE] for k in range(4))
    n1 = w1_ref.shape[1] // LANE
    p1 = jnp.dot(h, w1_ref[...], preferred_element_type=F32)
    p1s = jnp.dot(h, w1s_ref[...], preferred_element_type=F32)
    rot = p1 * jnp.concatenate([c4] * n1, axis=1) + p1s * jnp.concatenate([s4] * n1, axis=1)

    rq_t = (rot[:, :DA_W] * (DA_D ** -0.5 * LOG2E)).T
    row64 = lax.broadcasted_iota(jnp.int32, (2 * DA_D, 1), 0)
    for vh in range(2 * DA_H):
        hh, comp = vh // 2, vh % 2
        keep = (row64 >= DA_D) if comp else (row64 < DA_D)
        qda_ref[0, vh] = jnp.where(keep, rq_t[hh * 2 * DA_D:(hh + 1) * 2 * DA_D], 0.0).astype(BF16)
    for hh in range(DA_H):
        kda_ref[0, hh] = rot[:, DA_W + hh * 2 * DA_D:DA_W + (hh + 1) * 2 * DA_D].astype(BF16)
    kr_slot = rot[:, 2 * DA_W:2 * DA_W + LANE]

    ones_rows = (lax.broadcasted_iota(jnp.int32, (VT_ROWS - ML_V, tm), 0) == 0).astype(F32)
    p2 = jnp.dot(h, w2_ref[...], preferred_element_type=F32)
    v_t = p2[:, :DA_W].T
    for hh in range(DA_H):
        vda_ref[0, hh] = jnp.concatenate([v_t[hh * ML_V:(hh + 1) * ML_V], ones_rows], axis=0).astype(BF16)
    cqn = _rms(p2[:, DA_W:DA_W + ML_QR], gcq_ref[...]).astype(BF16)
    ckvn = _rms(p2[:, DA_W + ML_QR:DA_W + ML_QR + ML_KVR], gckv_ref[...]).astype(BF16)

    qu = jnp.dot(cqn, wuq_ref[...], preferred_element_type=F32)
    qus = jnp.dot(cqn, wuqs_ref[...], preferred_element_type=F32)
    qm = (qu * jnp.concatenate([cq_t] * ML_H, axis=1) + qus * jnp.concatenate([sq_t] * ML_H, axis=1))
    qm_ref[0] = (qm * ((ML_NOPE + ROT) ** -0.5 * LOG2E)).T.reshape(ML_H, LANE, tm).astype(BF16)
    kn = jnp.dot(ckvn, wkn_ref[...], preferred_element_type=F32)
    for hh in range(ML_H):
        km_ref[0, hh] = (kn[:, hh * LANE:(hh + 1) * LANE] + kr_slot).astype(BF16)
    vm_t = jnp.dot(ckvn, wvm_ref[...], preferred_element_type=F32).T
    for hh in range(ML_H):
        vm_ref[0, hh] = jnp.concatenate([vm_t[hh * ML_V:(hh + 1) * ML_V], ones_rows], axis=0).astype(BF16)

    hg_ref[0] = jnp.dot(h, w3_ref[...], preferred_element_type=F32)


def _project(xa, modl, modc, g, rope, wts, n_lat):
    b, t, d = xa.shape
    tm = _pick(t, (768, 384, 256, 128))
    w1, w1s, w2, w3, gcq, gckv, wuq, wuqs, wkn, wvm = wts
    nhg = w3.shape[1]
    rows = lambda hs, w: (jax.ShapeDtypeStruct((b, hs, t, w), BF16),
                          pl.BlockSpec((1, hs, tm, w), lambda bi, i: (bi, 0, i, 0)))
    cols = lambda hs, r: (jax.ShapeDtypeStruct((b, hs, r, t), BF16),
                          pl.BlockSpec((1, hs, r, tm), lambda bi, i: (bi, 0, 0, i)))
    outs = (cols(2 * DA_H, 2 * DA_D), rows(DA_H, 2 * DA_D), cols(DA_H, VT_ROWS),
            cols(ML_H, LANE), rows(ML_H, LANE), cols(ML_H, VT_ROWS),
            (jax.ShapeDtypeStruct((b, t, nhg), F32), pl.BlockSpec((1, tm, nhg), lambda bi, i: (bi, i, 0))))
    out_shapes = tuple(o[0] for o in outs)
    out_specs = tuple(o[1] for o in outs)
    in_specs = [
        pl.BlockSpec((1, tm, d), lambda bi, i: (bi, i, 0)),
        pl.BlockSpec((1, N_MOD, d), lambda bi, i: (bi, 0, 0)),
        _const_spec(modc.shape), _const_spec(g.shape),
        pl.BlockSpec((tm, rope.shape[1]), lambda bi, i: (i, 0)),
    ] + [_const_spec(w.shape) for w in wts]
    return pl.pallas_call(
        functools.partial(_proj_body, n_lat),
        grid=(b, t // tm),
        in_specs=in_specs, out_specs=out_specs, out_shape=out_shapes,
        compiler_params=_cparams(("arbitrary", "arbitrary")),
        name="project",
    )(xa, modl, modc, g, rope, *wts)


def _flash_body(n_kv, group, diff_scale, rc, lam_ref, g_ref, qt_ref, k_ref, vt_ref, *refs):
    o_ref, m_ref, acc_ref = refs[-3:]
    ki = pl.program_id(2)
    nk = pl.num_programs(2)
    tq = qt_ref.shape[3]

    @pl.when(ki == 0)
    def _():
        m_ref[...] = jnp.full(m_ref.shape, -jnp.inf, F32)
        acc_ref[...] = jnp.zeros(acc_ref.shape, F32)

    tk = k_ref.shape[2]
    ks = min(FLASH_KEYS, tk)
    chains = [(hh, c0, k0) for hh in range(n_kv) for c0 in range(0, group * tq, rc) for k0 in range(0, tk, ks)]

    def scores(hh, c0, k0):
        qt = qt_ref[0, hh * group + c0 // tq, :, c0 % tq:c0 % tq + rc]
        return jnp.dot(k_ref[0, hh, k0:k0 + ks, :], qt, preferred_element_type=F32)

    def softmax(hh, c0, k0, s):
        cols = slice(c0, c0 + rc)
        m_prev = m_ref[hh, :, cols]
        m_new = jnp.maximum(m_prev, jnp.max(s, axis=0, keepdims=True))
        m_ref[hh, :, cols] = m_new
        return jnp.exp2(s - m_new[0:1]).astype(BF16), jnp.exp2(m_prev - m_new)[0:1]

    def accumulate(hh, c0, k0, p, alpha):
        cols = slice(c0, c0 + rc)
        acc_ref[hh, :, cols] = alpha * acc_ref[hh, :, cols] + jnp.dot(
            vt_ref[0, hh, :, k0:k0 + ks], p, preferred_element_type=F32)

    n = len(chains)
    s_q = {i: scores(*chains[i]) for i in range(min(LOOKAHEAD, n))}
    p_q = {0: softmax(*chains[0], s_q.pop(0))}
    for ci in range(n):
        if ci + LOOKAHEAD < n:
            s_q[ci + LOOKAHEAD] = scores(*chains[ci + LOOKAHEAD])
        if ci + 1 < n:
            p_q[ci + 1] = softmax(*chains[ci + 1], s_q.pop(ci + 1))
        accumulate(*chains[ci], *p_q.pop(ci))

    @pl.when(ki == nk - 1)
    def _():
        outs = []
        for hh in range(n_kv):
            acc = acc_ref[hh]
            o = acc[:ML_V] / acc[ML_V:ML_V + 1]
            if group == 2:
                od = o[:, :tq] - lam_ref[0, 0] * o[:, tq:]
                ms = jnp.mean(od * od, axis=0, keepdims=True)
                o = od * lax.rsqrt(ms + EPS) * jnp.concatenate([g_ref[...]] * (tq // LANE), axis=1) * diff_scale
            outs.append(o.T)
        o_ref[0] = jnp.concatenate(outs, axis=-1).astype(o_ref.dtype)


def _flash(qt, k, vt, lam, g, into=None, *, group, diff_scale, t_q, q_off, t_kv, kv_off, tq, tk, t_out):
    b, hq, dk, _ = qt.shape
    n_kv = k.shape[1]
    assert hq == n_kv * group and q_off % tq == 0 and kv_off % tk == 0
    qo, ko = q_off // tq, kv_off // tk
    extra = () if into is None else (into,)
    return pl.pallas_call(
        functools.partial(_flash_body, n_kv, group, diff_scale, min(FLASH_ROWS, tq)),
        grid=(b, t_q // tq, t_kv // tk),
        in_specs=[
            pl.BlockSpec(memory_space=pltpu.SMEM),
            _const_spec(g.shape),
            pl.BlockSpec((1, hq, dk, tq), lambda bi, qi, ki: (bi, 0, 0, qi + qo)),
            pl.BlockSpec((1, n_kv, tk, dk), lambda bi, qi, ki: (bi, 0, ki + ko, 0)),
            pl.BlockSpec((1, n_kv, VT_ROWS, tk), lambda bi, qi, ki: (bi, 0, 0, ki + ko)),
        ] + [pl.BlockSpec(memory_space=pl.ANY)] * len(extra),
        out_specs=pl.BlockSpec((1, tq, n_kv * ML_V), lambda bi, qi, ki: (bi, qi + qo, 0)),
        out_shape=jax.ShapeDtypeStruct((b, t_out, n_kv * ML_V), BF16),
        input_output_aliases={5: 0} if extra else {},
        scratch_shapes=[pltpu.VMEM((n_kv, 8, group * tq), F32),
                        pltpu.VMEM((n_kv, VT_ROWS, group * tq), F32)],
        compiler_params=_cparams(("arbitrary", "arbitrary", "arbitrary")),
        name="flash_da" if group == 2 else "flash_mla",
    )(lam, g, qt, k, vt, *extra)


def _cumsum_rows(x):
    n, w = x.shape
    hi = x.astype(BF16)
    r1 = x - hi.astype(F32)
    mid = r1.astype(BF16)
    lo = (r1 - mid.astype(F32)).astype(BF16)
    tril = (lax.broadcasted_iota(jnp.int32, (n, n), 0) >= lax.broadcasted_iota(jnp.int32, (n, n), 1)).astype(BF16)
    parts = jnp.dot(tril, jnp.concatenate([hi, mid, lo], axis=1), preferred_element_type=F32)
    return parts[:, :w] + parts[:, w:2 * w] + parts[:, 2 * w:]


def _dot_nt(a, b):
    return lax.dot_general(a, b, (((1,), (1,)), ((), ())), preferred_element_type=F32)


def _pad_rows(x, start, total):
    parts = []
    if start:
        parts.append(jnp.zeros((start, x.shape[1]), x.dtype))
    parts.append(x)
    if total - start - x.shape[0]:
        parts.append(jnp.zeros((total - start - x.shape[0], x.shape[1]), x.dtype))
    return jnp.concatenate(parts, axis=0) if len(parts) > 1 else x


def _hgrn_body(nb, qf_ref, zf_ref, vf_ref, qb_ref, zb_ref, vb_ref, lb_ref, e_ref, of_ref, ob_ref, stf_ref, stb_ref):
    @pl.when(pl.program_id(1) == 0)
    def _():
        stf_ref[...] = jnp.zeros(stf_ref.shape, F32)
        stb_ref[...] = jnp.zeros(stb_ref.shape, F32)

    c, nsub, kw = HG_CHUNK, HG_CHUNK // HG_SUB, HG_KW
    streams = [(bi, d) for bi in range(nb) for d in range(2)]
    n = len(streams)
    qs = [(qb_ref if d else qf_ref)[bi] for bi, d in streams]
    vs = [(vb_ref if d else vf_ref)[bi] for bi, d in streams]
    st_refs = [(stb_ref if d else stf_ref) for bi, d in streams]
    sts = [st_refs[i][bi] for i, (bi, d) in enumerate(streams)]

    z = jnp.concatenate([(zb_ref if d else zf_ref)[bi] for bi, d in streams], axis=1)
    lb = jnp.concatenate([lb_ref[d:d + 1] for bi, d in streams], axis=1)
    f = lb + (1.0 - lb) * jax.nn.sigmoid(z)
    kk_all = (1.0 - lb) * jax.nn.sigmoid(-z)
    logf = jnp.log(jnp.maximum(f, FORGET_FLOOR))
    bf = _cumsum_rows(logf)
    kks, bs, b_ends = [], [], []
    for i, (bi, d) in enumerate(streams):
        sl = slice(i * kw, (i + 1) * kw)
        kks.append(kk_all[:, sl])
        b = (bf[c - 1:c, sl] - bf[:, sl] + logf[:, sl]) * LOG2E if d else bf[:, sl] * LOG2E
        bs.append(b)
        b_ends.append(b[0:1] if d else b[c - 1:c])

    def block_diag(st):
        st = st.astype(BF16)
        zero = jnp.zeros((HG_V, HG_K), BF16)
        return jnp.concatenate([jnp.concatenate(
            [st[:, hh * HG_K:(hh + 1) * HG_K] if j == hh else zero for j in range(HG_H)], axis=1)
            for hh in range(HG_H)], axis=0)

    outs = [_dot_nt((qs[i] * jnp.exp2(bs[i])).astype(BF16), block_diag(sts[i])) for i in range(n)]

    head_r = lax.broadcasted_iota(jnp.int32, (HG_H * c, 1), 0) // c
    same_blk = head_r == lax.broadcasted_iota(jnp.int32, (1, HG_H * c), 1) // c
    v_keep = head_r == lax.broadcasted_iota(jnp.int32, (1, HG_VW), 1) // HG_V
    a_mats = []
    for i, (bi, d) in enumerate(streams):
        q, kk, b = qs[i], kks[i], bs[i]
        q_slots, k_slots = [], []
        for j in (range(nsub - 1) if d else range(1, nsub)):
            r0 = j * HG_SUB
            if d:
                m, k0, k1 = b[r0 + HG_SUB:r0 + HG_SUB + 1], r0 + HG_SUB, c
            else:
                m, k0, k1 = b[r0 - 1:r0], 0, r0
            q_slots.append(_pad_rows(q[r0:r0 + HG_SUB] * jnp.exp2(b[r0:r0 + HG_SUB] - m), r0, c).astype(BF16))
            k_slots.append(_pad_rows(kk[k0:k1] * jnp.exp2(m - b[k0:k1]), k0, c).astype(BF16))
        stack = lambda slots: jnp.concatenate(
            [jnp.concatenate([x[:, hh * HG_K:(hh + 1) * HG_K] for x in slots], axis=1) for hh in range(HG_H)], axis=0)
        a_mats.append(_dot_nt(stack(q_slots), stack(k_slots)))

    terms = []
    for i in range(n):
        q, kk, b = qs[i], kks[i], bs[i]
        for j in range(nsub):
            r0 = j * HG_SUB
            qb, bb, kb = q[r0:r0 + HG_SUB], b[r0:r0 + HG_SUB], kk[r0:r0 + HG_SUB]
            terms += [(qb * jnp.exp2(bb - bb[s:s + 1]) * kb[s:s + 1]).astype(BF16) for s in range(HG_SUB)]
    a_bc = jnp.dot(jnp.concatenate(terms, axis=0), e_ref[...], preferred_element_type=F32)

    for i in range(n):
        v_stack = jnp.where(v_keep, jnp.concatenate([vs[i]] * HG_H, axis=0), 0.0).astype(BF16)
        o_stack = jnp.dot(jnp.where(same_blk, a_mats[i], 0.0).astype(BF16), v_stack, preferred_element_type=F32)
        for hh in range(HG_H):
            outs[i] = outs[i] + o_stack[hh * c:(hh + 1) * c]

    pair = lax.broadcasted_iota(jnp.int32, (HG_SUB * HG_SUB, 1), 0)
    keeps = [jnp.broadcast_to((pair % HG_SUB <= pair // HG_SUB) if d else (pair % HG_SUB >= pair // HG_SUB),
                              (HG_SUB * HG_SUB, HG_VW)) for d in range(2)]
    for i, (bi, d) in enumerate(streams):
        diag = []
        for j in range(nsub):
            r0 = j * HG_SUB
            base = (i * nsub + j) * HG_SUB * HG_SUB
            blk = jnp.where(keeps[d], a_bc[base:base + HG_SUB * HG_SUB], 0.0)
            vb = vs[i][r0:r0 + HG_SUB]
            od = blk[0:HG_SUB] * vb[0:1]
            for s in range(1, HG_SUB):
                od = od + blk[s * HG_SUB:(s + 1) * HG_SUB] * vb[s:s + 1]
            diag.append(od)
        (ob_ref if d else of_ref)[bi] = outs[i] + jnp.concatenate(diag, axis=0)

    for i, (bi, d) in enumerate(streams):
        kd = kks[i] * jnp.exp2(b_ends[i] - bs[i])
        upd = jnp.dot(vs[i].T.astype(BF16), kd.astype(BF16), preferred_element_type=F32)
        upd = jnp.concatenate([upd[hh * HG_V:(hh + 1) * HG_V, hh * HG_K:(hh + 1) * HG_K] for hh in range(HG_H)], axis=1)
        st_refs[i][bi] = sts[i] * jnp.exp2(b_ends[i]) + upd


def _hgrn(hg, lb2, e_mat, n_lat):
    b, t, _ = hg.shape
    c = HG_CHUNK
    nb = _pick(b, HG_BATCH)
    nc, nlc = t // c, n_lat // c
    fwd = lambda i: (i + nlc) % nc
    bwd = lambda i: nc - 1 - i
    v_blk = 3 * HG_KW // HG_VW
    return pl.pallas_call(
        functools.partial(_hgrn_body, nb),
        grid=(b // nb, nc),
        in_specs=[
            pl.BlockSpec((nb, c, HG_KW), lambda bi, i: (bi, fwd(i), 0)),
            pl.BlockSpec((nb, c, HG_KW), lambda bi, i: (bi, fwd(i), 1)),
            pl.BlockSpec((nb, c, HG_VW), lambda bi, i: (bi, fwd(i), v_blk)),
            pl.BlockSpec((nb, c, HG_KW), lambda bi, i: (bi, bwd(i), 0)),
            pl.BlockSpec((nb, c, HG_KW), lambda bi, i: (bi, bwd(i), 2)),
            pl.BlockSpec((nb, c, HG_VW), lambda bi, i: (bi, bwd(i), v_blk)),
            _const_spec(lb2.shape), _const_spec(e_mat.shape),
        ],
        out_specs=(pl.BlockSpec((nb, c, HG_VW), lambda bi, i: (bi, fwd(i), 0)),
                   pl.BlockSpec((nb, c, HG_VW), lambda bi, i: (bi, bwd(i), 0))),
        out_shape=(jax.ShapeDtypeStruct((b, t, HG_VW), F32), jax.ShapeDtypeStruct((b, t, HG_VW), F32)),
        scratch_shapes=[pltpu.VMEM((nb, HG_V, HG_KW), F32), pltpu.VMEM((nb, HG_V, HG_KW), F32)],
        compiler_params=_cparams(("arbitrary", "arbitrary")),
        name="hgrn2",
    )(hg, hg, hg, hg, hg, hg, lb2, e_mat)


def _mix_mlp_body(n_lat, ff_chunk, final, x_ref, oda_ref, oml_ref, of_ref, ob_ref, gate_ref, modl_ref, modc_ref,
                  ghn_ref, gmlp_ref, gfin_ref, grp_ref, wda_ref, wml_ref, whg_ref, w1_ref, w2_ref, o_ref):
    tm = x_ref.shape[1]
    rows = pl.program_id(1) * tm + lax.broadcasted_iota(jnp.int32, (tm, 1), 0)
    is_ctx = rows >= n_lat
    modl, modc = modl_ref[0], modc_ref[...]

    oh = of_ref[0] + ob_ref[0]
    sq = oh * oh
    sq_hi = sq.astype(BF16)
    sq_lo = (sq - sq_hi.astype(F32)).astype(BF16)
    ms = (jnp.dot(sq_hi, grp_ref[...], preferred_element_type=F32)
          + jnp.dot(sq_lo, grp_ref[...], preferred_element_type=F32)) * (1.0 / HG_V)
    gt = gate_ref[0]
    yh = oh * lax.rsqrt(ms + EPS) * ghn_ref[...] * (gt * jax.nn.sigmoid(gt))

    mix = (jnp.dot(oda_ref[0], wda_ref[...], preferred_element_type=F32)
           + jnp.dot(oml_ref[0], wml_ref[...], preferred_element_type=F32)
           + jnp.dot(yh.astype(BF16), whg_ref[...], preferred_element_type=F32))
    g_mix = jnp.where(is_ctx, modc[2:3], modl[2:3])
    x1 = x_ref[0] + g_mix * mix

    hn = _ada_rows(x1, gmlp_ref[...], modl, modc, is_ctx, 3, 4).astype(BF16)
    dff = w1_ref.shape[1]
    acc = jnp.zeros(x1.shape, F32)
    for j in range(dff // ff_chunk):
        u = jnp.dot(hn, w1_ref[:, j * ff_chunk:(j + 1) * ff_chunk], preferred_element_type=F32)
        u = jnp.maximum(u, 0.0)
        acc = acc + jnp.dot((u * u).astype(BF16), w2_ref[j * ff_chunk:(j + 1) * ff_chunk, :],
                            preferred_element_type=F32)
    g_mlp = jnp.where(is_ctx, modc[5:6], modl[5:6])
    x2 = x1 + g_mlp * acc
    if final:
        x2 = _rms(x2, gfin_ref[...])
    o_ref[0] = x2


def _mix_mlp(xa, oda, oml, o_f, o_b, hg, modl, modc, ghn, gmlp, gfin, grp, wda, wml, whg, w1, w2, n_lat, final):
    b, t, d = xa.shape
    t_out = n_lat if final else t
    tm = _pick(t_out, (512, 384, 256, 128)) if final else _pick(t, (384, 256, 128))
    gate_blk = (3 * HG_KW + HG_VW) // HG_VW
    row = lambda w: pl.BlockSpec((1, tm, w), lambda bi, i: (bi, i, 0))
    consts = (modc, ghn, gmlp, gfin, grp, wda, wml, whg, w1, w2)
    return pl.pallas_call(
        functools.partial(_mix_mlp_body, n_lat, 1024, final),
        grid=(b, t_out // tm),
        in_specs=[row(d), row(DA_W), row(ML_W), row(HG_VW), row(HG_VW),
                  pl.BlockSpec((1, tm, HG_VW), lambda bi, i: (bi, i, gate_blk)),
                  pl.BlockSpec((1, N_MOD, d), lambda bi, i: (bi, 0, 0))]
                 + [_const_spec(a.shape) for a in consts],
        out_specs=row(d),
        out_shape=jax.ShapeDtypeStruct((b, t_out, d), F32),
        compiler_params=_cparams(("arbitrary", "arbitrary")),
        name="mix_mlp",
    )(xa, oda, oml, o_f, o_b, hg, modl, *consts)


def _swap_cols(w):
    n = w.shape[-1]
    j = jnp.arange(n)
    src = (j // 16) * 16 + (j % 16 + 8) % 16
    return w[..., src]


def _slot_cols(w, width):
    k, n = w.shape
    hs = n // width
    return jnp.pad(w.reshape(k, hs, width), ((0, 0), (0, 0), (0, LANE - width))).reshape(k, hs * LANE)


def _layer_weights(w_in, w_uq, w_ukv):
    o = 0
    parts = {}
    for name, wdt in (("da_q", DA_W), ("da_k", DA_W), ("da_v", DA_W), ("cq", ML_QR), ("ckv", ML_KVR), ("kr", ROT),
                      ("hq", HG_KW), ("hzf", HG_KW), ("hzb", HG_KW), ("hv", HG_VW), ("hgate", HG_VW)):
        parts[name] = w_in[:, o:o + wdt]
        o += wdt
    d = w_in.shape[0]
    w1 = jnp.concatenate([parts["da_q"], parts["da_k"], jnp.zeros((d, ML_NOPE), F32), parts["kr"],
                          jnp.zeros((d, LANE - ML_NOPE - ROT), F32)], axis=1)
    w2 = jnp.concatenate([parts["da_v"], parts["cq"], parts["ckv"]], axis=1)
    w3 = jnp.concatenate([parts[n] for n in ("hq", "hzf", "hzb", "hv", "hgate")], axis=1)
    uq = w_uq.reshape(ML_QR, ML_H, ML_NOPE + ROT)
    uq_sw = jnp.concatenate([jnp.zeros((ML_QR, ML_H, ML_NOPE), F32), _swap_cols(uq[..., ML_NOPE:])], axis=-1)
    pad_q = lambda a: jnp.pad(a, ((0, 0), (0, 0), (0, LANE - ML_NOPE - ROT))).reshape(ML_QR, ML_H * LANE)
    ukv = w_ukv.reshape(ML_KVR, ML_H, ML_NOPE + ML_V)
    wkn = _slot_cols(ukv[..., :ML_NOPE].reshape(ML_KVR, ML_H * ML_NOPE), ML_NOPE)
    wvm = ukv[..., ML_NOPE:].reshape(ML_KVR, ML_H * ML_V)
    bf = lambda a: a.astype(BF16)
    return bf(w1), bf(_swap_cols(w1)), bf(w2), bf(w3), bf(pad_q(uq)), bf(pad_q(uq_sw)), bf(wkn), bf(wvm)


def _rope_tables(n_lat, n_ctx):
    rows = n_lat // GRID_W
    row = jnp.repeat(jnp.arange(rows, dtype=F32), GRID_W)
    col = jnp.tile(jnp.arange(GRID_W, dtype=F32), rows)
    n_freq = ROT // 4
    inv = ROPE_THETA ** (-jnp.arange(n_freq, dtype=F32) / n_freq)
    ar, ac = row[:, None] * inv, col[:, None] * inv
    c32 = jnp.concatenate([jnp.cos(ar), jnp.cos(ar), jnp.cos(ac), jnp.cos(ac)], axis=1)
    s32 = jnp.concatenate([-jnp.sin(ar), jnp.sin(ar), -jnp.sin(ac), jnp.sin(ac)], axis=1)
    c32 = jnp.concatenate([c32, jnp.ones((n_ctx, ROT), F32)], axis=0)
    s32 = jnp.concatenate([s32, jnp.zeros((n_ctx, ROT), F32)], axis=0)
    t = n_lat + n_ctx
    one, zero = jnp.ones((t, ML_NOPE), F32), jnp.zeros((t, ML_NOPE), F32)
    pad = jnp.zeros((t, LANE - ML_NOPE - ROT), F32)
    return jnp.concatenate([jnp.tile(c32, (1, LANE // ROT)), jnp.tile(s32, (1, LANE // ROT)),
                            one, c32, pad, zero, s32, pad], axis=1)


def kernel(x, c, ctx, c_ctx, w_mod, b_mod, g_mix, g_mlp, w_in, w_out, da_lambda, da_subln_g, mla_g_cq, mla_g_ckv,
           mla_w_uq, mla_w_ukv, hg_lower_bounds, hg_norm_g, w_ff1, w_ff2, g_final):
    bsz, n_lat, d = x.shape
    n_ctx = ctx.shape[1]
    depth = w_in.shape[0]
    t = n_lat + n_ctx

    lb = jax.nn.softmax(hg_lower_bounds.astype(F32), axis=1)
    lb = jnp.cumsum(lb, axis=1) - lb[:, :1]
    cond = jnp.concatenate([c, c_ctx[None]], axis=0)
    mod = _modulation(cond, w_mod, b_mod).reshape(depth, bsz + 1, N_MOD, d)
    rope = _rope_tables(n_lat, n_ctx)

    grp = (jnp.arange(HG_VW)[:, None] // HG_V == jnp.arange(HG_VW)[None, :] // HG_V).astype(BF16)
    e_mat = (jnp.arange(HG_KW)[:, None] // HG_K == jnp.arange(HG_VW)[None, :] // HG_V).astype(BF16)

    tq_da = _pick(n_lat, TQ_DA)
    tq_ml = _pick(n_lat, TQ_ML)
    tk = _pick(t, TK)
    tc = _pick(n_ctx, (256, 128))

    xa = jnp.concatenate([x, ctx], axis=1)
    for l in range(depth):
        last = l == depth - 1
        lam_init = 0.8 - 0.6 * math.exp(-0.3 * l)
        lam = (jnp.exp(jnp.sum(da_lambda[l, 0] * da_lambda[l, 1]))
               - jnp.exp(jnp.sum(da_lambda[l, 2] * da_lambda[l, 3])) + lam_init).astype(F32).reshape(1, 1)
        modl, modc = mod[l, :bsz], mod[l, bsz]
        w1, w1s, w2, w3, wuq, wuqs, wkn, wvm = _layer_weights(w_in[l], mla_w_uq[l], mla_w_ukv[l])
        wts = (w1, w1s, w2, w3, mla_g_cq[l][None], mla_g_ckv[l][None], wuq, wuqs, wkn, wvm)
        qda, kda, vda, qm, km, vm, hg = _project(xa, modl, modc, g_mix[l][None], rope, wts, n_lat)

        g_da = jnp.broadcast_to(da_subln_g[l][:, None], (2 * DA_D, LANE))
        t_out = n_lat if last else t
        da_kw = dict(group=2, diff_scale=1.0 - lam_init, t_out=t_out)
        ml_kw = dict(group=1, diff_scale=1.0, t_out=t_out)
        oda = _flash(qda, kda, vda, lam, g_da, t_q=n_lat, q_off=0, t_kv=t, kv_off=0, tq=tq_da, tk=tk, **da_kw)
        oml = _flash(qm, km, vm, lam, g_da, t_q=n_lat, q_off=0, t_kv=t, kv_off=0, tq=tq_ml, tk=tk, **ml_kw)
        if not last:
            ctx_kw = dict(t_q=n_ctx, q_off=n_lat, t_kv=n_ctx, kv_off=n_lat, tq=tc, tk=tc)
            oda = _flash(qda, kda, vda, lam, g_da, oda, **ctx_kw, **da_kw)
            oml = _flash(qm, km, vm, lam, g_da, oml, **ctx_kw, **ml_kw)

        o_f, o_b = _hgrn(hg, jnp.stack([lb[0, l], lb[1, l]]), e_mat, n_lat)

        ghn = jnp.tile(hg_norm_g[l], HG_H)[None]
        bf = lambda a: a.astype(BF16)
        xa = _mix_mlp(xa, oda, oml, o_f, o_b, hg, modl, modc, ghn, g_mlp[l][None], g_final[None], grp,
                      bf(w_out[l, :DA_W]), bf(w_out[l, DA_W:DA_W + ML_W]), bf(w_out[l, DA_W + ML_W:]),
                      bf(w_ff1[l]), bf(w_ff2[l]), n_lat, last)
    return xa
```

```python
import functools
import math

import jax
import jax.numpy as jnp
from jax import lax
from jax.experimental import pallas as pl
from jax.experimental.pallas import tpu as pltpu

F32 = jnp.float32
BF16 = jnp.bfloat16

EPS = 1e-6
ROPE_THETA = 10000.0
GRID_W = 64
ROT = 32
DA_H, DA_D = 6, 32
DA_W = DA_H * 2 * DA_D
ML_H, ML_QR, ML_KVR, ML_NOPE, ML_V = 6, 256, 128, 64, 64
ML_W = ML_H * ML_V
HG_H, HG_K, HG_V = 4, 128, 64
HG_KW, HG_VW = HG_H * HG_K, HG_H * HG_V
FORGET_FLOOR = 1e-30
N_MOD = 6
LOG2E = 1.4426950408889634

HG_CHUNK = 64
HG_BATCH = (4, 2, 1)
HG_SUB = 8
LANE = 128
VMEM_LIMIT = 56 * 1024 * 1024
TQ_DA = (1024, 512, 256, 128)
TQ_ML = (2048, 1024, 512, 256, 128)
TK = (768, 512, 384, 256, 128)
VT_ROWS = 80
LOOKAHEAD = 6
FLASH_KEYS = 256
FLASH_ROWS = 256


def _cparams(sem):
    return pltpu.CompilerParams(dimension_semantics=sem, vmem_limit_bytes=VMEM_LIMIT)


def _const_spec(shape):
    nd = len(shape)
    return pl.BlockSpec(shape, lambda *_: (0,) * nd, pipeline_mode=pl.Buffered(1))


def _pick(n, cands):
    for c in cands:
        if n % c == 0:
            return c
    raise ValueError(f"no tile for {n} in {cands}")


def _mod_body(c_ref, w_ref, b_ref, o_ref):
    cc = c_ref[...]
    s = cc * jax.nn.sigmoid(cc)
    o_ref[0] = jnp.dot(s.astype(BF16), w_ref[0].astype(BF16), preferred_element_type=F32) + b_ref[0]


def _modulation(cond, w_mod, b_mod):
    depth, d, nm = w_mod.shape
    r = cond.shape[0]
    tn = _pick(nm, (1536, 1024, 512, 128))
    return pl.pallas_call(
        _mod_body,
        grid=(depth, nm // tn),
        in_specs=[pl.BlockSpec((r, d), lambda l, j: (0, 0)),
                  pl.BlockSpec((1, d, tn), lambda l, j: (l, 0, j)),
                  pl.BlockSpec((1, 1, tn), lambda l, j: (l, 0, j))],
        out_specs=pl.BlockSpec((1, r, tn), lambda l, j: (l, 0, j)),
        out_shape=jax.ShapeDtypeStruct((depth, r, nm), F32),
        compiler_params=_cparams(("arbitrary", "arbitrary")),
        name="modulation",
    )(cond, w_mod, b_mod.reshape(depth, 1, nm))


def _ada_rows(x, g, modl, modc, is_ctx, i_shift, i_scale):
    ms = jnp.mean(x * x, axis=-1, keepdims=True)
    y = x * lax.rsqrt(ms + EPS) * g
    shift = jnp.where(is_ctx, modc[i_shift:i_shift + 1], modl[i_shift:i_shift + 1])
    scale = jnp.where(is_ctx, modc[i_scale:i_scale + 1], modl[i_scale:i_scale + 1])
    return y * (1.0 + scale) + shift


def _rms(x, g):
    return x * lax.rsqrt(jnp.mean(x * x, axis=-1, keepdims=True) + EPS) * g


def _proj_body(n_lat, x_ref, modl_ref, modc_ref, g_ref, rope_ref, w1_ref, w1s_ref, w2_ref, w3_ref,
               gcq_ref, gckv_ref, wuq_ref, wuqs_ref, wkn_ref, wvm_ref,
               qda_ref, kda_ref, vda_ref, qm_ref, km_ref, vm_ref, hg_ref):
    tm = x_ref.shape[1]
    rows = pl.program_id(1) * tm + lax.broadcasted_iota(jnp.int32, (tm, 1), 0)
    is_ctx = rows >= n_lat
    h = _ada_rows(x_ref[0], g_ref[...], modl_ref[0], modc_ref[...], is_ctx, 0, 1).astype(BF16)

    rope = rope_ref[...]
    c4, s4, cq_t, sq_t = (rope[:, k * LANE:(k + 1) * LANE] for k in range(4))
    n1 = w1_ref.shape[1] // LANE
    p1 = jnp.dot(h, w1_ref[...], preferred_element_type=F32)
    p1s = jnp.dot(h, w1s_ref[...], preferred_element_type=F32)
    rot = p1 * jnp.concatenate([c4] * n1, axis=1) + p1s * jnp.concatenate([s4] * n1, axis=1)

    rq_t = (rot[:, :DA_W] * (DA_D ** -0.5 * LOG2E)).T
    row64 = lax.broadcasted_iota(jnp.int32, (2 * DA_D, 1), 0)
    for vh in range(2 * DA_H):
        hh, comp = vh // 2, vh % 2
        keep = (row64 >= DA_D) if comp else (row64 < DA_D)
        qda_ref[0, vh] = jnp.where(keep, rq_t[hh * 2 * DA_D:(hh + 1) * 2 * DA_D], 0.0).astype(BF16)
    for hh in range(DA_H):
        kda_ref[0, hh] = rot[:, DA_W + hh * 2 * DA_D:DA_W + (hh + 1) * 2 * DA_D].astype(BF16)
    kr_slot = rot[:, 2 * DA_W:2 * DA_W + LANE]

    ones_rows = (lax.broadcasted_iota(jnp.int32, (VT_ROWS - ML_V, tm), 0) == 0).astype(F32)
    p2 = jnp.dot(h, w2_ref[...], preferred_element_type=F32)
    v_t = p2[:, :DA_W].T
    for hh in range(DA_H):
        vda_ref[0, hh] = jnp.concatenate([v_t[hh * ML_V:(hh + 1) * ML_V], ones_rows], axis=0).astype(BF16)
    cqn = _rms(p2[:, DA_W:DA_W + ML_QR], gcq_ref[...]).astype(BF16)
    ckvn = _rms(p2[:, DA_W + ML_QR:DA_W + ML_QR + ML_KVR], gckv_ref[...]).astype(BF16)

    qu = jnp.dot(cqn, wuq_ref[...], preferred_element_type=F32)
    qus = jnp.dot(cqn, wuqs_ref[...], preferred_element_type=F32)
    qm = (qu * jnp.concatenate([cq_t] * ML_H, axis=1) + qus * jnp.concatenate([sq_t] * ML_H, axis=1))
    qm_ref[0] = (qm * ((ML_NOPE + ROT) ** -0.5 * LOG2E)).T.reshape(ML_H, LANE, tm).astype(BF16)
    kn = jnp.dot(ckvn, wkn_ref[...], preferred_element_type=F32)
    for hh in range(ML_H):
        km_ref[0, hh] = (kn[:, hh * LANE:(hh + 1) * LANE] + kr_slot).astype(BF16)
    vm_t = jnp.dot(ckvn, wvm_ref[...], preferred_element_type=F32).T
    for hh in range(ML_H):
        vm_ref[0, hh] = jnp.concatenate([vm_t[hh * ML_V:(hh + 1) * ML_V], ones_rows], axis=0).astype(BF16)

    hg_ref[0] = jnp.dot(h, w3_ref[...], preferred_element_type=F32)


def _project(xa, modl, modc, g, rope, wts, n_lat):
    b, t, d = xa.shape
    tm = _pick(t, (768, 384, 256, 128))
    w1, w1s, w2, w3, gcq, gckv, wuq, wuqs, wkn, wvm = wts
    nhg = w3.shape[1]
    rows = lambda hs, w: (jax.ShapeDtypeStruct((b, hs, t, w), BF16),
                          pl.BlockSpec((1, hs, tm, w), lambda bi, i: (bi, 0, i, 0)))
    cols = lambda hs, r: (jax.ShapeDtypeStruct((b, hs, r, t), BF16),
                          pl.BlockSpec((1, hs, r, tm), lambda bi, i: (bi, 0, 0, i)))
    outs = (cols(2 * DA_H, 2 * DA_D), rows(DA_H, 2 * DA_D), cols(DA_H, VT_ROWS),
            cols(ML_H, LANE), rows(ML_H, LANE), cols(ML_H, VT_ROWS),
            (jax.ShapeDtypeStruct((b, t, nhg), F32), pl.BlockSpec((1, tm, nhg), lambda bi, i: (bi, i, 0))))
    out_shapes = tuple(o[0] for o in outs)
    out_specs = tuple(o[1] for o in outs)
    in_specs = [
        pl.BlockSpec((1, tm, d), lambda bi, i: (bi, i, 0)),
        pl.BlockSpec((1, N_MOD, d), lambda bi, i: (bi, 0, 0)),
        _const_spec(modc.shape), _const_spec(g.shape),
        pl.BlockSpec((tm, rope.shape[1]), lambda bi, i: (i, 0)),
    ] + [_const_spec(w.shape) for w in wts]
    return pl.pallas_call(
        functools.partial(_proj_body, n_lat),
        grid=(b, t // tm),
        in_specs=in_specs, out_specs=out_specs, out_shape=out_shapes,
        compiler_params=_cparams(("arbitrary", "arbitrary")),
        name="project",
    )(xa, modl, modc, g, rope, *wts)


def _flash_body(n_kv, group, diff_scale, rc, lam_ref, g_ref, qt_ref, k_ref, vt_ref, *refs):
    o_ref, m_ref, acc_ref = refs[-3:]
    ki = pl.program_id(2)
    nk = pl.num_programs(2)
    tq = qt_ref.shape[3]

    @pl.when(ki == 0)
    def _():
        m_ref[...] = jnp.full(m_ref.shape, -jnp.inf, F32)
        acc_ref[...] = jnp.zeros(acc_ref.shape, F32)

    tk = k_ref.shape[2]
    ks = min(FLASH_KEYS, tk)
    chains = [(hh, c0, k0) for hh in range(n_kv) for c0 in range(0, group * tq, rc) for k0 in range(0, tk, ks)]

    def scores(hh, c0, k0):
        qt = qt_ref[0, hh * group + c0 // tq, :, c0 % tq:c0 % tq + rc]
        return jnp.dot(k_ref[0, hh, k0:k0 + ks, :], qt, preferred_element_type=F32)

    def softmax(hh, c0, k0, s):
        cols = slice(c0, c0 + rc)
        m_prev = m_ref[hh, :, cols]
        m_new = jnp.maximum(m_prev, jnp.max(s, axis=0, keepdims=True))
        m_ref[hh, :, cols] = m_new
        return jnp.exp2(s - m_new[0:1]).astype(BF16), jnp.exp2(m_prev - m_new)[0:1]

    def accumulate(hh, c0, k0, p, alpha):
        cols = slice(c0, c0 + rc)
        acc_ref[hh, :, cols] = alpha * acc_ref[hh, :, cols] + jnp.dot(
            vt_ref[0, hh, :, k0:k0 + ks], p, preferred_element_type=F32)

    n = len(chains)
    s_q = {i: scores(*chains[i]) for i in range(min(LOOKAHEAD, n))}
    p_q = {0: softmax(*chains[0], s_q.pop(0))}
    for ci in range(n):
        if ci + LOOKAHEAD < n:
            s_q[ci + LOOKAHEAD] = scores(*chains[ci + LOOKAHEAD])
        if ci + 1 < n:
            p_q[ci + 1] = softmax(*chains[ci + 1], s_q.pop(ci + 1))
        accumulate(*chains[ci], *p_q.pop(ci))

    @pl.when(ki == nk - 1)
    def _():
        outs = []
        for hh in range(n_kv):
            acc = acc_ref[hh]
            o = acc[:ML_V] / acc[ML_V:ML_V + 1]
            if group == 2:
                od = o[:, :tq] - lam_ref[0, 0] * o[:, tq:]
                ms = jnp.mean(od * od, axis=0, keepdims=True)
                o = od * lax.rsqrt(ms + EPS) * jnp.concatenate([g_ref[...]] * (tq // LANE), axis=1) * diff_scale
            outs.append(o.T)
        o_ref[0] = jnp.concatenate(outs, axis=-1).astype(o_ref.dtype)


def _flash(qt, k, vt, lam, g, into=None, *, group, diff_scale, t_q, q_off, t_kv, kv_off, tq, tk, t_out):
    b, hq, dk, _ = qt.shape
    n_kv = k.shape[1]
    assert hq == n_kv * group and q_off % tq == 0 and kv_off % tk == 0
    qo, ko = q_off // tq, kv_off // tk
    extra = () if into is None else (into,)
    return pl.pallas_call(
        functools.partial(_flash_body, n_kv, group, diff_scale, min(FLASH_ROWS, tq)),
        grid=(b, t_q // tq, t_kv // tk),
        in_specs=[
            pl.BlockSpec(memory_space=pltpu.SMEM),
            _const_spec(g.shape),
            pl.BlockSpec((1, hq, dk, tq), lambda bi, qi, ki: (bi, 0, 0, qi + qo)),
            pl.BlockSpec((1, n_kv, tk, dk), lambda bi, qi, ki: (bi, 0, ki + ko, 0)),
            pl.BlockSpec((1, n_kv, VT_ROWS, tk), lambda bi, qi, ki: (bi, 0, 0, ki + ko)),
        ] + [pl.BlockSpec(memory_space=pl.ANY)] * len(extra),
        out_specs=pl.BlockSpec((1, tq, n_kv * ML_V), lambda bi, qi, ki: (bi, qi + qo, 0)),
        out_shape=jax.ShapeDtypeStruct((b, t_out, n_kv * ML_V), BF16),
        input_output_aliases={5: 0} if extra else {},
        scratch_shapes=[pltpu.VMEM((n_kv, 8, group * tq), F32),
                        pltpu.VMEM((n_kv, VT_ROWS, group * tq), F32)],
        compiler_params=_cparams(("arbitrary", "arbitrary", "arbitrary")),
        name="flash_da" if group == 2 else "flash_mla",
    )(lam, g, qt, k, vt, *extra)


def _cumsum_rows(x):
    n, w = x.shape
    hi = x.astype(BF16)
    r1 = x - hi.astype(F32)
    mid = r1.astype(BF16)
    lo = (r1 - mid.astype(F32)).astype(BF16)
    tril = (lax.broadcasted_iota(jnp.int32, (n, n), 0) >= lax.broadcasted_iota(jnp.int32, (n, n), 1)).astype(BF16)
    parts = jnp.dot(tril, jnp.concatenate([hi, mid, lo], axis=1), preferred_element_type=F32)
    return parts[:, :w] + parts[:, w:2 * w] + parts[:, 2 * w:]


def _dot_nt(a, b):
    return lax.dot_general(a, b, (((1,), (1,)), ((), ())), preferred_element_type=F32)


def _pad_rows(x, start, total):
    parts = []
    if start:
        parts.append(jnp.zeros((start, x.shape[1]), x.dtype))
    parts.append(x)
    if total - start - x.shape[0]:
        parts.append(jnp.zeros((total - start - x.shape[0], x.shape[1]), x.dtype))
    return jnp.concatenate(parts, axis=0) if len(parts) > 1 else x


def _hgrn_body(nb, qf_ref, zf_ref, vf_ref, qb_ref, zb_ref, vb_ref, lb_ref, e_ref, of_ref, ob_ref, stf_ref, stb_ref):
    @pl.when(pl.program_id(1) == 0)
    def _():
        stf_ref[...] = jnp.zeros(stf_ref.shape, F32)
        stb_ref[...] = jnp.zeros(stb_ref.shape, F32)

    c, nsub, kw = HG_CHUNK, HG_CHUNK // HG_SUB, HG_KW
    streams = [(bi, d) for bi in range(nb) for d in range(2)]
    n = len(streams)
    qs = [(qb_ref if d else qf_ref)[bi] for bi, d in streams]
    vs = [(vb_ref if d else vf_ref)[bi] for bi, d in streams]
    st_refs = [(stb_ref if d else stf_ref) for bi, d in streams]
    sts = [st_refs[i][bi] for i, (bi, d) in enumerate(streams)]

    z = jnp.concatenate([(zb_ref if d else zf_ref)[bi] for bi, d in streams], axis=1)
    lb = jnp.concatenate([lb_ref[d:d + 1] for bi, d in streams], axis=1)
    f = lb + (1.0 - lb) * jax.nn.sigmoid(z)
    kk_all = (1.0 - lb) * jax.nn.sigmoid(-z)
    logf = jnp.log(jnp.maximum(f, FORGET_FLOOR))
    bf = _cumsum_rows(logf)
    kks, bs, b_ends = [], [], []
    for i, (bi, d) in enumerate(streams):
        sl = slice(i * kw, (i + 1) * kw)
        kks.append(kk_all[:, sl])
        b = (bf[c - 1:c, sl] - bf[:, sl] + logf[:, sl]) * LOG2E if d else bf[:, sl] * LOG2E
        bs.append(b)
        b_ends.append(b[0:1] if d else b[c - 1:c])

    def block_diag(st):
        st = st.astype(BF16)
        zero = jnp.zeros((HG_V, HG_K), BF16)
        return jnp.concatenate([jnp.concatenate(
            [st[:, hh * HG_K:(hh + 1) * HG_K] if j == hh else zero for j in range(HG_H)], axis=1)
            for hh in range(HG_H)], axis=0)

    outs = [_dot_nt((qs[i] * jnp.exp2(bs[i])).astype(BF16), block_diag(sts[i])) for i in range(n)]

    head_r = lax.broadcasted_iota(jnp.int32, (HG_H * c, 1), 0) // c
    same_blk = head_r == lax.broadcasted_iota(jnp.int32, (1, HG_H * c), 1) // c
    v_keep = head_r == lax.broadcasted_iota(jnp.int32, (1, HG_VW), 1) // HG_V
    a_mats = []
    for i, (bi, d) in enumerate(streams):
        q, kk, b = qs[i], kks[i], bs[i]
        q_slots, k_slots = [], []
        for j in (range(nsub - 1) if d else range(1, nsub)):
            r0 = j * HG_SUB
            if d:
                m, k0, k1 = b[r0 + HG_SUB:r0 + HG_SUB + 1], r0 + HG_SUB, c
            else:
                m, k0, k1 = b[r0 - 1:r0], 0, r0
            q_slots.append(_pad_rows(q[r0:r0 + HG_SUB] * jnp.exp2(b[r0:r0 + HG_SUB] - m), r0, c).astype(BF16))
            k_slots.append(_pad_rows(kk[k0:k1] * jnp.exp2(m - b[k0:k1]), k0, c).astype(BF16))
        stack = lambda slots: jnp.concatenate(
            [jnp.concatenate([x[:, hh * HG_K:(hh + 1) * HG_K] for x in slots], axis=1) for hh in range(HG_H)], axis=0)
        a_mats.append(_dot_nt(stack(q_slots), stack(k_slots)))

    terms = []
    for i in range(n):
        q, kk, b = qs[i], kks[i], bs[i]
        for j in range(nsub):
            r0 = j * HG_SUB
            qb, bb, kb = q[r0:r0 + HG_SUB], b[r0:r0 + HG_SUB], kk[r0:r0 + HG_SUB]
            terms += [(qb * jnp.exp2(bb - bb[s:s + 1]) * kb[s:s + 1]).astype(BF16) for s in range(HG_SUB)]
    a_bc = jnp.dot(jnp.concatenate(terms, axis=0), e_ref[...], preferred_element_type=F32)

    for i in range(n):
        v_stack = jnp.where(v_keep, jnp.concatenate([vs[i]] * HG_H, axis=0), 0.0).astype(BF16)
        o_stack = jnp.dot(jnp.where(same_blk, a_mats[i], 0.0).astype(BF16), v_stack, preferred_element_type=F32)
        for hh in range(HG_H):
            outs[i] = outs[i] + o_stack[hh * c:(hh + 1) * c]

    pair = lax.broadcasted_iota(jnp.int32, (HG_SUB * HG_SUB, 1), 0)
    keeps = [jnp.broadcast_to((pair % HG_SUB <= pair // HG_SUB) if d else (pair % HG_SUB >= pair // HG_SUB),
                              (HG_SUB * HG_SUB, HG_VW)) for d in range(2)]
    for i, (bi, d) in enumerate(streams):
        diag = []
        for j in range(nsub):
            r0 = j * HG_SUB
            base = (i * nsub + j) * HG_SUB * HG_SUB
            blk = jnp.where(keeps[d], a_bc[base:base + HG_SUB * HG_SUB], 0.0)
            vb = vs[i][r0:r0 + HG_SUB]
            od = blk[0:HG_SUB] * vb[0:1]
            for s in range(1, HG_SUB):
                od = od + blk[s * HG_SUB:(s + 1) * HG_SUB] * vb[s:s + 1]
            diag.append(od)
        (ob_ref if d else of_ref)[bi] = outs[i] + jnp.concatenate(diag, axis=0)

    for i, (bi, d) in enumerate(streams):
        kd = kks[i] * jnp.exp2(b_ends[i] - bs[i])
        upd = jnp.dot(vs[i].T.astype(BF16), kd.astype(BF16), preferred_element_type=F32)
        upd = jnp.concatenate([upd[hh * HG_V:(hh + 1) * HG_V, hh * HG_K:(hh + 1) * HG_K] for hh in range(HG_H)], axis=1)
        st_refs[i][bi] = sts[i] * jnp.exp2(b_ends[i]) + upd


def _hgrn(hg, lb2, e_mat, n_lat):
    b, t, _ = hg.shape
    c = HG_CHUNK
    nb = _pick(b, HG_BATCH)
    nc, nlc = t // c, n_lat // c
    fwd = lambda i: (i + nlc) % nc
    bwd = lambda i: nc - 1 - i
    v_blk = 3 * HG_KW // HG_VW
    return pl.pallas_call(
        functools.partial(_hgrn_body, nb),
        grid=(b // nb, nc),
        in_specs=[
            pl.BlockSpec((nb, c, HG_KW), lambda bi, i: (bi, fwd(i), 0)),
            pl.BlockSpec((nb, c, HG_KW), lambda bi, i: (bi, fwd(i), 1)),
            pl.BlockSpec((nb, c, HG_VW), lambda bi, i: (bi, fwd(i), v_blk)),
            pl.BlockSpec((nb, c, HG_KW), lambda bi, i: (bi, bwd(i), 0)),
            pl.BlockSpec((nb, c, HG_KW), lambda bi, i: (bi, bwd(i), 2)),
            pl.BlockSpec((nb, c, HG_VW), lambda bi, i: (bi, bwd(i), v_blk)),
            _const_spec(lb2.shape), _const_spec(e_mat.shape),
        ],
        out_specs=(pl.BlockSpec((nb, c, HG_VW), lambda bi, i: (bi, fwd(i), 0)),
                   pl.BlockSpec((nb, c, HG_VW), lambda bi, i: (bi, bwd(i), 0))),
        out_shape=(jax.ShapeDtypeStruct((b, t, HG_VW), F32), jax.ShapeDtypeStruct((b, t, HG_VW), F32)),
        scratch_shapes=[pltpu.VMEM((nb, HG_V, HG_KW), F32), pltpu.VMEM((nb, HG_V, HG_KW), F32)],
        compiler_params=_cparams(("arbitrary", "arbitrary")),
        name="hgrn2",
    )(hg, hg, hg, hg, hg, hg, lb2, e_mat)


def _mix_mlp_body(n_lat, ff_chunk, final, x_ref, oda_ref, oml_ref, of_ref, ob_ref, gate_ref, modl_ref, modc_ref,
                  ghn_ref, gmlp_ref, gfin_ref, grp_ref, wda_ref, wml_ref, whg_ref, w1_ref, w2_ref, o_ref):
    tm = x_ref.shape[1]
    rows = pl.program_id(1) * tm + lax.broadcasted_iota(jnp.int32, (tm, 1), 0)
    is_ctx = rows >= n_lat
    modl, modc = modl_ref[0], modc_ref[...]

    oh = of_ref[0] + ob_ref[0]
    sq = oh * oh
    sq_hi = sq.astype(BF16)
    sq_lo = (sq - sq_hi.astype(F32)).astype(BF16)
    ms = (jnp.dot(sq_hi, grp_ref[...], preferred_element_type=F32)
          + jnp.dot(sq_lo, grp_ref[...], preferred_element_type=F32)) * (1.0 / HG_V)
    gt = gate_ref[0]
    yh = oh * lax.rsqrt(ms + EPS) * ghn_ref[...] * (gt * jax.nn.sigmoid(gt))

    mix = (jnp.dot(oda_ref[0], wda_ref[...], preferred_element_type=F32)
           + jnp.dot(oml_ref[0], wml_ref[...], preferred_element_type=F32)
           + jnp.dot(yh.astype(BF16), whg_ref[...], preferred_element_type=F32))
    g_mix = jnp.where(is_ctx, modc[2:3], modl[2:3])
    x1 = x_ref[0] + g_mix * mix

    hn = _ada_rows(x1, gmlp_ref[...], modl, modc, is_ctx, 3, 4).astype(BF16)
    dff = w1_ref.shape[1]
    acc = jnp.zeros(x1.shape, F32)
    for j in range(dff // ff_chunk):
        u = jnp.dot(hn, w1_ref[:, j * ff_chunk:(j + 1) * ff_chunk], preferred_element_type=F32)
        u = jnp.maximum(u, 0.0)
        acc = acc + jnp.dot((u * u).astype(BF16), w2_ref[j * ff_chunk:(j + 1) * ff_chunk, :],
                            preferred_element_type=F32)
    g_mlp = jnp.where(is_ctx, modc[5:6], modl[5:6])
    x2 = x1 + g_mlp * acc
    if final:
        x2 = _rms(x2, gfin_ref[...])
    o_ref[0] = x2


def _mix_mlp(xa, oda, oml, o_f, o_b, hg, modl, modc, ghn, gmlp, gfin, grp, wda, wml, whg, w1, w2, n_lat, final):
    b, t, d = xa.shape
    t_out = n_lat if final else t
    tm = _pick(t_out, (512, 384, 256, 128)) if final else _pick(t, (384, 256, 128))
    gate_blk = (3 * HG_KW + HG_VW) // HG_VW
    row = lambda w: pl.BlockSpec((1, tm, w), lambda bi, i: (bi, i, 0))
    consts = (modc, ghn, gmlp, gfin, grp, wda, wml, whg, w1, w2)
    return pl.pallas_call(
        functools.partial(_mix_mlp_body, n_lat, 1024, final),
        grid=(b, t_out // tm),
        in_specs=[row(d), row(DA_W), row(ML_W), row(HG_VW), row(HG_VW),
                  pl.BlockSpec((1, tm, HG_VW), lambda bi, i: (bi, i, gate_blk)),
                  pl.BlockSpec((1, N_MOD, d), lambda bi, i: (bi, 0, 0))]
                 + [_const_spec(a.shape) for a in consts],
        out_specs=row(d),
        out_shape=jax.ShapeDtypeStruct((b, t_out, d), F32),
        compiler_params=_cparams(("arbitrary", "arbitrary")),
        name="mix_mlp",
    )(xa, oda, oml, o_f, o_b, hg, modl, *consts)


def _swap_cols(w):
    n = w.shape[-1]
    j = jnp.arange(n)
    src = (j // 16) * 16 + (j % 16 + 8) % 16
    return w[..., src]


def _slot_cols(w, width):
    k, n = w.shape
    hs = n // width
    return jnp.pad(w.reshape(k, hs, width), ((0, 0), (0, 0), (0, LANE - width))).reshape(k, hs * LANE)


def _layer_weights(w_in, w_uq, w_ukv):
    o = 0
    parts = {}
    for name, wdt in (("da_q", DA_W), ("da_k", DA_W), ("da_v", DA_W), ("cq", ML_QR), ("ckv", ML_KVR), ("kr", ROT),
                      ("hq", HG_KW), ("hzf", HG_KW), ("hzb", HG_KW), ("hv", HG_VW), ("hgate", HG_VW)):
        parts[name] = w_in[:, o:o + wdt]
        o += wdt
    d = w_in.shape[0]
    w1 = jnp.concatenate([parts["da_q"], parts["da_k"], jnp.zeros((d, ML_NOPE), F32), parts["kr"],
                          jnp.zeros((d, LANE - ML_NOPE - ROT), F32)], axis=1)
    w2 = jnp.concatenate([parts["da_v"], parts["cq"], parts["ckv"]], axis=1)
    w3 = jnp.concatenate([parts[n] for n in ("hq", "hzf", "hzb", "hv", "hgate")], axis=1)
    uq = w_uq.reshape(ML_QR, ML_H, ML_NOPE + ROT)
    uq_sw = jnp.concatenate([jnp.zeros((ML_QR, ML_H, ML_NOPE), F32), _swap_cols(uq[..., ML_NOPE:])], axis=-1)
    pad_q = lambda a: jnp.pad(a, ((0, 0), (0, 0), (0, LANE - ML_NOPE - ROT))).reshape(ML_QR, ML_H * LANE)
    ukv = w_ukv.reshape(ML_KVR, ML_H, ML_NOPE + ML_V)
    wkn = _slot_cols(ukv[..., :ML_NOPE].reshape(ML_KVR, ML_H * ML_NOPE), ML_NOPE)
    wvm = ukv[..., ML_NOPE:].reshape(ML_KVR, ML_H * ML_V)
    bf = lambda a: a.astype(BF16)
    return bf(w1), bf(_swap_cols(w1)), bf(w2), bf(w3), bf(pad_q(uq)), bf(pad_q(uq_sw)), bf(wkn), bf(wvm)


def _rope_tables(n_lat, n_ctx):
    rows = n_lat // GRID_W
    row = jnp.repeat(jnp.arange(rows, dtype=F32), GRID_W)
    col = jnp.tile(jnp.arange(GRID_W, dtype=F32), rows)
    n_freq = ROT // 4
    inv = ROPE_THETA ** (-jnp.arange(n_freq, dtype=F32) / n_freq)
    ar, ac = row[:, None] * inv, col[:, None] * inv
    c32 = jnp.concatenate([jnp.cos(ar), jnp.cos(ar), jnp.cos(ac), jnp.cos(ac)], axis=1)
    s32 = jnp.concatenate([-jnp.sin(ar), jnp.sin(ar), -jnp.sin(ac), jnp.sin(ac)], axis=1)
    c32 = jnp.concatenate([c32, jnp.ones((n_ctx, ROT), F32)], axis=0)
    s32 = jnp.concatenate([s32, jnp.zeros((n_ctx, ROT), F32)], axis=0)
    t = n_lat + n_ctx
    one, zero = jnp.ones((t, ML_NOPE), F32), jnp.zeros((t, ML_NOPE), F32)
    pad = jnp.zeros((t, LANE - ML_NOPE - ROT), F32)
    return jnp.concatenate([jnp.tile(c32, (1, LANE // ROT)), jnp.tile(s32, (1, LANE // ROT)),
                            one, c32, pad, zero, s32, pad], axis=1)


def kernel(x, c, ctx, c_ctx, w_mod, b_mod, g_mix, g_mlp, w_in, w_out, da_lambda, da_subln_g, mla_g_cq, mla_g_ckv,
           mla_w_uq, mla_w_ukv, hg_lower_bounds, hg_norm_g, w_ff1, w_ff2, g_final):
    bsz, n_lat, d = x.shape
    n_ctx = ctx.shape[1]
    depth = w_in.shape[0]
    t = n_lat + n_ctx

    lb = jax.nn.softmax(hg_lower_bounds.astype(F32), axis=1)
    lb = jnp.cumsum(lb, axis=1) - lb[:, :1]
    cond = jnp.concatenate([c, c_ctx[None]], axis=0)
    mod = _modulation(cond, w_mod, b_mod).reshape(depth, bsz + 1, N_MOD, d)
    rope = _rope_tables(n_lat, n_ctx)

    grp = (jnp.arange(HG_VW)[:, None] // HG_V == jnp.arange(HG_VW)[None, :] // HG_V).astype(BF16)
    e_mat = (jnp.arange(HG_KW)[:, None] // HG_K == jnp.arange(HG_VW)[None, :] // HG_V).astype(BF16)

    tq_da = _pick(n_lat, TQ_DA)
    tq_ml = _pick(n_lat, TQ_ML)
    tk = _pick(t, TK)
    tc = _pick(n_ctx, (256, 128))

    xa = jnp.concatenate([x, ctx], axis=1)
    oda = jnp.zeros((bsz, t, DA_W), BF16)
    oml = jnp.zeros((bsz, t, ML_W), BF16)
    for l in range(depth):
        last = l == depth - 1
        lam_init = 0.8 - 0.6 * math.exp(-0.3 * l)
        lam = (jnp.exp(jnp.sum(da_lambda[l, 0] * da_lambda[l, 1]))
               - jnp.exp(jnp.sum(da_lambda[l, 2] * da_lambda[l, 3])) + lam_init).astype(F32).reshape(1, 1)
        modl, modc = mod[l, :bsz], mod[l, bsz]
        w1, w1s, w2, w3, wuq, wuqs, wkn, wvm = _layer_weights(w_in[l], mla_w_uq[l], mla_w_ukv[l])
        wts = (w1, w1s, w2, w3, mla_g_cq[l][None], mla_g_ckv[l][None], wuq, wuqs, wkn, wvm)
        qda, kda, vda, qm, km, vm, hg = _project(xa, modl, modc, g_mix[l][None], rope, wts, n_lat)

        g_da = jnp.broadcast_to(da_subln_g[l][:, None], (2 * DA_D, LANE))
        t_out = n_lat if last else t
        da_kw = dict(group=2, diff_scale=1.0 - lam_init, t_out=t_out)
        ml_kw = dict(group=1, diff_scale=1.0, t_out=t_out)
        lat_kw = dict(t_q=n_lat, q_off=0, t_kv=t, kv_off=0, tk=tk)
        oda = _flash(qda, kda, vda, lam, g_da, None if last else oda, tq=tq_da, **lat_kw, **da_kw)
        oml = _flash(qm, km, vm, lam, g_da, None if last else oml, tq=tq_ml, **lat_kw, **ml_kw)
        if not last:
            ctx_kw = dict(t_q=n_ctx, q_off=n_lat, t_kv=n_ctx, kv_off=n_lat, tq=tc, tk=tc)
            oda = _flash(qda, kda, vda, lam, g_da, oda, **ctx_kw, **da_kw)
            oml = _flash(qm, km, vm, lam, g_da, oml, **ctx_kw, **ml_kw)

        o_f, o_b = _hgrn(hg, jnp.stack([lb[0, l], lb[1, l]]), e_mat, n_lat)

        ghn = jnp.tile(hg_norm_g[l], HG_H)[None]
        bf = lambda a: a.astype(BF16)
        xa = _mix_mlp(xa, oda, oml, o_f, o_b, hg, modl, modc, ghn, g_mlp[l][None], g_final[None], grp,
                      bf(w_out[l, :DA_W]), bf(w_out[l, DA_W:DA_W + ML_W]), bf(w_out[l, DA_W + ML_W:]),
                      bf(w_ff1[l]), bf(w_ff2[l]), n_lat, last)
    return xa
```

```python
import functools
import math

import jax
import jax.numpy as jnp
from jax import lax
from jax.experimental import pallas as pl
from jax.experimental.pallas import tpu as pltpu

F32 = jnp.float32
BF16 = jnp.bfloat16

EPS = 1e-6
ROPE_THETA = 10000.0
GRID_W = 64
ROT = 32
DA_H, DA_D = 6, 32
DA_W = DA_H * 2 * DA_D
ML_H, ML_QR, ML_KVR, ML_NOPE, ML_V = 6, 256, 128, 64, 64
ML_W = ML_H * ML_V
HG_H, HG_K, HG_V = 4, 128, 64
HG_KW, HG_VW = HG_H * HG_K, HG_H * HG_V
FORGET_FLOOR = 1e-30
N_MOD = 6
LOG2E = 1.4426950408889634

HG_CHUNK = 64
HG_BATCH = (4, 2, 1)
HG_SUB = 8
LANE = 128
VMEM_LIMIT = 56 * 1024 * 1024
TQ_DA = (1024, 512, 256, 128)
TQ_ML = (2048, 1024, 512, 256, 128)
TK = (768, 512, 384, 256, 128)
VT_ROWS = 80
LOOKAHEAD = 6
FLASH_KEYS = 256
FLASH_ROWS = 256


def _cparams(sem):
    return pltpu.CompilerParams(dimension_semantics=sem, vmem_limit_bytes=VMEM_LIMIT)


def _const_spec(shape):
    nd = len(shape)
    return pl.BlockSpec(shape, lambda *_: (0,) * nd, pipeline_mode=pl.Buffered(1))


def _pick(n, cands):
    for c in cands:
        if n % c == 0:
            return c
    raise ValueError(f"no tile for {n} in {cands}")


def _mod_body(c_ref, w_ref, b_ref, o_ref):
    cc = c_ref[...]
    s = cc * jax.nn.sigmoid(cc)
    o_ref[0] = jnp.dot(s.astype(BF16), w_ref[0].astype(BF16), preferred_element_type=F32) + b_ref[0]


def _modulation(cond, w_mod, b_mod):
    depth, d, nm = w_mod.shape
    r = cond.shape[0]
    tn = _pick(nm, (1536, 1024, 512, 128))
    return pl.pallas_call(
        _mod_body,
        grid=(depth, nm // tn),
        in_specs=[pl.BlockSpec((r, d), lambda l, j: (0, 0)),
                  pl.BlockSpec((1, d, tn), lambda l, j: (l, 0, j)),
                  pl.BlockSpec((1, 1, tn), lambda l, j: (l, 0, j))],
        out_specs=pl.BlockSpec((1, r, tn), lambda l, j: (l, 0, j)),
        out_shape=jax.ShapeDtypeStruct((depth, r, nm), F32),
        compiler_params=_cparams(("arbitrary", "arbitrary")),
        name="modulation",
    )(cond, w_mod, b_mod.reshape(depth, 1, nm))


def _ada_rows(x, g, modl, modc, is_ctx, i_shift, i_scale):
    ms = jnp.mean(x * x, axis=-1, keepdims=True)
    y = x * lax.rsqrt(ms + EPS) * g
    shift = jnp.where(is_ctx, modc[i_shift:i_shift + 1], modl[i_shift:i_shift + 1])
    scale = jnp.where(is_ctx, modc[i_scale:i_scale + 1], modl[i_scale:i_scale + 1])
    return y * (1.0 + scale) + shift


def _rms(x, g):
    return x * lax.rsqrt(jnp.mean(x * x, axis=-1, keepdims=True) + EPS) * g


def _proj_body(n_lat, x_ref, modl_ref, modc_ref, g_ref, rope_ref, w1_ref, w1s_ref, w2_ref, w3_ref,
               gcq_ref, gckv_ref, wuq_ref, wuqs_ref, wkn_ref, wvm_ref,
               qda_ref, kda_ref, vda_ref, qm_ref, km_ref, vm_ref, hg_ref):
    tm = x_ref.shape[1]
    rows = pl.program_id(1) * tm + lax.broadcasted_iota(jnp.int32, (tm, 1), 0)
    is_ctx = rows >= n_lat
    h = _ada_rows(x_ref[0], g_ref[...], modl_ref[0], modc_ref[...], is_ctx, 0, 1).astype(BF16)

    rope = rope_ref[...]
    c4, s4, cq_t, sq_t = (rope[:, k * LANE:(k + 1) * LANE] for k in range(4))
    n1 = w1_ref.shape[1] // LANE
    p1 = jnp.dot(h, w1_ref[...], preferred_element_type=F32)
    p1s = jnp.dot(h, w1s_ref[...], preferred_element_type=F32)
    rot = p1 * jnp.concatenate([c4] * n1, axis=1) + p1s * jnp.concatenate([s4] * n1, axis=1)

    rq_t = (rot[:, :DA_W] * (DA_D ** -0.5 * LOG2E)).T
    row64 = lax.broadcasted_iota(jnp.int32, (2 * DA_D, 1), 0)
    for vh in range(2 * DA_H):
        hh, comp = vh // 2, vh % 2
        keep = (row64 >= DA_D) if comp else (row64 < DA_D)
        qda_ref[0, vh] = jnp.where(keep, rq_t[hh * 2 * DA_D:(hh + 1) * 2 * DA_D], 0.0).astype(BF16)
    for hh in range(DA_H):
        kda_ref[0, hh] = rot[:, DA_W + hh * 2 * DA_D:DA_W + (hh + 1) * 2 * DA_D].astype(BF16)
    kr_slot = rot[:, 2 * DA_W:2 * DA_W + LANE]

    ones_rows = (lax.broadcasted_iota(jnp.int32, (VT_ROWS - ML_V, tm), 0) == 0).astype(F32)
    p2 = jnp.dot(h, w2_ref[...], preferred_element_type=F32)
    v_t = p2[:, :DA_W].T
    for hh in range(DA_H):
        vda_ref[0, hh] = jnp.concatenate([v_t[hh * ML_V:(hh + 1) * ML_V], ones_rows], axis=0).astype(BF16)
    cqn = _rms(p2[:, DA_W:DA_W + ML_QR], gcq_ref[...]).astype(BF16)
    ckvn = _rms(p2[:, DA_W + ML_QR:DA_W + ML_QR + ML_KVR], gckv_ref[...]).astype(BF16)

    qu = jnp.dot(cqn, wuq_ref[...], preferred_element_type=F32)
    qus = jnp.dot(cqn, wuqs_ref[...], preferred_element_type=F32)
    qm = (qu * jnp.concatenate([cq_t] * ML_H, axis=1) + qus * jnp.concatenate([sq_t] * ML_H, axis=1))
    qm_ref[0] = (qm * ((ML_NOPE + ROT) ** -0.5 * LOG2E)).T.reshape(ML_H, LANE, tm).astype(BF16)
    kn = jnp.dot(ckvn, wkn_ref[...], preferred_element_type=F32)
    for hh in range(ML_H):
        km_ref[0, hh] = (kn[:, hh * LANE:(hh + 1) * LANE] + kr_slot).astype(BF16)
    vm_t = jnp.dot(ckvn, wvm_ref[...], preferred_element_type=F32).T
    for hh in range(ML_H):
        vm_ref[0, hh] = jnp.concatenate([vm_t[hh * ML_V:(hh + 1) * ML_V], ones_rows], axis=0).astype(BF16)

    hg_ref[0] = jnp.dot(h, w3_ref[...], preferred_element_type=F32)


def _project(xa, modl, modc, g, rope, wts, n_lat):
    b, t, d = xa.shape
    tm = _pick(t, (768, 384, 256, 128))
    w1, w1s, w2, w3, gcq, gckv, wuq, wuqs, wkn, wvm = wts
    nhg = w3.shape[1]
    rows = lambda hs, w: (jax.ShapeDtypeStruct((b, hs, t, w), BF16),
                          pl.BlockSpec((1, hs, tm, w), lambda bi, i: (bi, 0, i, 0)))
    cols = lambda hs, r: (jax.ShapeDtypeStruct((b, hs, r, t), BF16),
                          pl.BlockSpec((1, hs, r, tm), lambda bi, i: (bi, 0, 0, i)))
    outs = (cols(2 * DA_H, 2 * DA_D), rows(DA_H, 2 * DA_D), cols(DA_H, VT_ROWS),
            cols(ML_H, LANE), rows(ML_H, LANE), cols(ML_H, VT_ROWS),
            (jax.ShapeDtypeStruct((b, t, nhg), F32), pl.BlockSpec((1, tm, nhg), lambda bi, i: (bi, i, 0))))
    out_shapes = tuple(o[0] for o in outs)
    out_specs = tuple(o[1] for o in outs)
    in_specs = [
        pl.BlockSpec((1, tm, d), lambda bi, i: (bi, i, 0)),
        pl.BlockSpec((1, N_MOD, d), lambda bi, i: (bi, 0, 0)),
        _const_spec(modc.shape), _const_spec(g.shape),
        pl.BlockSpec((tm, rope.shape[1]), lambda bi, i: (i, 0)),
    ] + [_const_spec(w.shape) for w in wts]
    return pl.pallas_call(
        functools.partial(_proj_body, n_lat),
        grid=(b, t // tm),
        in_specs=in_specs, out_specs=out_specs, out_shape=out_shapes,
        compiler_params=_cparams(("arbitrary", "arbitrary")),
        name="project",
    )(xa, modl, modc, g, rope, *wts)


def _flash_body(n_kv, group, diff_scale, rc, lam_ref, g_ref, qt_ref, k_ref, vt_ref, *refs):
    o_ref, m_ref, acc_ref = refs[-3:]
    ki = pl.program_id(2)
    nk = pl.num_programs(2)
    tq = qt_ref.shape[3]

    @pl.when(ki == 0)
    def _():
        m_ref[...] = jnp.full(m_ref.shape, -jnp.inf, F32)
        acc_ref[...] = jnp.zeros(acc_ref.shape, F32)

    tk = k_ref.shape[2]
    ks = min(FLASH_KEYS, tk)
    chains = [(hh, c0, k0) for hh in range(n_kv) for c0 in range(0, group * tq, rc) for k0 in range(0, tk, ks)]

    def scores(hh, c0, k0):
        qt = qt_ref[0, hh * group + c0 // tq, :, c0 % tq:c0 % tq + rc]
        return jnp.dot(k_ref[0, hh, k0:k0 + ks, :], qt, preferred_element_type=F32)

    def softmax(hh, c0, k0, s):
        cols = slice(c0, c0 + rc)
        m_prev = m_ref[hh, :, cols]
        m_new = jnp.maximum(m_prev, jnp.max(s, axis=0, keepdims=True))
        m_ref[hh, :, cols] = m_new
        return jnp.exp2(s - m_new[0:1]).astype(BF16), jnp.exp2(m_prev - m_new)[0:1]

    def accumulate(hh, c0, k0, p, alpha):
        cols = slice(c0, c0 + rc)
        acc_ref[hh, :, cols] = alpha * acc_ref[hh, :, cols] + jnp.dot(
            vt_ref[0, hh, :, k0:k0 + ks], p, preferred_element_type=F32)

    n = len(chains)
    s_q = {i: scores(*chains[i]) for i in range(min(LOOKAHEAD, n))}
    p_q = {0: softmax(*chains[0], s_q.pop(0))}
    for ci in range(n):
        if ci + LOOKAHEAD < n:
            s_q[ci + LOOKAHEAD] = scores(*chains[ci + LOOKAHEAD])
        if ci + 1 < n:
            p_q[ci + 1] = softmax(*chains[ci + 1], s_q.pop(ci + 1))
        accumulate(*chains[ci], *p_q.pop(ci))

    @pl.when(ki == nk - 1)
    def _():
        outs = []
        for hh in range(n_kv):
            acc = acc_ref[hh]
            o = acc[:ML_V] / acc[ML_V:ML_V + 1]
            if group == 2:
                od = o[:, :tq] - lam_ref[0, 0] * o[:, tq:]
                ms = jnp.mean(od * od, axis=0, keepdims=True)
                o = od * lax.rsqrt(ms + EPS) * jnp.concatenate([g_ref[...]] * (tq // LANE), axis=1) * diff_scale
            outs.append(o.T)
        o_ref[0] = jnp.concatenate(outs, axis=-1).astype(o_ref.dtype)


def _flash(qt, k, vt, lam, g, into=None, *, group, diff_scale, t_q, q_off, t_kv, kv_off, tq, tk, t_out):
    b, hq, dk, _ = qt.shape
    n_kv = k.shape[1]
    assert hq == n_kv * group and q_off % tq == 0 and kv_off % tk == 0
    qo, ko = q_off // tq, kv_off // tk
    extra = () if into is None else (into,)
    return pl.pallas_call(
        functools.partial(_flash_body, n_kv, group, diff_scale, min(FLASH_ROWS, tq)),
        grid=(b, t_q // tq, t_kv // tk),
        in_specs=[
            pl.BlockSpec(memory_space=pltpu.SMEM),
            _const_spec(g.shape),
            pl.BlockSpec((1, hq, dk, tq), lambda bi, qi, ki: (bi, 0, 0, qi + qo)),
            pl.BlockSpec((1, n_kv, tk, dk), lambda bi, qi, ki: (bi, 0, ki + ko, 0)),
            pl.BlockSpec((1, n_kv, VT_ROWS, tk), lambda bi, qi, ki: (bi, 0, 0, ki + ko)),
        ] + [pl.BlockSpec(memory_space=pl.ANY)] * len(extra),
        out_specs=pl.BlockSpec((1, tq, n_kv * ML_V), lambda bi, qi, ki: (bi, qi + qo, 0)),
        out_shape=jax.ShapeDtypeStruct((b, t_out, n_kv * ML_V), BF16),
        input_output_aliases={5: 0} if extra else {},
        scratch_shapes=[pltpu.VMEM((n_kv, 8, group * tq), F32),
                        pltpu.VMEM((n_kv, VT_ROWS, group * tq), F32)],
        compiler_params=_cparams(("arbitrary", "arbitrary", "arbitrary")),
        name="flash_da" if group == 2 else "flash_mla",
    )(lam, g, qt, k, vt, *extra)


def _cumsum_rows(x):
    n, w = x.shape
    hi = x.astype(BF16)
    r1 = x - hi.astype(F32)
    mid = r1.astype(BF16)
    lo = (r1 - mid.astype(F32)).astype(BF16)
    tril = (lax.broadcasted_iota(jnp.int32, (n, n), 0) >= lax.broadcasted_iota(jnp.int32, (n, n), 1)).astype(BF16)
    parts = jnp.dot(tril, jnp.concatenate([hi, mid, lo], axis=1), preferred_element_type=F32)
    return parts[:, :w] + parts[:, w:2 * w] + parts[:, 2 * w:]


def _dot_nt(a, b):
    return lax.dot_general(a, b, (((1,), (1,)), ((), ())), preferred_element_type=F32)


def _pad_rows(x, start, total):
    parts = []
    if start:
        parts.append(jnp.zeros((start, x.shape[1]), x.dtype))
    parts.append(x)
    if total - start - x.shape[0]:
        parts.append(jnp.zeros((total - start - x.shape[0], x.shape[1]), x.dtype))
    return jnp.concatenate(parts, axis=0) if len(parts) > 1 else x


def _hgrn_body(nb, qf_ref, zf_ref, vf_ref, qb_ref, zb_ref, vb_ref, lb_ref, e_ref, of_ref, ob_ref, stf_ref, stb_ref):
    @pl.when(pl.program_id(1) == 0)
    def _():
        stf_ref[...] = jnp.zeros(stf_ref.shape, F32)
        stb_ref[...] = jnp.zeros(stb_ref.shape, F32)

    c, nsub, kw = HG_CHUNK, HG_CHUNK // HG_SUB, HG_KW
    streams = [(bi, d) for bi in range(nb) for d in range(2)]
    n = len(streams)
    qs = [(qb_ref if d else qf_ref)[bi] * jnp.sign(1.0 - lb_ref[d:d + 1]) for bi, d in streams]
    vs = [(vb_ref if d else vf_ref)[bi] for bi, d in streams]
    st_refs = [(stb_ref if d else stf_ref) for bi, d in streams]
    sts = [st_refs[i][bi] for i, (bi, d) in enumerate(streams)]

    z = jnp.concatenate([(zb_ref if d else zf_ref)[bi] for bi, d in streams], axis=1)
    lb = jnp.concatenate([lb_ref[d:d + 1] for bi, d in streams], axis=1)
    f = lb + (1.0 - lb) * jax.nn.sigmoid(z)
    kk_all = (1.0 - lb) * jax.nn.sigmoid(-z)
    logf = jnp.log(jnp.maximum(f, FORGET_FLOOR))
    bf = _cumsum_rows(logf)
    kks, bs, b_ends = [], [], []
    for i, (bi, d) in enumerate(streams):
        sl = slice(i * kw, (i + 1) * kw)
        kks.append(kk_all[:, sl])
        b = (bf[c - 1:c, sl] - bf[:, sl] + logf[:, sl]) * LOG2E if d else bf[:, sl] * LOG2E
        bs.append(b)
        b_ends.append(b[0:1] if d else b[c - 1:c])
    cs = [jnp.log2(jnp.abs(kks[i])) - bs[i] for i in range(n)]

    def block_diag(st):
        st = st.astype(BF16)
        zero = jnp.zeros((HG_V, HG_K), BF16)
        return jnp.concatenate([jnp.concatenate(
            [st[:, hh * HG_K:(hh + 1) * HG_K] if j == hh else zero for j in range(HG_H)], axis=1)
            for hh in range(HG_H)], axis=0)

    outs = [_dot_nt((qs[i] * jnp.exp2(bs[i])).astype(BF16), block_diag(sts[i])) for i in range(n)]

    head_r = lax.broadcasted_iota(jnp.int32, (HG_H * c, 1), 0) // c
    same_blk = head_r == lax.broadcasted_iota(jnp.int32, (1, HG_H * c), 1) // c
    v_keep = head_r == lax.broadcasted_iota(jnp.int32, (1, HG_VW), 1) // HG_V
    a_mats = []
    for i, (bi, d) in enumerate(streams):
        q, b, cc = qs[i], bs[i], cs[i]
        q_slots, k_slots = [], []
        for j in (range(nsub - 1) if d else range(1, nsub)):
            r0 = j * HG_SUB
            if d:
                m, k0, k1 = b[r0 + HG_SUB:r0 + HG_SUB + 1], r0 + HG_SUB, c
            else:
                m, k0, k1 = b[r0 - 1:r0], 0, r0
            q_slots.append(_pad_rows(q[r0:r0 + HG_SUB] * jnp.exp2(b[r0:r0 + HG_SUB] - m), r0, c).astype(BF16))
            k_slots.append(_pad_rows(jnp.exp2(m + cc[k0:k1]), k0, c).astype(BF16))
        stack = lambda slots: jnp.concatenate(
            [jnp.concatenate([x[:, hh * HG_K:(hh + 1) * HG_K] for x in slots], axis=1) for hh in range(HG_H)], axis=0)
        a_mats.append(_dot_nt(stack(q_slots), stack(k_slots)))

    terms = []
    for i in range(n):
        q, b, cc = qs[i], bs[i], cs[i]
        for j in range(nsub):
            r0 = j * HG_SUB
            qb, bb, cb = q[r0:r0 + HG_SUB], b[r0:r0 + HG_SUB], cc[r0:r0 + HG_SUB]
            terms += [(qb * jnp.exp2(bb + cb[s:s + 1])).astype(BF16) for s in range(HG_SUB)]
    a_bc = jnp.dot(jnp.concatenate(terms, axis=0), e_ref[...], preferred_element_type=F32)

    for i in range(n):
        v_stack = jnp.where(v_keep, jnp.concatenate([vs[i]] * HG_H, axis=0), 0.0).astype(BF16)
        o_stack = jnp.dot(jnp.where(same_blk, a_mats[i], 0.0).astype(BF16), v_stack, preferred_element_type=F32)
        for hh in range(HG_H):
            outs[i] = outs[i] + o_stack[hh * c:(hh + 1) * c]

    pair = lax.broadcasted_iota(jnp.int32, (HG_SUB * HG_SUB, 1), 0)
    keeps = [jnp.broadcast_to((pair % HG_SUB <= pair // HG_SUB) if d else (pair % HG_SUB >= pair // HG_SUB),
                              (HG_SUB * HG_SUB, HG_VW)) for d in range(2)]
    for i, (bi, d) in enumerate(streams):
        diag = []
        for j in range(nsub):
            r0 = j * HG_SUB
            base = (i * nsub + j) * HG_SUB * HG_SUB
            blk = jnp.where(keeps[d], a_bc[base:base + HG_SUB * HG_SUB], 0.0)
            vb = vs[i][r0:r0 + HG_SUB]
            od = blk[0:HG_SUB] * vb[0:1]
            for s in range(1, HG_SUB):
                od = od + blk[s * HG_SUB:(s + 1) * HG_SUB] * vb[s:s + 1]
            diag.append(od)
        (ob_ref if d else of_ref)[bi] = outs[i] + jnp.concatenate(diag, axis=0)

    for i, (bi, d) in enumerate(streams):
        kd = jnp.exp2(b_ends[i] + cs[i])
        upd = jnp.dot(vs[i].T.astype(BF16), kd.astype(BF16), preferred_element_type=F32)
        upd = jnp.concatenate([upd[hh * HG_V:(hh + 1) * HG_V, hh * HG_K:(hh + 1) * HG_K] for hh in range(HG_H)], axis=1)
        st_refs[i][bi] = sts[i] * jnp.exp2(b_ends[i]) + upd


def _hgrn(hg, lb2, e_mat, n_lat):
    b, t, _ = hg.shape
    c = HG_CHUNK
    nb = _pick(b, HG_BATCH)
    nc, nlc = t // c, n_lat // c
    fwd = lambda i: (i + nlc) % nc
    bwd = lambda i: nc - 1 - i
    v_blk = 3 * HG_KW // HG_VW
    return pl.pallas_call(
        functools.partial(_hgrn_body, nb),
        grid=(b // nb, nc),
        in_specs=[
            pl.BlockSpec((nb, c, HG_KW), lambda bi, i: (bi, fwd(i), 0)),
            pl.BlockSpec((nb, c, HG_KW), lambda bi, i: (bi, fwd(i), 1)),
            pl.BlockSpec((nb, c, HG_VW), lambda bi, i: (bi, fwd(i), v_blk)),
            pl.BlockSpec((nb, c, HG_KW), lambda bi, i: (bi, bwd(i), 0)),
            pl.BlockSpec((nb, c, HG_KW), lambda bi, i: (bi, bwd(i), 2)),
            pl.BlockSpec((nb, c, HG_VW), lambda bi, i: (bi, bwd(i), v_blk)),
            _const_spec(lb2.shape), _const_spec(e_mat.shape),
        ],
        out_specs=(pl.BlockSpec((nb, c, HG_VW), lambda bi, i: (bi, fwd(i), 0)),
                   pl.BlockSpec((nb, c, HG_VW), lambda bi, i: (bi, bwd(i), 0))),
        out_shape=(jax.ShapeDtypeStruct((b, t, HG_VW), F32), jax.ShapeDtypeStruct((b, t, HG_VW), F32)),
        scratch_shapes=[pltpu.VMEM((nb, HG_V, HG_KW), F32), pltpu.VMEM((nb, HG_V, HG_KW), F32)],
        compiler_params=_cparams(("arbitrary", "arbitrary")),
        name="hgrn2",
    )(hg, hg, hg, hg, hg, hg, lb2, e_mat)


def _mix_mlp_body(n_lat, ff_chunk, final, x_ref, oda_ref, oml_ref, of_ref, ob_ref, gate_ref, modl_ref, modc_ref,
                  ghn_ref, gmlp_ref, gfin_ref, grp_ref, wda_ref, wml_ref, whg_ref, w1_ref, w2_ref, o_ref):
    tm = x_ref.shape[1]
    rows = pl.program_id(1) * tm + lax.broadcasted_iota(jnp.int32, (tm, 1), 0)
    is_ctx = rows >= n_lat
    modl, modc = modl_ref[0], modc_ref[...]

    oh = of_ref[0] + ob_ref[0]
    sq = oh * oh
    sq_hi = sq.astype(BF16)
    sq_lo = (sq - sq_hi.astype(F32)).astype(BF16)
    ms = (jnp.dot(sq_hi, grp_ref[...], preferred_element_type=F32)
          + jnp.dot(sq_lo, grp_ref[...], preferred_element_type=F32)) * (1.0 / HG_V)
    gt = gate_ref[0]
    yh = oh * lax.rsqrt(ms + EPS) * ghn_ref[...] * (gt * jax.nn.sigmoid(gt))

    mix = (jnp.dot(oda_ref[0], wda_ref[...], preferred_element_type=F32)
           + jnp.dot(oml_ref[0], wml_ref[...], preferred_element_type=F32)
           + jnp.dot(yh.astype(BF16), whg_ref[...], preferred_element_type=F32))
    g_mix = jnp.where(is_ctx, modc[2:3], modl[2:3])
    x1 = x_ref[0] + g_mix * mix

    hn = _ada_rows(x1, gmlp_ref[...], modl, modc, is_ctx, 3, 4).astype(BF16)
    dff = w1_ref.shape[1]
    acc = jnp.zeros(x1.shape, F32)
    for j in range(dff // ff_chunk):
        u = jnp.dot(hn, w1_ref[:, j * ff_chunk:(j + 1) * ff_chunk], preferred_element_type=F32)
        u = jnp.maximum(u, 0.0)
        acc = acc + jnp.dot((u * u).astype(BF16), w2_ref[j * ff_chunk:(j + 1) * ff_chunk, :],
                            preferred_element_type=F32)
    g_mlp = jnp.where(is_ctx, modc[5:6], modl[5:6])
    x2 = x1 + g_mlp * acc
    if final:
        x2 = _rms(x2, gfin_ref[...])
    o_ref[0] = x2


def _mix_mlp(xa, oda, oml, o_f, o_b, hg, modl, modc, ghn, gmlp, gfin, grp, wda, wml, whg, w1, w2, n_lat, final):
    b, t, d = xa.shape
    t_out = n_lat if final else t
    tm = _pick(t_out, (512, 384, 256, 128)) if final else _pick(t, (384, 256, 128))
    gate_blk = (3 * HG_KW + HG_VW) // HG_VW
    row = lambda w: pl.BlockSpec((1, tm, w), lambda bi, i: (bi, i, 0))
    consts = (modc, ghn, gmlp, gfin, grp, wda, wml, whg, w1, w2)
    return pl.pallas_call(
        functools.partial(_mix_mlp_body, n_lat, 1024, final),
        grid=(b, t_out // tm),
        in_specs=[row(d), row(DA_W), row(ML_W), row(HG_VW), row(HG_VW),
                  pl.BlockSpec((1, tm, HG_VW), lambda bi, i: (bi, i, gate_blk)),
                  pl.BlockSpec((1, N_MOD, d), lambda bi, i: (bi, 0, 0))]
                 + [_const_spec(a.shape) for a in consts],
        out_specs=row(d),
        out_shape=jax.ShapeDtypeStruct((b, t_out, d), F32),
        compiler_params=_cparams(("arbitrary", "arbitrary")),
        name="mix_mlp",
    )(xa, oda, oml, o_f, o_b, hg, modl, *consts)


def _swap_cols(w):
    n = w.shape[-1]
    j = jnp.arange(n)
    src = (j // 16) * 16 + (j % 16 + 8) % 16
    return w[..., src]


def _slot_cols(w, width):
    k, n = w.shape
    hs = n // width
    return jnp.pad(w.reshape(k, hs, width), ((0, 0), (0, 0), (0, LANE - width))).reshape(k, hs * LANE)


def _layer_weights(w_in, w_uq, w_ukv):
    o = 0
    parts = {}
    for name, wdt in (("da_q", DA_W), ("da_k", DA_W), ("da_v", DA_W), ("cq", ML_QR), ("ckv", ML_KVR), ("kr", ROT),
                      ("hq", HG_KW), ("hzf", HG_KW), ("hzb", HG_KW), ("hv", HG_VW), ("hgate", HG_VW)):
        parts[name] = w_in[:, o:o + wdt]
        o += wdt
    d = w_in.shape[0]
    w1 = jnp.concatenate([parts["da_q"], parts["da_k"], jnp.zeros((d, ML_NOPE), F32), parts["kr"],
                          jnp.zeros((d, LANE - ML_NOPE - ROT), F32)], axis=1)
    w2 = jnp.concatenate([parts["da_v"], parts["cq"], parts["ckv"]], axis=1)
    w3 = jnp.concatenate([parts[n] for n in ("hq", "hzf", "hzb", "hv", "hgate")], axis=1)
    uq = w_uq.reshape(ML_QR, ML_H, ML_NOPE + ROT)
    uq_sw = jnp.concatenate([jnp.zeros((ML_QR, ML_H, ML_NOPE), F32), _swap_cols(uq[..., ML_NOPE:])], axis=-1)
    pad_q = lambda a: jnp.pad(a, ((0, 0), (0, 0), (0, LANE - ML_NOPE - ROT))).reshape(ML_QR, ML_H * LANE)
    ukv = w_ukv.reshape(ML_KVR, ML_H, ML_NOPE + ML_V)
    wkn = _slot_cols(ukv[..., :ML_NOPE].reshape(ML_KVR, ML_H * ML_NOPE), ML_NOPE)
    wvm = ukv[..., ML_NOPE:].reshape(ML_KVR, ML_H * ML_V)
    bf = lambda a: a.astype(BF16)
    return bf(w1), bf(_swap_cols(w1)), bf(w2), bf(w3), bf(pad_q(uq)), bf(pad_q(uq_sw)), bf(wkn), bf(wvm)


def _rope_tables(n_lat, n_ctx):
    rows = n_lat // GRID_W
    row = jnp.repeat(jnp.arange(rows, dtype=F32), GRID_W)
    col = jnp.tile(jnp.arange(GRID_W, dtype=F32), rows)
    n_freq = ROT // 4
    inv = ROPE_THETA ** (-jnp.arange(n_freq, dtype=F32) / n_freq)
    ar, ac = row[:, None] * inv, col[:, None] * inv
    c32 = jnp.concatenate([jnp.cos(ar), jnp.cos(ar), jnp.cos(ac), jnp.cos(ac)], axis=1)
    s32 = jnp.concatenate([-jnp.sin(ar), jnp.sin(ar), -jnp.sin(ac), jnp.sin(ac)], axis=1)
    c32 = jnp.concatenate([c32, jnp.ones((n_ctx, ROT), F32)], axis=0)
    s32 = jnp.concatenate([s32, jnp.zeros((n_ctx, ROT), F32)], axis=0)
    t = n_lat + n_ctx
    one, zero = jnp.ones((t, ML_NOPE), F32), jnp.zeros((t, ML_NOPE), F32)
    pad = jnp.zeros((t, LANE - ML_NOPE - ROT), F32)
    return jnp.concatenate([jnp.tile(c32, (1, LANE // ROT)), jnp.tile(s32, (1, LANE // ROT)),
                            one, c32, pad, zero, s32, pad], axis=1)


def kernel(x, c, ctx, c_ctx, w_mod, b_mod, g_mix, g_mlp, w_in, w_out, da_lambda, da_subln_g, mla_g_cq, mla_g_ckv,
           mla_w_uq, mla_w_ukv, hg_lower_bounds, hg_norm_g, w_ff1, w_ff2, g_final):
    bsz, n_lat, d = x.shape
    n_ctx = ctx.shape[1]
    depth = w_in.shape[0]
    t = n_lat + n_ctx

    lb = jax.nn.softmax(hg_lower_bounds.astype(F32), axis=1)
    lb = jnp.cumsum(lb, axis=1) - lb[:, :1]
    cond = jnp.concatenate([c, c_ctx[None]], axis=0)
    mod = _modulation(cond, w_mod, b_mod).reshape(depth, bsz + 1, N_MOD, d)
    rope = _rope_tables(n_lat, n_ctx)

    grp = (jnp.arange(HG_VW)[:, None] // HG_V == jnp.arange(HG_VW)[None, :] // HG_V).astype(BF16)
    e_mat = (jnp.arange(HG_KW)[:, None] // HG_K == jnp.arange(HG_VW)[None, :] // HG_V).astype(BF16)

    tq_da = _pick(n_lat, TQ_DA)
    tq_ml = _pick(n_lat, TQ_ML)
    tk = _pick(t, TK)
    tc = _pick(n_ctx, (256, 128))

    xa = jnp.concatenate([x, ctx], axis=1)
    oda = jnp.zeros((bsz, t, DA_W), BF16)
    oml = jnp.zeros((bsz, t, ML_W), BF16)
    for l in range(depth):
        last = l == depth - 1
        lam_init = 0.8 - 0.6 * math.exp(-0.3 * l)
        lam = (jnp.exp(jnp.sum(da_lambda[l, 0] * da_lambda[l, 1]))
               - jnp.exp(jnp.sum(da_lambda[l, 2] * da_lambda[l, 3])) + lam_init).astype(F32).reshape(1, 1)
        modl, modc = mod[l, :bsz], mod[l, bsz]
        w1, w1s, w2, w3, wuq, wuqs, wkn, wvm = _layer_weights(w_in[l], mla_w_uq[l], mla_w_ukv[l])
        wts = (w1, w1s, w2, w3, mla_g_cq[l][None], mla_g_ckv[l][None], wuq, wuqs, wkn, wvm)
        qda, kda, vda, qm, km, vm, hg = _project(xa, modl, modc, g_mix[l][None], rope, wts, n_lat)

        g_da = jnp.broadcast_to(da_subln_g[l][:, None], (2 * DA_D, LANE))
        t_out = n_lat if last else t
        da_kw = dict(group=2, diff_scale=1.0 - lam_init, t_out=t_out)
        ml_kw = dict(group=1, diff_scale=1.0, t_out=t_out)
        lat_kw = dict(t_q=n_lat, q_off=0, t_kv=t, kv_off=0, tk=tk)
        oda = _flash(qda, kda, vda, lam, g_da, None if last else oda, tq=tq_da, **lat_kw, **da_kw)
        oml = _flash(qm, km, vm, lam, g_da, None if last else oml, tq=tq_ml, **lat_kw, **ml_kw)
        if not last:
            ctx_kw = dict(t_q=n_ctx, q_off=n_lat, t_kv=n_ctx, kv_off=n_lat, tq=tc, tk=tc)
            oda = _flash(qda, kda, vda, lam, g_da, oda, **ctx_kw, **da_kw)
            oml = _flash(qm, km, vm, lam, g_da, oml, **ctx_kw, **ml_kw)

        o_f, o_b = _hgrn(hg, jnp.stack([lb[0, l], lb[1, l]]), e_mat, n_lat)

        ghn = jnp.tile(hg_norm_g[l], HG_H)[None]
        bf = lambda a: a.astype(BF16)
        xa = _mix_mlp(xa, oda, oml, o_f, o_b, hg, modl, modc, ghn, g_mlp[l][None], g_final[None], grp,
                      bf(w_out[l, :DA_W]), bf(w_out[l, DA_W:DA_W + ML_W]), bf(w_out[l, DA_W + ML_W:]),
                      bf(w_ff1[l]), bf(w_ff2[l]), n_lat, last)
    return xa
```

```python
import functools
import math

import jax
import jax.numpy as jnp
from jax import lax
from jax.experimental import pallas as pl
from jax.experimental.pallas import tpu as pltpu

F32 = jnp.float32
BF16 = jnp.bfloat16

EPS = 1e-6
ROPE_THETA = 10000.0
GRID_W = 64
ROT = 32
DA_H, DA_D = 6, 32
DA_W = DA_H * 2 * DA_D
ML_H, ML_QR, ML_KVR, ML_NOPE, ML_V = 6, 256, 128, 64, 64
ML_W = ML_H * ML_V
HG_H, HG_K, HG_V = 4, 128, 64
HG_KW, HG_VW = HG_H * HG_K, HG_H * HG_V
FORGET_FLOOR = 1e-30
N_MOD = 6
LOG2E = 1.4426950408889634

HG_CHUNK = 64
HG_BATCH = (4, 2, 1)
HG_SUB = 8
LANE = 128
VMEM_LIMIT = 56 * 1024 * 1024
TQ_DA = (1024, 512, 256, 128)
TQ_ML = (2048, 1024, 512, 256, 128)
TK = (768, 512, 384, 256, 128)
VT_ROWS = 80
LOOKAHEAD = 6
FLASH_KEYS = 256
FLASH_ROWS = 256


def _cparams(sem):
    return pltpu.CompilerParams(dimension_semantics=sem, vmem_limit_bytes=VMEM_LIMIT)


def _const_spec(shape):
    nd = len(shape)
    return pl.BlockSpec(shape, lambda *_: (0,) * nd, pipeline_mode=pl.Buffered(1))


def _pick(n, cands):
    for c in cands:
        if n % c == 0:
            return c
    raise ValueError(f"no tile for {n} in {cands}")


def _mod_body(c_ref, w_ref, b_ref, o_ref):
    cc = c_ref[...]
    s = cc * jax.nn.sigmoid(cc)
    o_ref[0] = jnp.dot(s.astype(BF16), w_ref[0].astype(BF16), preferred_element_type=F32) + b_ref[0]


def _modulation(cond, w_mod, b_mod):
    depth, d, nm = w_mod.shape
    r = cond.shape[0]
    tn = _pick(nm, (1536, 1024, 512, 128))
    return pl.pallas_call(
        _mod_body,
        grid=(depth, nm // tn),
        in_specs=[pl.BlockSpec((r, d), lambda l, j: (0, 0)),
                  pl.BlockSpec((1, d, tn), lambda l, j: (l, 0, j)),
                  pl.BlockSpec((1, 1, tn), lambda l, j: (l, 0, j))],
        out_specs=pl.BlockSpec((1, r, tn), lambda l, j: (l, 0, j)),
        out_shape=jax.ShapeDtypeStruct((depth, r, nm), F32),
        compiler_params=_cparams(("arbitrary", "arbitrary")),
        name="modulation",
    )(cond, w_mod, b_mod.reshape(depth, 1, nm))


def _ada_rows(x, g, modl, modc, is_ctx, i_shift, i_scale):
    ms = jnp.mean(x * x, axis=-1, keepdims=True)
    y = x * lax.rsqrt(ms + EPS) * g
    shift = jnp.where(is_ctx, modc[i_shift:i_shift + 1], modl[i_shift:i_shift + 1])
    scale = jnp.where(is_ctx, modc[i_scale:i_scale + 1], modl[i_scale:i_scale + 1])
    return y * (1.0 + scale) + shift


def _rms(x, g):
    return x * lax.rsqrt(jnp.mean(x * x, axis=-1, keepdims=True) + EPS) * g


def _proj_body(n_lat, x_ref, modl_ref, modc_ref, g_ref, rope_ref, w1_ref, w1s_ref, w2_ref, w3_ref,
               gcq_ref, gckv_ref, wuq_ref, wuqs_ref, wkn_ref, wvm_ref,
               qda_ref, kda_ref, vda_ref, qm_ref, km_ref, vm_ref, hg_ref):
    tm = x_ref.shape[1]
    rows = pl.program_id(1) * tm + lax.broadcasted_iota(jnp.int32, (tm, 1), 0)
    is_ctx = rows >= n_lat
    h = _ada_rows(x_ref[0], g_ref[...], modl_ref[0], modc_ref[...], is_ctx, 0, 1).astype(BF16)

    rope = rope_ref[...]
    c4, s4, cq_t, sq_t = (rope[:, k * LANE:(k + 1) * LANE] for k in range(4))
    n1 = w1_ref.shape[1] // LANE
    p1 = jnp.dot(h, w1_ref[...], preferred_element_type=F32)
    p1s = jnp.dot(h, w1s_ref[...], preferred_element_type=F32)
    rot = p1 * jnp.concatenate([c4] * n1, axis=1) + p1s * jnp.concatenate([s4] * n1, axis=1)

    rq_t = (rot[:, :DA_W] * (DA_D ** -0.5 * LOG2E)).T
    row64 = lax.broadcasted_iota(jnp.int32, (2 * DA_D, 1), 0)
    for vh in range(2 * DA_H):
        hh, comp = vh // 2, vh % 2
        keep = (row64 >= DA_D) if comp else (row64 < DA_D)
        qda_ref[0, vh] = jnp.where(keep, rq_t[hh * 2 * DA_D:(hh + 1) * 2 * DA_D], 0.0).astype(BF16)
    for hh in range(DA_H):
        kda_ref[0, hh] = rot[:, DA_W + hh * 2 * DA_D:DA_W + (hh + 1) * 2 * DA_D].astype(BF16)
    kr_slot = rot[:, 2 * DA_W:2 * DA_W + LANE]

    ones_rows = (lax.broadcasted_iota(jnp.int32, (VT_ROWS - ML_V, tm), 0) == 0).astype(F32)
    p2 = jnp.dot(h, w2_ref[...], preferred_element_type=F32)
    v_t = p2[:, :DA_W].T
    for hh in range(DA_H):
        vda_ref[0, hh] = jnp.concatenate([v_t[hh * ML_V:(hh + 1) * ML_V], ones_rows], axis=0).astype(BF16)
    cqn = _rms(p2[:, DA_W:DA_W + ML_QR], gcq_ref[...]).astype(BF16)
    ckvn = _rms(p2[:, DA_W + ML_QR:DA_W + ML_QR + ML_KVR], gckv_ref[...]).astype(BF16)

    qu = jnp.dot(cqn, wuq_ref[...], preferred_element_type=F32)
    qus = jnp.dot(cqn, wuqs_ref[...], preferred_element_type=F32)
    qm = (qu * jnp.concatenate([cq_t] * ML_H, axis=1) + qus * jnp.concatenate([sq_t] * ML_H, axis=1))
    qm_ref[0] = (qm * ((ML_NOPE + ROT) ** -0.5 * LOG2E)).T.reshape(ML_H, LANE, tm).astype(BF16)
    kn = jnp.dot(ckvn, wkn_ref[...], preferred_element_type=F32)
    for hh in range(ML_H):
        km_ref[0, hh] = (kn[:, hh * LANE:(hh + 1) * LANE] + kr_slot).astype(BF16)
    vm_t = jnp.dot(ckvn, wvm_ref[...], preferred_element_type=F32).T
    for hh in range(ML_H):
        vm_ref[0, hh] = jnp.concatenate([vm_t[hh * ML_V:(hh + 1) * ML_V], ones_rows], axis=0).astype(BF16)

    hg_ref[0] = jnp.dot(h, w3_ref[...], preferred_element_type=F32)


def _project(xa, modl, modc, g, rope, wts, n_lat):
    b, t, d = xa.shape
    tm = _pick(t, (768, 384, 256, 128))
    w1, w1s, w2, w3, gcq, gckv, wuq, wuqs, wkn, wvm = wts
    nhg = w3.shape[1]
    rows = lambda hs, w: (jax.ShapeDtypeStruct((b, hs, t, w), BF16),
                          pl.BlockSpec((1, hs, tm, w), lambda bi, i: (bi, 0, i, 0)))
    cols = lambda hs, r: (jax.ShapeDtypeStruct((b, hs, r, t), BF16),
                          pl.BlockSpec((1, hs, r, tm), lambda bi, i: (bi, 0, 0, i)))
    outs = (cols(2 * DA_H, 2 * DA_D), rows(DA_H, 2 * DA_D), cols(DA_H, VT_ROWS),
            cols(ML_H, LANE), rows(ML_H, LANE), cols(ML_H, VT_ROWS),
            (jax.ShapeDtypeStruct((b, t, nhg), F32), pl.BlockSpec((1, tm, nhg), lambda bi, i: (bi, i, 0))))
    out_shapes = tuple(o[0] for o in outs)
    out_specs = tuple(o[1] for o in outs)
    in_specs = [
        pl.BlockSpec((1, tm, d), lambda bi, i: (bi, i, 0)),
        pl.BlockSpec((1, N_MOD, d), lambda bi, i: (bi, 0, 0)),
        _const_spec(modc.shape), _const_spec(g.shape),
        pl.BlockSpec((tm, rope.shape[1]), lambda bi, i: (i, 0)),
    ] + [_const_spec(w.shape) for w in wts]
    return pl.pallas_call(
        functools.partial(_proj_body, n_lat),
        grid=(b, t // tm),
        in_specs=in_specs, out_specs=out_specs, out_shape=out_shapes,
        compiler_params=_cparams(("arbitrary", "arbitrary")),
        name="project",
    )(xa, modl, modc, g, rope, *wts)


def _flash_body(n_kv, group, diff_scale, rc, lam_ref, g_ref, qt_ref, k_ref, vt_ref, *refs):
    o_ref, m_ref, acc_ref = refs[-3:]
    ki = pl.program_id(2)
    nk = pl.num_programs(2)
    tq = qt_ref.shape[3]

    @pl.when(ki == 0)
    def _():
        m_ref[...] = jnp.full(m_ref.shape, -jnp.inf, F32)
        acc_ref[...] = jnp.zeros(acc_ref.shape, F32)

    tk = k_ref.shape[2]
    ks = min(FLASH_KEYS, tk)
    chains = [(hh, c0, k0) for hh in range(n_kv) for c0 in range(0, group * tq, rc) for k0 in range(0, tk, ks)]

    def scores(hh, c0, k0):
        qt = qt_ref[0, hh * group + c0 // tq, :, c0 % tq:c0 % tq + rc]
        return jnp.dot(k_ref[0, hh, k0:k0 + ks, :], qt, preferred_element_type=F32)

    def softmax(hh, c0, k0, s):
        cols = slice(c0, c0 + rc)
        m_prev = m_ref[hh, :, cols]
        m_new = jnp.maximum(m_prev, jnp.max(s, axis=0, keepdims=True))
        m_ref[hh, :, cols] = m_new
        return jnp.exp2(s - m_new[0:1]).astype(BF16), jnp.exp2(m_prev - m_new)[0:1]

    def accumulate(hh, c0, k0, p, alpha):
        cols = slice(c0, c0 + rc)
        acc_ref[hh, :, cols] = alpha * acc_ref[hh, :, cols] + jnp.dot(
            vt_ref[0, hh, :, k0:k0 + ks], p, preferred_element_type=F32)

    n = len(chains)
    s_q = {i: scores(*chains[i]) for i in range(min(LOOKAHEAD, n))}
    p_q = {0: softmax(*chains[0], s_q.pop(0))}
    for ci in range(n):
        if ci + LOOKAHEAD < n:
            s_q[ci + LOOKAHEAD] = scores(*chains[ci + LOOKAHEAD])
        if ci + 1 < n:
            p_q[ci + 1] = softmax(*chains[ci + 1], s_q.pop(ci + 1))
        accumulate(*chains[ci], *p_q.pop(ci))

    @pl.when(ki == nk - 1)
    def _():
        outs = []
        for hh in range(n_kv):
            acc = acc_ref[hh]
            o = acc[:ML_V] / acc[ML_V:ML_V + 1]
            if group == 2:
                od = o[:, :tq] - lam_ref[0, 0] * o[:, tq:]
                ms = jnp.mean(od * od, axis=0, keepdims=True)
                o = od * lax.rsqrt(ms + EPS) * jnp.concatenate([g_ref[...]] * (tq // LANE), axis=1) * diff_scale
            outs.append(o.T)
        o_ref[0] = jnp.concatenate(outs, axis=-1).astype(o_ref.dtype)


def _flash(qt, k, vt, lam, g, into=None, *, group, diff_scale, t_q, q_off, t_kv, kv_off, tq, tk, t_out):
    b, hq, dk, _ = qt.shape
    n_kv = k.shape[1]
    assert hq == n_kv * group and q_off % tq == 0 and kv_off % tk == 0
    qo, ko = q_off // tq, kv_off // tk
    extra = () if into is None else (into,)
    return pl.pallas_call(
        functools.partial(_flash_body, n_kv, group, diff_scale, min(FLASH_ROWS, tq)),
        grid=(b, t_q // tq, t_kv // tk),
        in_specs=[
            pl.BlockSpec(memory_space=pltpu.SMEM),
            _const_spec(g.shape),
            pl.BlockSpec((1, hq, dk, tq), lambda bi, qi, ki: (bi, 0, 0, qi + qo)),
            pl.BlockSpec((1, n_kv, tk, dk), lambda bi, qi, ki: (bi, 0, ki + ko, 0)),
            pl.BlockSpec((1, n_kv, VT_ROWS, tk), lambda bi, qi, ki: (bi, 0, 0, ki + ko)),
        ] + [pl.BlockSpec(memory_space=pl.ANY)] * len(extra),
        out_specs=pl.BlockSpec((1, tq, n_kv * ML_V), lambda bi, qi, ki: (bi, qi + qo, 0)),
        out_shape=jax.ShapeDtypeStruct((b, t_out, n_kv * ML_V), BF16),
        input_output_aliases={5: 0} if extra else {},
        scratch_shapes=[pltpu.VMEM((n_kv, 8, group * tq), F32),
                        pltpu.VMEM((n_kv, VT_ROWS, group * tq), F32)],
        compiler_params=_cparams(("arbitrary", "arbitrary", "arbitrary")),
        name="flash_da" if group == 2 else "flash_mla",
    )(lam, g, qt, k, vt, *extra)


def _cumsum_rows(x):
    n, w = x.shape
    hi = x.astype(BF16)
    r1 = x - hi.astype(F32)
    mid = r1.astype(BF16)
    lo = (r1 - mid.astype(F32)).astype(BF16)
    tril = (lax.broadcasted_iota(jnp.int32, (n, n), 0) >= lax.broadcasted_iota(jnp.int32, (n, n), 1)).astype(BF16)
    parts = jnp.dot(tril, jnp.concatenate([hi, mid, lo], axis=1), preferred_element_type=F32)
    return parts[:, :w] + parts[:, w:2 * w] + parts[:, 2 * w:]


def _dot_nt(a, b):
    return lax.dot_general(a, b, (((1,), (1,)), ((), ())), preferred_element_type=F32)


def _pad_rows(x, start, total):
    parts = []
    if start:
        parts.append(jnp.zeros((start, x.shape[1]), x.dtype))
    parts.append(x)
    if total - start - x.shape[0]:
        parts.append(jnp.zeros((total - start - x.shape[0], x.shape[1]), x.dtype))
    return jnp.concatenate(parts, axis=0) if len(parts) > 1 else x


def _hgrn_body(nb, qf_ref, zf_ref, vf_ref, qb_ref, zb_ref, vb_ref, lb_ref, e_ref, of_ref, ob_ref, stf_ref, stb_ref):
    @pl.when(pl.program_id(1) == 0)
    def _():
        stf_ref[...] = jnp.zeros(stf_ref.shape, F32)
        stb_ref[...] = jnp.zeros(stb_ref.shape, F32)

    c, nsub, kw = HG_CHUNK, HG_CHUNK // HG_SUB, HG_KW
    streams = [(bi, d) for bi in range(nb) for d in range(2)]
    n = len(streams)
    qs = [(qb_ref if d else qf_ref)[bi] * jnp.sign(1.0 - lb_ref[d:d + 1]) for bi, d in streams]
    vs = [(vb_ref if d else vf_ref)[bi] for bi, d in streams]
    st_refs = [(stb_ref if d else stf_ref) for bi, d in streams]
    sts = [st_refs[i][bi] for i, (bi, d) in enumerate(streams)]

    z = jnp.concatenate([(zb_ref if d else zf_ref)[bi] for bi, d in streams], axis=1)
    lb = jnp.concatenate([lb_ref[d:d + 1] for bi, d in streams], axis=1)
    f = lb + (1.0 - lb) * jax.nn.sigmoid(z)
    kk_all = (1.0 - lb) * jax.nn.sigmoid(-z)
    logf = jnp.log(jnp.maximum(f, FORGET_FLOOR))
    bf = _cumsum_rows(logf)
    kks, bs, b_ends = [], [], []
    for i, (bi, d) in enumerate(streams):
        sl = slice(i * kw, (i + 1) * kw)
        kks.append(kk_all[:, sl])
        b = (bf[c - 1:c, sl] - bf[:, sl] + logf[:, sl]) * LOG2E if d else bf[:, sl] * LOG2E
        bs.append(b)
        b_ends.append(b[0:1] if d else b[c - 1:c])
    cs = [jnp.log2(jnp.abs(kks[i])) - bs[i] for i in range(n)]

    def block_diag(st):
        st = st.astype(BF16)
        zero = jnp.zeros((HG_V, HG_K), BF16)
        return jnp.concatenate([jnp.concatenate(
            [st[:, hh * HG_K:(hh + 1) * HG_K] if j == hh else zero for j in range(HG_H)], axis=1)
            for hh in range(HG_H)], axis=0)

    outs = [_dot_nt((qs[i] * jnp.exp2(bs[i])).astype(BF16), block_diag(sts[i])) for i in range(n)]

    head_r = lax.broadcasted_iota(jnp.int32, (HG_H * c, 1), 0) // c
    same_blk = head_r == lax.broadcasted_iota(jnp.int32, (1, HG_H * c), 1) // c
    v_keep = head_r == lax.broadcasted_iota(jnp.int32, (1, HG_VW), 1) // HG_V
    a_mats = []
    for i, (bi, d) in enumerate(streams):
        q, b, cc = qs[i], bs[i], cs[i]
        q_slots, k_slots = [], []
        for j in (range(nsub - 1) if d else range(1, nsub)):
            r0 = j * HG_SUB
            if d:
                m, k0, k1 = b[r0 + HG_SUB:r0 + HG_SUB + 1], r0 + HG_SUB, c
            else:
                m, k0, k1 = b[r0 - 1:r0], 0, r0
            q_slots.append(_pad_rows(q[r0:r0 + HG_SUB] * jnp.exp2(b[r0:r0 + HG_SUB] - m), r0, c).astype(BF16))
            k_slots.append(_pad_rows(jnp.exp2(m + cc[k0:k1]), k0, c).astype(BF16))
        stack = lambda slots: jnp.concatenate(
            [jnp.concatenate([x[:, hh * HG_K:(hh + 1) * HG_K] for x in slots], axis=1) for hh in range(HG_H)], axis=0)
        a_mats.append(_dot_nt(stack(q_slots), stack(k_slots)))

    terms = []
    for i in range(n):
        q, b, cc = qs[i], bs[i], cs[i]
        for j in range(nsub):
            r0 = j * HG_SUB
            qb, bb, cb = q[r0:r0 + HG_SUB], b[r0:r0 + HG_SUB], cc[r0:r0 + HG_SUB]
            terms += [(qb * jnp.exp2(bb + cb[s:s + 1])).astype(BF16) for s in range(HG_SUB)]
    a_bc = jnp.dot(jnp.concatenate(terms, axis=0), e_ref[...], preferred_element_type=F32)

    for i in range(n):
        v_stack = jnp.where(v_keep, jnp.concatenate([vs[i]] * HG_H, axis=0), 0.0).astype(BF16)
        o_stack = jnp.dot(jnp.where(same_blk, a_mats[i], 0.0).astype(BF16), v_stack, preferred_element_type=F32)
        for hh in range(HG_H):
            outs[i] = outs[i] + o_stack[hh * c:(hh + 1) * c]

    pair = lax.broadcasted_iota(jnp.int32, (HG_SUB * HG_SUB, 1), 0)
    keeps = [jnp.broadcast_to((pair % HG_SUB <= pair // HG_SUB) if d else (pair % HG_SUB >= pair // HG_SUB),
                              (HG_SUB * HG_SUB, HG_VW)) for d in range(2)]
    for i, (bi, d) in enumerate(streams):
        diag = []
        for j in range(nsub):
            r0 = j * HG_SUB
            base = (i * nsub + j) * HG_SUB * HG_SUB
            blk = jnp.where(keeps[d], a_bc[base:base + HG_SUB * HG_SUB], 0.0)
            vb = vs[i][r0:r0 + HG_SUB]
            od = blk[0:HG_SUB] * vb[0:1]
            for s in range(1, HG_SUB):
                od = od + blk[s * HG_SUB:(s + 1) * HG_SUB] * vb[s:s + 1]
            diag.append(od)
        (ob_ref if d else of_ref)[bi] = outs[i] + jnp.concatenate(diag, axis=0)

    for i, (bi, d) in enumerate(streams):
        kd = jnp.exp2(b_ends[i] + cs[i])
        upd = jnp.dot(vs[i].T.astype(BF16), kd.astype(BF16), preferred_element_type=F32)
        upd = jnp.concatenate([upd[hh * HG_V:(hh + 1) * HG_V, hh * HG_K:(hh + 1) * HG_K] for hh in range(HG_H)], axis=1)
        st_refs[i][bi] = sts[i] * jnp.exp2(b_ends[i]) + upd


def _hgrn(hg, lb2, e_mat, n_lat):
    b, t, _ = hg.shape
    c = HG_CHUNK
    nb = _pick(b, HG_BATCH)
    nc, nlc = t // c, n_lat // c
    fwd = lambda i: (i + nlc) % nc
    bwd = lambda i: nc - 1 - i
    v_blk = 3 * HG_KW // HG_VW
    return pl.pallas_call(
        functools.partial(_hgrn_body, nb),
        grid=(b // nb, nc),
        in_specs=[
            pl.BlockSpec((nb, c, HG_KW), lambda bi, i: (bi, fwd(i), 0)),
            pl.BlockSpec((nb, c, HG_KW), lambda bi, i: (bi, fwd(i), 1)),
            pl.BlockSpec((nb, c, HG_VW), lambda bi, i: (bi, fwd(i), v_blk)),
            pl.BlockSpec((nb, c, HG_KW), lambda bi, i: (bi, bwd(i), 0)),
            pl.BlockSpec((nb, c, HG_KW), lambda bi, i: (bi, bwd(i), 2)),
            pl.BlockSpec((nb, c, HG_VW), lambda bi, i: (bi, bwd(i), v_blk)),
            _const_spec(lb2.shape), _const_spec(e_mat.shape),
        ],
        out_specs=(pl.BlockSpec((nb, c, HG_VW), lambda bi, i: (bi, fwd(i), 0)),
                   pl.BlockSpec((nb, c, HG_VW), lambda bi, i: (bi, bwd(i), 0))),
        out_shape=(jax.ShapeDtypeStruct((b, t, HG_VW), F32), jax.ShapeDtypeStruct((b, t, HG_VW), F32)),
        scratch_shapes=[pltpu.VMEM((nb, HG_V, HG_KW), F32), pltpu.VMEM((nb, HG_V, HG_KW), F32)],
        compiler_params=_cparams(("arbitrary", "arbitrary")),
        name="hgrn2",
    )(hg, hg, hg, hg, hg, hg, lb2, e_mat)


def _mix_mlp_body(n_lat, ff_chunk, final, x_ref, oda_ref, oml_ref, of_ref, ob_ref, gate_ref, modl_ref, modc_ref,
                  ghn_ref, gmlp_ref, gfin_ref, grp_ref, wo_ref, w1_ref, w2_ref, o_ref):
    tm = x_ref.shape[1]
    rows = pl.program_id(1) * tm + lax.broadcasted_iota(jnp.int32, (tm, 1), 0)
    is_ctx = rows >= n_lat
    modl, modc = modl_ref[0], modc_ref[...]

    oh = of_ref[0] + ob_ref[0]
    sq = oh * oh
    sq_hi = sq.astype(BF16)
    sq_lo = (sq - sq_hi.astype(F32)).astype(BF16)
    ms = (jnp.dot(sq_hi, grp_ref[...], preferred_element_type=F32)
          + jnp.dot(sq_lo, grp_ref[...], preferred_element_type=F32)) * (1.0 / HG_V)
    gt = gate_ref[0]
    yh = oh * lax.rsqrt(ms + EPS) * ghn_ref[...] * (gt * jax.nn.sigmoid(gt))

    y = jnp.concatenate([oda_ref[0], oml_ref[0], yh.astype(BF16)], axis=1)
    mix = jnp.dot(y, wo_ref[...], preferred_element_type=F32)
    g_mix = jnp.where(is_ctx, modc[2:3], modl[2:3])
    x1 = x_ref[0] + g_mix * mix

    hn = _ada_rows(x1, gmlp_ref[...], modl, modc, is_ctx, 3, 4).astype(BF16)
    dff = w1_ref.shape[1]
    acc = jnp.zeros(x1.shape, F32)
    for j in range(dff // ff_chunk):
        u = jnp.dot(hn, w1_ref[:, j * ff_chunk:(j + 1) * ff_chunk], preferred_element_type=F32)
        u = jnp.maximum(u, 0.0)
        acc = acc + jnp.dot((u * u).astype(BF16), w2_ref[j * ff_chunk:(j + 1) * ff_chunk, :],
                            preferred_element_type=F32)
    g_mlp = jnp.where(is_ctx, modc[5:6], modl[5:6])
    x2 = x1 + g_mlp * acc
    if final:
        x2 = _rms(x2, gfin_ref[...])
    o_ref[0] = x2


def _mix_mlp(xa, oda, oml, o_f, o_b, hg, modl, modc, ghn, gmlp, gfin, grp, wo, w1, w2, n_lat, final):
    b, t, d = xa.shape
    t_out = n_lat if final else t
    tm = _pick(t_out, (512, 384, 256, 128)) if final else _pick(t, (384, 256, 128))
    gate_blk = (3 * HG_KW + HG_VW) // HG_VW
    row = lambda w: pl.BlockSpec((1, tm, w), lambda bi, i: (bi, i, 0))
    consts = (modc, ghn, gmlp, gfin, grp, wo, w1, w2)
    return pl.pallas_call(
        functools.partial(_mix_mlp_body, n_lat, 1024, final),
        grid=(b, t_out // tm),
        in_specs=[row(d), row(DA_W), row(ML_W), row(HG_VW), row(HG_VW),
                  pl.BlockSpec((1, tm, HG_VW), lambda bi, i: (bi, i, gate_blk)),
                  pl.BlockSpec((1, N_MOD, d), lambda bi, i: (bi, 0, 0))]
                 + [_const_spec(a.shape) for a in consts],
        out_specs=row(d),
        out_shape=jax.ShapeDtypeStruct((b, t_out, d), F32),
        compiler_params=_cparams(("arbitrary", "arbitrary")),
        name="mix_mlp",
    )(xa, oda, oml, o_f, o_b, hg, modl, *consts)


def _swap_cols(w):
    n = w.shape[-1]
    j = jnp.arange(n)
    src = (j // 16) * 16 + (j % 16 + 8) % 16
    return w[..., src]


def _slot_cols(w, width):
    k, n = w.shape
    hs = n // width
    return jnp.pad(w.reshape(k, hs, width), ((0, 0), (0, 0), (0, LANE - width))).reshape(k, hs * LANE)


def _layer_weights(w_in, w_uq, w_ukv):
    o = 0
    parts = {}
    for name, wdt in (("da_q", DA_W), ("da_k", DA_W), ("da_v", DA_W), ("cq", ML_QR), ("ckv", ML_KVR), ("kr", ROT),
                      ("hq", HG_KW), ("hzf", HG_KW), ("hzb", HG_KW), ("hv", HG_VW), ("hgate", HG_VW)):
        parts[name] = w_in[:, o:o + wdt]
        o += wdt
    d = w_in.shape[0]
    w1 = jnp.concatenate([parts["da_q"], parts["da_k"], jnp.zeros((d, ML_NOPE), F32), parts["kr"],
                          jnp.zeros((d, LANE - ML_NOPE - ROT), F32)], axis=1)
    w2 = jnp.concatenate([parts["da_v"], parts["cq"], parts["ckv"]], axis=1)
    w3 = jnp.concatenate([parts[n] for n in ("hq", "hzf", "hzb", "hv", "hgate")], axis=1)
    uq = w_uq.reshape(ML_QR, ML_H, ML_NOPE + ROT)
    uq_sw = jnp.concatenate([jnp.zeros((ML_QR, ML_H, ML_NOPE), F32), _swap_cols(uq[..., ML_NOPE:])], axis=-1)
    pad_q = lambda a: jnp.pad(a, ((0, 0), (0, 0), (0, LANE - ML_NOPE - ROT))).reshape(ML_QR, ML_H * LANE)
    ukv = w_ukv.reshape(ML_KVR, ML_H, ML_NOPE + ML_V)
    wkn = _slot_cols(ukv[..., :ML_NOPE].reshape(ML_KVR, ML_H * ML_NOPE), ML_NOPE)
    wvm = ukv[..., ML_NOPE:].reshape(ML_KVR, ML_H * ML_V)
    bf = lambda a: a.astype(BF16)
    return bf(w1), bf(_swap_cols(w1)), bf(w2), bf(w3), bf(pad_q(uq)), bf(pad_q(uq_sw)), bf(wkn), bf(wvm)


def _rope_tables(n_lat, n_ctx):
    rows = n_lat // GRID_W
    row = jnp.repeat(jnp.arange(rows, dtype=F32), GRID_W)
    col = jnp.tile(jnp.arange(GRID_W, dtype=F32), rows)
    n_freq = ROT // 4
    inv = ROPE_THETA ** (-jnp.arange(n_freq, dtype=F32) / n_freq)
    ar, ac = row[:, None] * inv, col[:, None] * inv
    c32 = jnp.concatenate([jnp.cos(ar), jnp.cos(ar), jnp.cos(ac), jnp.cos(ac)], axis=1)
    s32 = jnp.concatenate([-jnp.sin(ar), jnp.sin(ar), -jnp.sin(ac), jnp.sin(ac)], axis=1)
    c32 = jnp.concatenate([c32, jnp.ones((n_ctx, ROT), F32)], axis=0)
    s32 = jnp.concatenate([s32, jnp.zeros((n_ctx, ROT), F32)], axis=0)
    t = n_lat + n_ctx
    one, zero = jnp.ones((t, ML_NOPE), F32), jnp.zeros((t, ML_NOPE), F32)
    pad = jnp.zeros((t, LANE - ML_NOPE - ROT), F32)
    return jnp.concatenate([jnp.tile(c32, (1, LANE // ROT)), jnp.tile(s32, (1, LANE // ROT)),
                            one, c32, pad, zero, s32, pad], axis=1)


def kernel(x, c, ctx, c_ctx, w_mod, b_mod, g_mix, g_mlp, w_in, w_out, da_lambda, da_subln_g, mla_g_cq, mla_g_ckv,
           mla_w_uq, mla_w_ukv, hg_lower_bounds, hg_norm_g, w_ff1, w_ff2, g_final):
    bsz, n_lat, d = x.shape
    n_ctx = ctx.shape[1]
    depth = w_in.shape[0]
    t = n_lat + n_ctx

    lb = jax.nn.softmax(hg_lower_bounds.astype(F32), axis=1)
    lb = jnp.cumsum(lb, axis=1) - lb[:, :1]
    cond = jnp.concatenate([c, c_ctx[None]], axis=0)
    mod = _modulation(cond, w_mod, b_mod).reshape(depth, bsz + 1, N_MOD, d)
    rope = _rope_tables(n_lat, n_ctx)

    grp = (jnp.arange(HG_VW)[:, None] // HG_V == jnp.arange(HG_VW)[None, :] // HG_V).astype(BF16)
    e_mat = (jnp.arange(HG_KW)[:, None] // HG_K == jnp.arange(HG_VW)[None, :] // HG_V).astype(BF16)

    tq_da = _pick(n_lat, TQ_DA)
    tq_ml = _pick(n_lat, TQ_ML)
    tk = _pick(t, TK)
    tc = _pick(n_ctx, (256, 128))

    xa = jnp.concatenate([x, ctx], axis=1)
    oda = jnp.zeros((bsz, t, DA_W), BF16)
    oml = jnp.zeros((bsz, t, ML_W), BF16)
    for l in range(depth):
        last = l == depth - 1
        lam_init = 0.8 - 0.6 * math.exp(-0.3 * l)
        lam = (jnp.exp(jnp.sum(da_lambda[l, 0] * da_lambda[l, 1]))
               - jnp.exp(jnp.sum(da_lambda[l, 2] * da_lambda[l, 3])) + lam_init).astype(F32).reshape(1, 1)
        modl, modc = mod[l, :bsz], mod[l, bsz]
        w1, w1s, w2, w3, wuq, wuqs, wkn, wvm = _layer_weights(w_in[l], mla_w_uq[l], mla_w_ukv[l])
        wts = (w1, w1s, w2, w3, mla_g_cq[l][None], mla_g_ckv[l][None], wuq, wuqs, wkn, wvm)
        qda, kda, vda, qm, km, vm, hg = _project(xa, modl, modc, g_mix[l][None], rope, wts, n_lat)

        g_da = jnp.broadcast_to(da_subln_g[l][:, None], (2 * DA_D, LANE))
        t_out = n_lat if last else t
        da_kw = dict(group=2, diff_scale=1.0 - lam_init, t_out=t_out)
        ml_kw = dict(group=1, diff_scale=1.0, t_out=t_out)
        lat_kw = dict(t_q=n_lat, q_off=0, t_kv=t, kv_off=0, tk=tk)
        oda = _flash(qda, kda, vda, lam, g_da, None if last else oda, tq=tq_da, **lat_kw, **da_kw)
        oml = _flash(qm, km, vm, lam, g_da, None if last else oml, tq=tq_ml, **lat_kw, **ml_kw)
        if not last:
            ctx_kw = dict(t_q=n_ctx, q_off=n_lat, t_kv=n_ctx, kv_off=n_lat, tq=tc, tk=tc)
            oda = _flash(qda, kda, vda, lam, g_da, oda, **ctx_kw, **da_kw)
            oml = _flash(qm, km, vm, lam, g_da, oml, **ctx_kw, **ml_kw)

        o_f, o_b = _hgrn(hg, jnp.stack([lb[0, l], lb[1, l]]), e_mat, n_lat)

        ghn = jnp.tile(hg_norm_g[l], HG_H)[None]
        bf = lambda a: a.astype(BF16)
        xa = _mix_mlp(xa, oda, oml, o_f, o_b, hg, modl, modc, ghn, g_mlp[l][None], g_final[None], grp,
                      bf(w_out[l]), bf(w_ff1[l]), bf(w_ff2[l]), n_lat, last)
    return xa
```

```python
import functools
import math

import jax
import jax.numpy as jnp
from jax import lax
from jax.experimental import pallas as pl
from jax.experimental.pallas import tpu as pltpu

F32 = jnp.float32
BF16 = jnp.bfloat16

EPS = 1e-6
ROPE_THETA = 10000.0
GRID_W = 64
ROT = 32
DA_H, DA_D = 6, 32
DA_W = DA_H * 2 * DA_D
ML_H, ML_QR, ML_KVR, ML_NOPE, ML_V = 6, 256, 128, 64, 64
ML_W = ML_H * ML_V
HG_H, HG_K, HG_V = 4, 128, 64
HG_KW, HG_VW = HG_H * HG_K, HG_H * HG_V
FORGET_FLOOR = 1e-30
N_MOD = 6
LOG2E = 1.4426950408889634

HG_CHUNK = 64
HG_BATCH = (4, 2, 1)
HG_SUB = 8
LANE = 128
VMEM_LIMIT = 56 * 1024 * 1024
TQ_DA = (1024, 512, 256, 128)
TQ_ML = (2048, 1024, 512, 256, 128)
TK = (768, 512, 384, 256, 128)
VT_ROWS = 80
LOOKAHEAD = 6
FLASH_KEYS = 256
FLASH_ROWS = 256


def _cparams(sem):
    return pltpu.CompilerParams(dimension_semantics=sem, vmem_limit_bytes=VMEM_LIMIT)


def _const_spec(shape):
    nd = len(shape)
    return pl.BlockSpec(shape, lambda *_: (0,) * nd, pipeline_mode=pl.Buffered(1))


def _pick(n, cands):
    for c in cands:
        if n % c == 0:
            return c
    raise ValueError(f"no tile for {n} in {cands}")


def _mod_body(c_ref, w_ref, b_ref, o_ref):
    cc = c_ref[...]
    s = cc * jax.nn.sigmoid(cc)
    o_ref[0] = jnp.dot(s.astype(BF16), w_ref[0].astype(BF16), preferred_element_type=F32) + b_ref[0]


def _modulation(cond, w_mod, b_mod):
    depth, d, nm = w_mod.shape
    r = cond.shape[0]
    tn = _pick(nm, (1536, 1024, 512, 128))
    return pl.pallas_call(
        _mod_body,
        grid=(depth, nm // tn),
        in_specs=[pl.BlockSpec((r, d), lambda l, j: (0, 0)),
                  pl.BlockSpec((1, d, tn), lambda l, j: (l, 0, j)),
                  pl.BlockSpec((1, 1, tn), lambda l, j: (l, 0, j))],
        out_specs=pl.BlockSpec((1, r, tn), lambda l, j: (l, 0, j)),
        out_shape=jax.ShapeDtypeStruct((depth, r, nm), F32),
        compiler_params=_cparams(("arbitrary", "arbitrary")),
        name="modulation",
    )(cond, w_mod, b_mod.reshape(depth, 1, nm))


def _ada_rows(x, g, modl, modc, is_ctx, i_shift, i_scale):
    ms = jnp.mean(x * x, axis=-1, keepdims=True)
    y = x * lax.rsqrt(ms + EPS) * g
    shift = jnp.where(is_ctx, modc[i_shift:i_shift + 1], modl[i_shift:i_shift + 1])
    scale = jnp.where(is_ctx, modc[i_scale:i_scale + 1], modl[i_scale:i_scale + 1])
    return y * (1.0 + scale) + shift


def _rms(x, g):
    return x * lax.rsqrt(jnp.mean(x * x, axis=-1, keepdims=True) + EPS) * g


def _proj_body(n_lat, x_ref, modl_ref, modc_ref, g_ref, rope_ref, w1_ref, w1s_ref, w2_ref, w3_ref,
               gcq_ref, gckv_ref, wuq_ref, wuqs_ref, wkn_ref, wvm_ref,
               qda_ref, kda_ref, vda_ref, qm_ref, km_ref, vm_ref, hg_ref):
    tm = x_ref.shape[1]
    rows = pl.program_id(1) * tm + lax.broadcasted_iota(jnp.int32, (tm, 1), 0)
    is_ctx = rows >= n_lat
    h = _ada_rows(x_ref[0], g_ref[...], modl_ref[0], modc_ref[...], is_ctx, 0, 1).astype(BF16)

    rope = rope_ref[...]
    c4, s4, cq_t, sq_t = (rope[:, k * LANE:(k + 1) * LANE] for k in range(4))
    n1 = w1_ref.shape[1] // LANE
    p1 = jnp.dot(h, w1_ref[...], preferred_element_type=F32)
    p1s = jnp.dot(h, w1s_ref[...], preferred_element_type=F32)
    rot = p1 * jnp.concatenate([c4] * n1, axis=1) + p1s * jnp.concatenate([s4] * n1, axis=1)

    rq_t = (rot[:, :DA_W] * (DA_D ** -0.5 * LOG2E)).T
    row64 = lax.broadcasted_iota(jnp.int32, (2 * DA_D, 1), 0)
    for vh in range(2 * DA_H):
        hh, comp = vh // 2, vh % 2
        keep = (row64 >= DA_D) if comp else (row64 < DA_D)
        qda_ref[0, vh] = jnp.where(keep, rq_t[hh * 2 * DA_D:(hh + 1) * 2 * DA_D], 0.0).astype(BF16)
    for hh in range(DA_H):
        kda_ref[0, hh] = rot[:, DA_W + hh * 2 * DA_D:DA_W + (hh + 1) * 2 * DA_D].astype(BF16)
    kr_slot = rot[:, 2 * DA_W:2 * DA_W + LANE]

    ones_rows = (lax.broadcasted_iota(jnp.int32, (VT_ROWS - ML_V, tm), 0) == 0).astype(F32)
    p2 = jnp.dot(h, w2_ref[...], preferred_element_type=F32)
    v_t = p2[:, :DA_W].T
    for hh in range(DA_H):
        vda_ref[0, hh] = jnp.concatenate([v_t[hh * ML_V:(hh + 1) * ML_V], ones_rows], axis=0).astype(BF16)
    cqn = _rms(p2[:, DA_W:DA_W + ML_QR], gcq_ref[...]).astype(BF16)
    ckvn = _rms(p2[:, DA_W + ML_QR:DA_W + ML_QR + ML_KVR], gckv_ref[...]).astype(BF16)

    qu = jnp.dot(cqn, wuq_ref[...], preferred_element_type=F32)
    qus = jnp.dot(cqn, wuqs_ref[...], preferred_element_type=F32)
    qm = (qu * jnp.concatenate([cq_t] * ML_H, axis=1) + qus * jnp.concatenate([sq_t] * ML_H, axis=1))
    qm_ref[0] = (qm * ((ML_NOPE + ROT) ** -0.5 * LOG2E)).T.reshape(ML_H, LANE, tm).astype(BF16)
    kn = jnp.dot(ckvn, wkn_ref[...], preferred_element_type=F32)
    for hh in range(ML_H):
        km_ref[0, hh] = (kn[:, hh * LANE:(hh + 1) * LANE] + kr_slot).astype(BF16)
    vm_t = jnp.dot(ckvn, wvm_ref[...], preferred_element_type=F32).T
    for hh in range(ML_H):
        vm_ref[0, hh] = jnp.concatenate([vm_t[hh * ML_V:(hh + 1) * ML_V], ones_rows], axis=0).astype(BF16)

    hg_ref[0] = jnp.dot(h, w3_ref[...], preferred_element_type=F32)


def _project(xa, modl, modc, g, rope, wts, n_lat):
    b, t, d = xa.shape
    tm = _pick(t, (768, 384, 256, 128))
    w1, w1s, w2, w3, gcq, gckv, wuq, wuqs, wkn, wvm = wts
    nhg = w3.shape[1]
    rows = lambda hs, w: (jax.ShapeDtypeStruct((b, hs, t, w), BF16),
                          pl.BlockSpec((1, hs, tm, w), lambda bi, i: (bi, 0, i, 0)))
    cols = lambda hs, r: (jax.ShapeDtypeStruct((b, hs, r, t), BF16),
                          pl.BlockSpec((1, hs, r, tm), lambda bi, i: (bi, 0, 0, i)))
    outs = (cols(2 * DA_H, 2 * DA_D), rows(DA_H, 2 * DA_D), cols(DA_H, VT_ROWS),
            cols(ML_H, LANE), rows(ML_H, LANE), cols(ML_H, VT_ROWS),
            (jax.ShapeDtypeStruct((b, t, nhg), F32), pl.BlockSpec((1, tm, nhg), lambda bi, i: (bi, i, 0))))
    out_shapes = tuple(o[0] for o in outs)
    out_specs = tuple(o[1] for o in outs)
    in_specs = [
        pl.BlockSpec((1, tm, d), lambda bi, i: (bi, i, 0)),
        pl.BlockSpec((1, N_MOD, d), lambda bi, i: (bi, 0, 0)),
        _const_spec(modc.shape), _const_spec(g.shape),
        pl.BlockSpec((tm, rope.shape[1]), lambda bi, i: (i, 0)),
    ] + [_const_spec(w.shape) for w in wts]
    return pl.pallas_call(
        functools.partial(_proj_body, n_lat),
        grid=(b, t // tm),
        in_specs=in_specs, out_specs=out_specs, out_shape=out_shapes,
        compiler_params=_cparams(("arbitrary", "arbitrary")),
        name="project",
    )(xa, modl, modc, g, rope, *wts)


def _flash_body(n_kv, group, diff_scale, rc, lam_ref, g_ref, qt_ref, k_ref, vt_ref, *refs):
    o_ref, m_ref, acc_ref = refs[-3:]
    ki = pl.program_id(2)
    nk = pl.num_programs(2)
    tq = qt_ref.shape[3]

    @pl.when(ki == 0)
    def _():
        m_ref[...] = jnp.full(m_ref.shape, -jnp.inf, F32)
        acc_ref[...] = jnp.zeros(acc_ref.shape, F32)

    tk = k_ref.shape[2]
    ks = min(FLASH_KEYS, tk)
    chains = [(hh, c0, k0) for hh in range(n_kv) for c0 in range(0, group * tq, rc) for k0 in range(0, tk, ks)]

    def scores(hh, c0, k0):
        qt = qt_ref[0, hh * group + c0 // tq, :, c0 % tq:c0 % tq + rc]
        return jnp.dot(k_ref[0, hh, k0:k0 + ks, :], qt, preferred_element_type=F32)

    def softmax(hh, c0, k0, s):
        cols = slice(c0, c0 + rc)
        m_prev = m_ref[hh, :, cols]
        m_new = jnp.maximum(m_prev, jnp.max(s, axis=0, keepdims=True))
        m_ref[hh, :, cols] = m_new
        return jnp.exp2(s - m_new[0:1]).astype(BF16), jnp.exp2(m_prev - m_new)[0:1]

    def accumulate(hh, c0, k0, p, alpha):
        cols = slice(c0, c0 + rc)
        acc_ref[hh, :, cols] = alpha * acc_ref[hh, :, cols] + jnp.dot(
            vt_ref[0, hh, :, k0:k0 + ks], p, preferred_element_type=F32)

    n = len(chains)
    s_q = {i: scores(*chains[i]) for i in range(min(LOOKAHEAD, n))}
    p_q = {0: softmax(*chains[0], s_q.pop(0))}
    for ci in range(n):
        if ci + LOOKAHEAD < n:
            s_q[ci + LOOKAHEAD] = scores(*chains[ci + LOOKAHEAD])
        if ci + 1 < n:
            p_q[ci + 1] = softmax(*chains[ci + 1], s_q.pop(ci + 1))
        accumulate(*chains[ci], *p_q.pop(ci))

    @pl.when(ki == nk - 1)
    def _():
        outs = []
        for hh in range(n_kv):
            acc = acc_ref[hh]
            o = acc[:ML_V] / acc[ML_V:ML_V + 1]
            if group == 2:
                od = o[:, :tq] - lam_ref[0, 0] * o[:, tq:]
                ms = jnp.mean(od * od, axis=0, keepdims=True)
                o = od * lax.rsqrt(ms + EPS) * jnp.concatenate([g_ref[...]] * (tq // LANE), axis=1) * diff_scale
            outs.append(o.T)
        o_ref[0] = jnp.concatenate(outs, axis=-1).astype(o_ref.dtype)


def _flash(qt, k, vt, lam, g, into=None, *, group, diff_scale, t_q, q_off, t_kv, kv_off, tq, tk, t_out):
    b, hq, dk, _ = qt.shape
    n_kv = k.shape[1]
    assert hq == n_kv * group and q_off % tq == 0 and kv_off % tk == 0
    qo, ko = q_off // tq, kv_off // tk
    extra = () if into is None else (into,)
    return pl.pallas_call(
        functools.partial(_flash_body, n_kv, group, diff_scale, min(FLASH_ROWS, tq)),
        grid=(b, t_q // tq, t_kv // tk),
        in_specs=[
            pl.BlockSpec(memory_space=pltpu.SMEM),
            _const_spec(g.shape),
            pl.BlockSpec((1, hq, dk, tq), lambda bi, qi, ki: (bi, 0, 0, qi + qo)),
            pl.BlockSpec((1, n_kv, tk, dk), lambda bi, qi, ki: (bi, 0, ki + ko, 0)),
            pl.BlockSpec((1, n_kv, VT_ROWS, tk), lambda bi, qi, ki: (bi, 0, 0, ki + ko)),
        ] + [pl.BlockSpec(memory_space=pl.ANY)] * len(extra),
        out_specs=pl.BlockSpec((1, tq, n_kv * ML_V), lambda bi, qi, ki: (bi, qi + qo, 0)),
        out_shape=jax.ShapeDtypeStruct((b, t_out, n_kv * ML_V), BF16),
        input_output_aliases={5: 0} if extra else {},
        scratch_shapes=[pltpu.VMEM((n_kv, 8, group * tq), F32),
                        pltpu.VMEM((n_kv, VT_ROWS, group * tq), F32)],
        compiler_params=_cparams(("arbitrary", "arbitrary", "arbitrary")),
        name="flash_da" if group == 2 else "flash_mla",
    )(lam, g, qt, k, vt, *extra)


def _cumsum_rows(x):
    n = x.shape[0]
    hi = x.astype(BF16)
    r1 = x - hi.astype(F32)
    mid = r1.astype(BF16)
    lo = (r1 - mid.astype(F32)).astype(BF16)
    tril = (lax.broadcasted_iota(jnp.int32, (n, n), 0) >= lax.broadcasted_iota(jnp.int32, (n, n), 1)).astype(BF16)
    return jnp.dot(jnp.concatenate([tril] * 3, axis=1), jnp.concatenate([hi, mid, lo], axis=0),
                   preferred_element_type=F32)


def _dot_nt(a, b):
    return lax.dot_general(a, b, (((1,), (1,)), ((), ())), preferred_element_type=F32)


def _pad_rows(x, start, total):
    parts = []
    if start:
        parts.append(jnp.zeros((start, x.shape[1]), x.dtype))
    parts.append(x)
    if total - start - x.shape[0]:
        parts.append(jnp.zeros((total - start - x.shape[0], x.shape[1]), x.dtype))
    return jnp.concatenate(parts, axis=0) if len(parts) > 1 else x


def _hgrn_body(nb, qf_ref, zf_ref, vf_ref, qb_ref, zb_ref, vb_ref, lb_ref, e_ref, of_ref, ob_ref, stf_ref, stb_ref):
    @pl.when(pl.program_id(1) == 0)
    def _():
        stf_ref[...] = jnp.zeros(stf_ref.shape, F32)
        stb_ref[...] = jnp.zeros(stb_ref.shape, F32)

    c, nsub, kw = HG_CHUNK, HG_CHUNK // HG_SUB, HG_KW
    streams = [(bi, d) for bi in range(nb) for d in range(2)]
    n = len(streams)
    qs = [(qb_ref if d else qf_ref)[bi] * jnp.sign(1.0 - lb_ref[d:d + 1]) for bi, d in streams]
    vs = [(vb_ref if d else vf_ref)[bi] for bi, d in streams]
    st_refs = [(stb_ref if d else stf_ref) for bi, d in streams]
    sts = [st_refs[i][bi] for i, (bi, d) in enumerate(streams)]

    z = jnp.concatenate([(zb_ref if d else zf_ref)[bi] for bi, d in streams], axis=1)
    lb = jnp.concatenate([lb_ref[d:d + 1] for bi, d in streams], axis=1)
    f = lb + (1.0 - lb) * jax.nn.sigmoid(z)
    kk_all = (1.0 - lb) * jax.nn.sigmoid(-z)
    logf = jnp.log(jnp.maximum(f, FORGET_FLOOR))
    bf = _cumsum_rows(logf)
    kks, bs, b_ends = [], [], []
    for i, (bi, d) in enumerate(streams):
        sl = slice(i * kw, (i + 1) * kw)
        kks.append(kk_all[:, sl])
        b = (bf[c - 1:c, sl] - bf[:, sl] + logf[:, sl]) * LOG2E if d else bf[:, sl] * LOG2E
        bs.append(b)
        b_ends.append(b[0:1] if d else b[c - 1:c])
    cs = [jnp.log2(jnp.abs(kks[i])) - bs[i] for i in range(n)]

    def block_diag(st):
        st = st.astype(BF16)
        zero = jnp.zeros((HG_V, HG_K), BF16)
        return jnp.concatenate([jnp.concatenate(
            [st[:, hh * HG_K:(hh + 1) * HG_K] if j == hh else zero for j in range(HG_H)], axis=1)
            for hh in range(HG_H)], axis=0)

    outs = [_dot_nt((qs[i] * jnp.exp2(bs[i])).astype(BF16), block_diag(sts[i])) for i in range(n)]

    head_r = lax.broadcasted_iota(jnp.int32, (HG_H * c, 1), 0) // c
    same_blk = head_r == lax.broadcasted_iota(jnp.int32, (1, HG_H * c), 1) // c
    v_keep = head_r == lax.broadcasted_iota(jnp.int32, (1, HG_VW), 1) // HG_V
    a_mats = []
    for i, (bi, d) in enumerate(streams):
        q, b, cc = qs[i], bs[i], cs[i]
        q_slots, k_slots = [], []
        for j in (range(nsub - 1) if d else range(1, nsub)):
            r0 = j * HG_SUB
            if d:
                m, k0, k1 = b[r0 + HG_SUB:r0 + HG_SUB + 1], r0 + HG_SUB, c
            else:
                m, k0, k1 = b[r0 - 1:r0], 0, r0
            q_slots.append(_pad_rows(q[r0:r0 + HG_SUB] * jnp.exp2(b[r0:r0 + HG_SUB] - m), r0, c).astype(BF16))
            k_slots.append(_pad_rows(jnp.exp2(m + cc[k0:k1]), k0, c).astype(BF16))
        stack = lambda slots: jnp.concatenate(
            [jnp.concatenate([x[:, hh * HG_K:(hh + 1) * HG_K] for x in slots], axis=1) for hh in range(HG_H)], axis=0)
        a_mats.append(_dot_nt(stack(q_slots), stack(k_slots)))

    terms = []
    for i in range(n):
        q, b, cc = qs[i], bs[i], cs[i]
        for j in range(nsub):
            r0 = j * HG_SUB
            qb, bb, cb = q[r0:r0 + HG_SUB], b[r0:r0 + HG_SUB], cc[r0:r0 + HG_SUB]
            terms += [(qb * jnp.exp2(bb + cb[s:s + 1])).astype(BF16) for s in range(HG_SUB)]
    a_bc = jnp.dot(jnp.concatenate(terms, axis=0), e_ref[...], preferred_element_type=F32)

    for i in range(n):
        v_stack = jnp.where(v_keep, jnp.concatenate([vs[i]] * HG_H, axis=0), 0.0).astype(BF16)
        o_stack = jnp.dot(jnp.where(same_blk, a_mats[i], 0.0).astype(BF16), v_stack, preferred_element_type=F32)
        for hh in range(HG_H):
            outs[i] = outs[i] + o_stack[hh * c:(hh + 1) * c]

    pair = lax.broadcasted_iota(jnp.int32, (HG_SUB * HG_SUB, 1), 0)
    keeps = [jnp.broadcast_to((pair % HG_SUB <= pair // HG_SUB) if d else (pair % HG_SUB >= pair // HG_SUB),
                              (HG_SUB * HG_SUB, HG_VW)) for d in range(2)]
    for i, (bi, d) in enumerate(streams):
        diag = []
        for j in range(nsub):
            r0 = j * HG_SUB
            base = (i * nsub + j) * HG_SUB * HG_SUB
            blk = jnp.where(keeps[d], a_bc[base:base + HG_SUB * HG_SUB], 0.0)
            vb = vs[i][r0:r0 + HG_SUB]
            od = blk[0:HG_SUB] * vb[0:1]
            for s in range(1, HG_SUB):
                od = od + blk[s * HG_SUB:(s + 1) * HG_SUB] * vb[s:s + 1]
            diag.append(od)
        (ob_ref if d else of_ref)[bi] = outs[i] + jnp.concatenate(diag, axis=0)

    for i, (bi, d) in enumerate(streams):
        kd = jnp.exp2(b_ends[i] + cs[i])
        upd = jnp.dot(vs[i].T.astype(BF16), kd.astype(BF16), preferred_element_type=F32)
        upd = jnp.concatenate([upd[hh * HG_V:(hh + 1) * HG_V, hh * HG_K:(hh + 1) * HG_K] for hh in range(HG_H)], axis=1)
        st_refs[i][bi] = sts[i] * jnp.exp2(b_ends[i]) + upd


def _hgrn(hg, lb2, e_mat, n_lat):
    b, t, _ = hg.shape
    c = HG_CHUNK
    nb = _pick(b, HG_BATCH)
    nc, nlc = t // c, n_lat // c
    fwd = lambda i: (i + nlc) % nc
    bwd = lambda i: nc - 1 - i
    v_blk = 3 * HG_KW // HG_VW
    return pl.pallas_call(
        functools.partial(_hgrn_body, nb),
        grid=(b // nb, nc),
        in_specs=[
            pl.BlockSpec((nb, c, HG_KW), lambda bi, i: (bi, fwd(i), 0)),
            pl.BlockSpec((nb, c, HG_KW), lambda bi, i: (bi, fwd(i), 1)),
            pl.BlockSpec((nb, c, HG_VW), lambda bi, i: (bi, fwd(i), v_blk)),
            pl.BlockSpec((nb, c, HG_KW), lambda bi, i: (bi, bwd(i), 0)),
            pl.BlockSpec((nb, c, HG_KW), lambda bi, i: (bi, bwd(i), 2)),
            pl.BlockSpec((nb, c, HG_VW), lambda bi, i: (bi, bwd(i), v_blk)),
            _const_spec(lb2.shape), _const_spec(e_mat.shape),
        ],
        out_specs=(pl.BlockSpec((nb, c, HG_VW), lambda bi, i: (bi, fwd(i), 0)),
                   pl.BlockSpec((nb, c, HG_VW), lambda bi, i: (bi, bwd(i), 0))),
        out_shape=(jax.ShapeDtypeStruct((b, t, HG_VW), F32), jax.ShapeDtypeStruct((b, t, HG_VW), F32)),
        scratch_shapes=[pltpu.VMEM((nb, HG_V, HG_KW), F32), pltpu.VMEM((nb, HG_V, HG_KW), F32)],
        compiler_params=_cparams(("arbitrary", "arbitrary")),
        name="hgrn2",
    )(hg, hg, hg, hg, hg, hg, lb2, e_mat)


def _mix_mlp_body(n_lat, ff_chunk, final, x_ref, oda_ref, oml_ref, of_ref, ob_ref, gate_ref, modl_ref, modc_ref,
                  ghn_ref, gmlp_ref, gfin_ref, grp_ref, wo_ref, w1_ref, w2_ref, o_ref):
    tm = x_ref.shape[1]
    rows = pl.program_id(1) * tm + lax.broadcasted_iota(jnp.int32, (tm, 1), 0)
    is_ctx = rows >= n_lat
    modl, modc = modl_ref[0], modc_ref[...]

    oh = of_ref[0] + ob_ref[0]
    sq = oh * oh
    sq_hi = sq.astype(BF16)
    sq_lo = (sq - sq_hi.astype(F32)).astype(BF16)
    ms = (jnp.dot(sq_hi, grp_ref[...], preferred_element_type=F32)
          + jnp.dot(sq_lo, grp_ref[...], preferred_element_type=F32)) * (1.0 / HG_V)
    gt = gate_ref[0]
    yh = oh * lax.rsqrt(ms + EPS) * ghn_ref[...] * (gt * jax.nn.sigmoid(gt))

    y = jnp.concatenate([oda_ref[0], oml_ref[0], yh.astype(BF16)], axis=1)
    mix = jnp.dot(y, wo_ref[...], preferred_element_type=F32)
    g_mix = jnp.where(is_ctx, modc[2:3], modl[2:3])
    x1 = x_ref[0] + g_mix * mix

    hn = _ada_rows(x1, gmlp_ref[...], modl, modc, is_ctx, 3, 4).astype(BF16)
    dff = w1_ref.shape[1]
    acc = jnp.zeros(x1.shape, F32)
    for j in range(dff // ff_chunk):
        u = jnp.dot(hn, w1_ref[:, j * ff_chunk:(j + 1) * ff_chunk], preferred_element_type=F32)
        u = jnp.maximum(u, 0.0)
        acc = acc + jnp.dot((u * u).astype(BF16), w2_ref[j * ff_chunk:(j + 1) * ff_chunk, :],
                            preferred_element_type=F32)
    g_mlp = jnp.where(is_ctx, modc[5:6], modl[5:6])
    x2 = x1 + g_mlp * acc
    if final:
        x2 = _rms(x2, gfin_ref[...])
    o_ref[0] = x2


def _mix_mlp(xa, oda, oml, o_f, o_b, hg, modl, modc, ghn, gmlp, gfin, grp, wo, w1, w2, n_lat, final):
    b, t, d = xa.shape
    t_out = n_lat if final else t
    tm = _pick(t_out, (512, 384, 256, 128)) if final else _pick(t, (384, 256, 128))
    gate_blk = (3 * HG_KW + HG_VW) // HG_VW
    row = lambda w: pl.BlockSpec((1, tm, w), lambda bi, i: (bi, i, 0))
    consts = (modc, ghn, gmlp, gfin, grp, wo, w1, w2)
    return pl.pallas_call(
        functools.partial(_mix_mlp_body, n_lat, 1024, final),
        grid=(b, t_out // tm),
        in_specs=[row(d), row(DA_W), row(ML_W), row(HG_VW), row(HG_VW),
                  pl.BlockSpec((1, tm, HG_VW), lambda bi, i: (bi, i, gate_blk)),
                  pl.BlockSpec((1, N_MOD, d), lambda bi, i: (bi, 0, 0))]
                 + [_const_spec(a.shape) for a in consts],
        out_specs=row(d),
        out_shape=jax.ShapeDtypeStruct((b, t_out, d), F32),
        compiler_params=_cparams(("arbitrary", "arbitrary")),
        name="mix_mlp",
    )(xa, oda, oml, o_f, o_b, hg, modl, *consts)


def _swap_cols(w):
    n = w.shape[-1]
    j = jnp.arange(n)
    src = (j // 16) * 16 + (j % 16 + 8) % 16
    return w[..., src]


def _slot_cols(w, width):
    k, n = w.shape
    hs = n // width
    return jnp.pad(w.reshape(k, hs, width), ((0, 0), (0, 0), (0, LANE - width))).reshape(k, hs * LANE)


def _layer_weights(w_in, w_uq, w_ukv):
    o = 0
    parts = {}
    for name, wdt in (("da_q", DA_W), ("da_k", DA_W), ("da_v", DA_W), ("cq", ML_QR), ("ckv", ML_KVR), ("kr", ROT),
                      ("hq", HG_KW), ("hzf", HG_KW), ("hzb", HG_KW), ("hv", HG_VW), ("hgate", HG_VW)):
        parts[name] = w_in[:, o:o + wdt]
        o += wdt
    d = w_in.shape[0]
    w1 = jnp.concatenate([parts["da_q"], parts["da_k"], jnp.zeros((d, ML_NOPE), F32), parts["kr"],
                          jnp.zeros((d, LANE - ML_NOPE - ROT), F32)], axis=1)
    w2 = jnp.concatenate([parts["da_v"], parts["cq"], parts["ckv"]], axis=1)
    w3 = jnp.concatenate([parts[n] for n in ("hq", "hzf", "hzb", "hv", "hgate")], axis=1)
    uq = w_uq.reshape(ML_QR, ML_H, ML_NOPE + ROT)
    uq_sw = jnp.concatenate([jnp.zeros((ML_QR, ML_H, ML_NOPE), F32), _swap_cols(uq[..., ML_NOPE:])], axis=-1)
    pad_q = lambda a: jnp.pad(a, ((0, 0), (0, 0), (0, LANE - ML_NOPE - ROT))).reshape(ML_QR, ML_H * LANE)
    ukv = w_ukv.reshape(ML_KVR, ML_H, ML_NOPE + ML_V)
    wkn = _slot_cols(ukv[..., :ML_NOPE].reshape(ML_KVR, ML_H * ML_NOPE), ML_NOPE)
    wvm = ukv[..., ML_NOPE:].reshape(ML_KVR, ML_H * ML_V)
    bf = lambda a: a.astype(BF16)
    return bf(w1), bf(_swap_cols(w1)), bf(w2), bf(w3), bf(pad_q(uq)), bf(pad_q(uq_sw)), bf(wkn), bf(wvm)


def _rope_tables(n_lat, n_ctx):
    rows = n_lat // GRID_W
    row = jnp.repeat(jnp.arange(rows, dtype=F32), GRID_W)
    col = jnp.tile(jnp.arange(GRID_W, dtype=F32), rows)
    n_freq = ROT // 4
    inv = ROPE_THETA ** (-jnp.arange(n_freq, dtype=F32) / n_freq)
    ar, ac = row[:, None] * inv, col[:, None] * inv
    c32 = jnp.concatenate([jnp.cos(ar), jnp.cos(ar), jnp.cos(ac), jnp.cos(ac)], axis=1)
    s32 = jnp.concatenate([-jnp.sin(ar), jnp.sin(ar), -jnp.sin(ac), jnp.sin(ac)], axis=1)
    c32 = jnp.concatenate([c32, jnp.ones((n_ctx, ROT), F32)], axis=0)
    s32 = jnp.concatenate([s32, jnp.zeros((n_ctx, ROT), F32)], axis=0)
    t = n_lat + n_ctx
    one, zero = jnp.ones((t, ML_NOPE), F32), jnp.zeros((t, ML_NOPE), F32)
    pad = jnp.zeros((t, LANE - ML_NOPE - ROT), F32)
    return jnp.concatenate([jnp.tile(c32, (1, LANE // ROT)), jnp.tile(s32, (1, LANE // ROT)),
                            one, c32, pad, zero, s32, pad], axis=1)


def kernel(x, c, ctx, c_ctx, w_mod, b_mod, g_mix, g_mlp, w_in, w_out, da_lambda, da_subln_g, mla_g_cq, mla_g_ckv,
           mla_w_uq, mla_w_ukv, hg_lower_bounds, hg_norm_g, w_ff1, w_ff2, g_final):
    bsz, n_lat, d = x.shape
    n_ctx = ctx.shape[1]
    depth = w_in.shape[0]
    t = n_lat + n_ctx

    lb = jax.nn.softmax(hg_lower_bounds.astype(F32), axis=1)
    lb = jnp.cumsum(lb, axis=1) - lb[:, :1]
    cond = jnp.concatenate([c, c_ctx[None]], axis=0)
    mod = _modulation(cond, w_mod, b_mod).reshape(depth, bsz + 1, N_MOD, d)
    rope = _rope_tables(n_lat, n_ctx)

    grp = (jnp.arange(HG_VW)[:, None] // HG_V == jnp.arange(HG_VW)[None, :] // HG_V).astype(BF16)
    e_mat = (jnp.arange(HG_KW)[:, None] // HG_K == jnp.arange(HG_VW)[None, :] // HG_V).astype(BF16)

    tq_da = _pick(n_lat, TQ_DA)
    tq_ml = _pick(n_lat, TQ_ML)
    tk = _pick(t, TK)
    tc = _pick(n_ctx, (256, 128))

    xa = jnp.concatenate([x, ctx], axis=1)
    oda = jnp.zeros((bsz, t, DA_W), BF16)
    oml = jnp.zeros((bsz, t, ML_W), BF16)
    for l in range(depth):
        last = l == depth - 1
        lam_init = 0.8 - 0.6 * math.exp(-0.3 * l)
        lam = (jnp.exp(jnp.sum(da_lambda[l, 0] * da_lambda[l, 1]))
               - jnp.exp(jnp.sum(da_lambda[l, 2] * da_lambda[l, 3])) + lam_init).astype(F32).reshape(1, 1)
        modl, modc = mod[l, :bsz], mod[l, bsz]
        w1, w1s, w2, w3, wuq, wuqs, wkn, wvm = _layer_weights(w_in[l], mla_w_uq[l], mla_w_ukv[l])
        wts = (w1, w1s, w2, w3, mla_g_cq[l][None], mla_g_ckv[l][None], wuq, wuqs, wkn, wvm)
        qda, kda, vda, qm, km, vm, hg = _project(xa, modl, modc, g_mix[l][None], rope, wts, n_lat)

        g_da = jnp.broadcast_to(da_subln_g[l][:, None], (2 * DA_D, LANE))
        t_out = n_lat if last else t
        da_kw = dict(group=2, diff_scale=1.0 - lam_init, t_out=t_out)
        ml_kw = dict(group=1, diff_scale=1.0, t_out=t_out)
        lat_kw = dict(t_q=n_lat, q_off=0, t_kv=t, kv_off=0, tk=tk)
        oda = _flash(qda, kda, vda, lam, g_da, None if last else oda, tq=tq_da, **lat_kw, **da_kw)
        oml = _flash(qm, km, vm, lam, g_da, None if last else oml, tq=tq_ml, **lat_kw, **ml_kw)
        if not last:
            ctx_kw = dict(t_q=n_ctx, q_off=n_lat, t_kv=n_ctx, kv_off=n_lat, tq=tc, tk=tc)
            oda = _flash(qda, kda, vda, lam, g_da, oda, **ctx_kw, **da_kw)
            oml = _flash(qm, km, vm, lam, g_da, oml, **ctx_kw, **ml_kw)

        o_f, o_b = _hgrn(hg, jnp.stack([lb[0, l], lb[1, l]]), e_mat, n_lat)

        ghn = jnp.tile(hg_norm_g[l], HG_H)[None]
        bf = lambda a: a.astype(BF16)
        xa = _mix_mlp(xa, oda, oml, o_f, o_b, hg, modl, modc, ghn, g_mlp[l][None], g_final[None], grp,
                      bf(w_out[l]), bf(w_ff1[l]), bf(w_ff2[l]), n_lat, last)
    return xa
```
